```python
import math, functools
import jax, jax.numpy as jnp
from jax import lax
import numpy as np

D_MODEL = 2048
BATCH = 4
SEQ = 2048
DEPTH = 4
DEC_BATCH = 32
DEC_SEQ = 8
PAST_LEN = 16384
PAGE_SIZE = 128

CHUNK = 128
A_WIDTH = D_MODEL // 2
A_GROUPS = 4
A_GROUP_DIM = A_WIDTH // A_GROUPS
HEAD_DIM = 64
B_WIDTH = D_MODEL - A_WIDTH
N_HEADS = B_WIDTH // HEAD_DIM
N_KV_HEADS = max(1, N_HEADS // 8)
GQA = N_HEADS // N_KV_HEADS
KV_WIDTH = N_KV_HEADS * HEAD_DIM
WINDOW = 128
MIX_WIDTH = A_WIDTH + B_WIDTH
IN_COLS = 2 * A_WIDTH + B_WIDTH + 2 * KV_WIDTH
D_FF = 128 * math.ceil(8 * D_MODEL / 3 / 128)
N_EXPERTS = 8
TOP_K = 2
D_FF_EXPERT = 7 * D_MODEL // 2
N_DENSE = (DEPTH + 1) // 2
N_MOE = DEPTH // 2
ALPHA = (2.0 * DEPTH) ** 0.25
BETA = (8.0 * DEPTH) ** -0.25
LN_EPS = 1e-5
NEG_INF = -1e30

kernel_name = 'hybrid_chunkmlp_swa_sink_deepnorm_adaln_step'


def layer_norm(x, g, b):
    xf = x.astype(jnp.float32)
    mu = jnp.mean(xf, axis=-1, keepdims=True)
    var = jnp.mean(jnp.square(xf - mu), axis=-1, keepdims=True)
    return ((xf - mu) * lax.rsqrt(var + LN_EPS) * g + b).astype(x.dtype)


def adaln(c, w, b):
    m = jax.nn.silu(c) @ w + b
    return jnp.split(m[:, None, :], 6, axis=-1)


def modulate(x, shift, scale):
    return x * (1 + scale) + shift


def group_layer_norm(v, g, b):
    n, L = v.shape[:2]
    vg = v.reshape(n, L, A_GROUPS, A_GROUP_DIM)
    return layer_norm(vg, g, b).reshape(n, L, A_WIDTH)


def chunk_spatial_gating(u, v, w_s, b_s):
    n, L, _ = u.shape
    Lp = -(-L // CHUNK) * CHUNK
    if Lp != L:
        pad = ((0, 0), (0, Lp - L), (0, 0))
        u, v = jnp.pad(u, pad), jnp.pad(v, pad)
    nc = Lp // CHUNK
    vg = v.reshape(n, nc, CHUNK, A_GROUPS, A_GROUP_DIM)
    ug = u.reshape(n, nc, CHUNK, A_GROUPS, A_GROUP_DIM)
    causal = jnp.tril(jnp.ones((CHUNK, CHUNK), dtype=bool))
    w = jnp.where(causal[None], w_s, jnp.zeros_like(w_s))
    s = jnp.einsum('gts,ncsgd->nctgd', w.astype(v.dtype), vg) + b_s.T[None, None, :, :, None]
    return (ug * s).reshape(n, Lp, A_WIDTH)[:, :L]


def sink_attention(q, k, v, mask, sinks):
    scores = jnp.einsum('...qkgd,...skd->...kgqs', q, k).astype(jnp.float32) * (HEAD_DIM ** -0.5)
    scores = jnp.where(mask[..., None, None, :, :], scores, NEG_INF)
    sink = jnp.broadcast_to(sinks.astype(jnp.float32)[:, :, None, None], scores.shape[:-1] + (1,))
    p = jax.nn.softmax(jnp.concatenate([scores, sink], axis=-1), axis=-1)[..., :-1]
    return jnp.einsum('...kgqs,...skd->...qkgd', p.astype(v.dtype), v)


def prompt_window_attention(q, k, v, sinks):
    n, L = q.shape[:2]
    Lp = -(-L // WINDOW) * WINDOW
    qp, kp, vp = q, k, v
    if Lp != L:
        qp = jnp.pad(q, ((0, 0), (0, Lp - L), (0, 0), (0, 0), (0, 0)))
        kp = jnp.pad(k, ((0, 0), (0, Lp - L), (0, 0), (0, 0)))
        vp = jnp.pad(v, ((0, 0), (0, Lp - L), (0, 0), (0, 0)))
    nb = Lp // WINDOW
    qb = qp.reshape(n, nb, WINDOW, N_KV_HEADS, GQA, HEAD_DIM)
    kb = kp.reshape(n, nb, WINDOW, N_KV_HEADS, HEAD_DIM)
    vb = vp.reshape(n, nb, WINDOW, N_KV_HEADS, HEAD_DIM)
    prev = lambda t: jnp.concatenate([jnp.zeros_like(t[:, :1]), t[:, :-1]], axis=1)
    kk = jnp.concatenate([prev(kb), kb], axis=2)
    vv = jnp.concatenate([prev(vb), vb], axis=2)
    blk = jnp.arange(nb)[:, None, None]
    a = jnp.arange(WINDOW)[None, :, None]
    j = jnp.arange(2 * WINDOW)[None, None, :]
    d = a + WINDOW - j
    mask = (d >= 0) & (d <= WINDOW) & (blk * WINDOW + j - WINDOW >= 0)
    out = sink_attention(qb, kk, vv, mask, sinks).reshape(n, Lp, B_WIDTH)[:, :L]
    wl = min(WINDOW, L)
    return out, k[:, L - wl:], v[:, L - wl:]


def sample_window_attention(q, k, v, cache_k, cache_v, sinks):
    n, ds = q.shape[:2]
    w = cache_k.shape[1]
    kk = jnp.concatenate([cache_k.astype(k.dtype), k], axis=1)
    vv = jnp.concatenate([cache_v.astype(v.dtype), v], axis=1)
    qpos = jnp.arange(ds)[:, None]
    kpos = jnp.concatenate([jnp.arange(w) - w, jnp.arange(ds)])[None, :]
    d = qpos - kpos
    mask = (d >= 0) & (d <= WINDOW)
    out = sink_attention(q, kk, vv, mask, sinks).reshape(n, ds, B_WIDTH)
    return out, kk[:, -w:], vv[:, -w:]


def mixing_sublayer(h, attn_fn, w_in, b_in, g_v, b_v, w_s, b_s, w_out, b_out):
    n, L, _ = h.shape
    proj = h @ w_in + b_in
    o1 = 2 * A_WIDTH
    o2 = o1 + B_WIDTH
    o3 = o2 + KV_WIDTH
    uv = jax.nn.gelu(proj[..., :o1])
    u = uv[..., :A_WIDTH]
    v = group_layer_norm(uv[..., A_WIDTH:], g_v, b_v)
    a_out = chunk_spatial_gating(u, v, w_s, b_s)
    q = proj[..., o1:o2].reshape(n, L, N_KV_HEADS, GQA, HEAD_DIM)
    k = proj[..., o2:o3].reshape(n, L, N_KV_HEADS, HEAD_DIM)
    va = proj[..., o3:].reshape(n, L, N_KV_HEADS, HEAD_DIM)
    b_att, win_k, win_v = attn_fn(q, k, va)
    y = jnp.concatenate([a_out, b_att], axis=-1) @ w_out + b_out
    return y, v, win_k, win_v


def swiglu(h, w_gu, w_down):
    g, u = jnp.split(h @ w_gu, 2, axis=-1)
    return (jax.nn.silu(g) * u) @ w_down


def moe_swiglu(h, w_router, w_gu, w_down):
    logits = (h @ w_router).astype(jnp.float32)
    top_v, top_i = lax.top_k(logits, TOP_K)
    wts = jax.nn.softmax(top_v, axis=-1)
    combine = jnp.sum(jax.nn.one_hot(top_i, N_EXPERTS, dtype=jnp.float32) * wts[..., None], axis=-2).astype(h.dtype)
    y = jnp.zeros_like(h)
    for e in range(N_EXPERTS):
        y = y + combine[..., e:e + 1] * swiglu(h, w_gu[e], w_down[e])
    return y


def setup_inputs(seed: int = 0) -> dict:
    key = jax.random.key(seed)
    ks = jax.random.split(key, 32)
    nrm = lambda k, shape, s: jax.random.normal(k, shape, jnp.float32) * s
    wb = min(WINDOW, PAST_LEN)
    D = D_MODEL
    return {
        'x_prompt': nrm(ks[0], (BATCH, SEQ, D), 1.0),
        'x_sample': nrm(ks[1], (DEC_BATCH, DEC_SEQ, D), 1.0),
        'cache_win_k': nrm(ks[2], (DEPTH, DEC_BATCH, wb, N_KV_HEADS, HEAD_DIM), 1.0),
        'cache_win_v': nrm(ks[3], (DEPTH, DEC_BATCH, wb, N_KV_HEADS, HEAD_DIM), 1.0),
        'c_prompt': nrm(ks[4], (BATCH, D), 1.0),
        'c_sample': nrm(ks[5], (DEC_BATCH, D), 1.0),
        'w_ada': nrm(ks[6], (DEPTH, D, 6 * D), 0.2 * D ** -0.5),
        'b_ada': nrm(ks[7], (DEPTH, 6 * D), 0.01),
        'w_in': nrm(ks[8], (DEPTH, D, IN_COLS), D ** -0.5),
        'b_in': nrm(ks[9], (DEPTH, IN_COLS), 0.01),
        'v_norm_g': 1.0 + nrm(ks[10], (DEPTH, A_GROUPS, A_GROUP_DIM), 0.01),
        'v_norm_b': nrm(ks[11], (DEPTH, A_GROUPS, A_GROUP_DIM), 0.01),
        'w_spatial': nrm(ks[12], (DEPTH, A_GROUPS, CHUNK, CHUNK), CHUNK ** -0.5),
        'b_spatial': 1.0 + nrm(ks[13], (DEPTH, A_GROUPS, CHUNK), 0.01),
        'attn_sinks': nrm(ks[14], (DEPTH, N_KV_HEADS, GQA), 0.5),
        'w_out': nrm(ks[15], (DEPTH, MIX_WIDTH, D), BETA * MIX_WIDTH ** -0.5),
        'b_out': nrm(ks[16], (DEPTH, D), 0.01),
        'ln1_g': 1.0 + nrm(ks[17], (DEPTH, D), 0.01),
        'ln1_b': nrm(ks[18], (DEPTH, D), 0.01),
        'ln2_g': 1.0 + nrm(ks[19], (DEPTH, D), 0.01),
        'ln2_b': nrm(ks[20], (DEPTH, D), 0.01),
        'w_ffn_gu': nrm(ks[21], (N_DENSE, D, 2 * D_FF), D ** -0.5),
        'w_ffn_down': nrm(ks[22], (N_DENSE, D_FF, D), BETA * D_FF ** -0.5),
        'w_router': nrm(ks[23], (N_MOE, D, N_EXPERTS), D ** -0.5),
        'w_exp_gu': nrm(ks[24], (N_MOE, N_EXPERTS, D, 2 * D_FF_EXPERT), D ** -0.5),
        'w_exp_down': nrm(ks[25], (N_MOE, N_EXPERTS, D_FF_EXPERT, D), BETA * D_FF_EXPERT ** -0.5),
    }


def reference(x_prompt, x_sample, cache_win_k, cache_win_v, c_prompt, c_sample, w_ada, b_ada, w_in, b_in,
              v_norm_g, v_norm_b, w_spatial, b_spatial, attn_sinks, w_out, b_out, ln1_g, ln1_b, ln2_g, ln2_b,
              w_ffn_gu, w_ffn_down, w_router, w_exp_gu, w_exp_down):

    def layer(l, x, c, attn_fn):
        sh1, sc1, g1, sh2, sc2, g2 = adaln(c, w_ada[l], b_ada[l])
        y, v_rows, wk, wv = mixing_sublayer(modulate(x, sh1, sc1), attn_fn, w_in[l], b_in[l], v_norm_g[l],
                                            v_norm_b[l], w_spatial[l], b_spatial[l], w_out[l], b_out[l])
        x = layer_norm(ALPHA * x + (1 + g1) * y, ln1_g[l], ln1_b[l])
        h = modulate(x, sh2, sc2)
        if l % 2 == 0:
            f = swiglu(h, w_ffn_gu[l // 2], w_ffn_down[l // 2])
        else:
            f = moe_swiglu(h, w_router[l // 2], w_exp_gu[l // 2], w_exp_down[l // 2])
        x = layer_norm(ALPHA * x + (1 + g2) * f, ln2_g[l], ln2_b[l])
        return x, v_rows, wk, wv

    xp, xs = x_prompt, x_sample
    kp_list, vp_list, ks_list, vs_list, cv_list = [], [], [], [], []
    for l in range(DEPTH):
        prompt_attn = functools.partial(prompt_window_attention, sinks=attn_sinks[l])
        sample_attn = functools.partial(sample_window_attention, cache_k=cache_win_k[l],
                                        cache_v=cache_win_v[l], sinks=attn_sinks[l])
        xp, _, kp, vp = layer(l, xp, c_prompt, prompt_attn)
        xs, v_rows_s, ks_, vs_ = layer(l, xs, c_sample, sample_attn)
        kp_list.append(kp)
        vp_list.append(vp)
        ks_list.append(ks_)
        vs_list.append(vs_)
        cv_list.append(v_rows_s)

    win_k_prompt = jnp.stack(kp_list)
    win_v_prompt = jnp.stack(vp_list)
    win_k_sample = jnp.stack(ks_list)
    win_v_sample = jnp.stack(vs_list)
    chunk_v_sample = jnp.stack(cv_list)
    return (xp, xs, win_k_prompt, win_v_prompt, win_k_sample, win_v_sample, chunk_v_sample)
```

```python
import functools

import jax
import jax.numpy as jnp
from jax import lax
from jax.experimental import pallas as pl
from jax.experimental.pallas import tpu as pltpu

F32 = jnp.float32
BF16 = jnp.bfloat16

D_MODEL = 2048
BATCH = 4
SEQ = 2048
DEPTH = 4
DEC_BATCH = 32
DEC_SEQ = 8
CHUNK = 128
A_WIDTH = D_MODEL // 2
A_GROUPS = 4
A_GROUP_DIM = A_WIDTH // A_GROUPS
HEAD_DIM = 64
B_WIDTH = D_MODEL - A_WIDTH
N_HEADS = B_WIDTH // HEAD_DIM
N_KV_HEADS = 2
GQA = N_HEADS // N_KV_HEADS
KV_WIDTH = N_KV_HEADS * HEAD_DIM
WINDOW = 128
IN_COLS = 2 * A_WIDTH + B_WIDTH + 2 * KV_WIDTH
D_FF = 5504
N_EXPERTS = 8
TOP_K = 2
D_FF_EXPERT = 7168
ALPHA = (2.0 * DEPTH) ** 0.25
LN_EPS = 1e-5
NEG_INF = -1e30

M_PROMPT = BATCH * SEQ
M_SAMPLE = DEC_BATCH * DEC_SEQ
M_ROWS = M_PROMPT + M_SAMPLE
ROWS_PER_UNIT = 256
N_PROMPT_UNITS = M_PROMPT // ROWS_PER_UNIT
N_UNITS = M_ROWS // ROWS_PER_UNIT
UNITS_PER_SEQ = SEQ // ROWS_PER_UNIT
ADA_ROWS = 40
P_SHIFT1, P_SCALE1, P_GATE1, P_SHIFT2, P_SCALE2, P_GATE2 = range(6)

D_FF_PAD = 5632
FFN_TILE = 512
FFN_ROW_TILE = 4 * ROWS_PER_UNIT
MOE_FF_TILE = 256
MOE_ROW_TILE = 4 * ROWS_PER_UNIT
N_PAIRS = M_ROWS * TOP_K
MOE_TILES = N_PAIRS // MOE_ROW_TILE + N_EXPERTS
MOE_ROWS = MOE_TILES * MOE_ROW_TILE
MOE_UNITS = MOE_ROWS // ROWS_PER_UNIT

VMEM_LIMIT = 56 * 1024 * 1024


def _params(*sem):
    return pltpu.CompilerParams(dimension_semantics=sem, vmem_limit_bytes=VMEM_LIMIT)


def _dot(a, b):
    return jnp.dot(a, b, preferred_element_type=F32)


def _layer_norm(z, g, b):
    mu = jnp.mean(z, axis=-1, keepdims=True)
    zc = z - mu
    var = jnp.mean(zc * zc, axis=-1, keepdims=True)
    return zc * lax.rsqrt(var + LN_EPS) * g + b


def _gelu_tanh(x):
    return x * (0.5 * (1.0 + jnp.tanh(0.7978845608028654 * (x + 0.044715 * (x * x * x)))))


def _silu(x):
    return x * jax.nn.sigmoid(x)


def _prompt_batch_of_unit(i):
    return jnp.minimum(i // UNITS_PER_SEQ, BATCH - 1)


def _adaln_body(c_ref, w_ref, b_ref, o_ref):
    s = _silu(c_ref[...]).astype(BF16)
    o_ref[...] = _dot(s, w_ref[...].astype(BF16)) + b_ref[...]


def _adaln(c_rows, w_ada, b_ada):
    tn = 1024
    return pl.pallas_call(
        _adaln_body,
        grid=(DEPTH, 6 * D_MODEL // tn),
        in_specs=[pl.BlockSpec((ADA_ROWS, D_MODEL), lambda l, n: (0, 0)),
                  pl.BlockSpec((None, D_MODEL, tn), lambda l, n: (l, 0, n)),
                  pl.BlockSpec((None, 1, tn), lambda l, n: (l, 0, n))],
        out_specs=pl.BlockSpec((None, ADA_ROWS, tn), lambda l, n: (l, 0, n)),
        out_shape=jax.ShapeDtypeStruct((DEPTH, ADA_ROWS, 6 * D_MODEL), F32),
        compiler_params=_params("arbitrary", "arbitrary"),
        name="adaln",
    )(c_rows, w_ada, b_ada.reshape(DEPTH, 1, 6 * D_MODEL))


def _mod0_body(x_ref, mp_ref, ss_ref, cs_ref, h_ref):
    i = pl.program_id(0)
    x = x_ref[...]

    @pl.when(i < N_PROMPT_UNITS)
    def _():
        h_ref[...] = (x * (1 + mp_ref[P_SCALE1:P_SCALE1 + 1, :]) + mp_ref[P_SHIFT1:P_SHIFT1 + 1, :]).astype(BF16)

    @pl.when(i >= N_PROMPT_UNITS)
    def _():
        h_ref[...] = (x * (1 + cs_ref[...]) + ss_ref[...]).astype(BF16)


def _unit_spec(width):
    return pl.BlockSpec((ROWS_PER_UNIT, width), lambda i: (i, 0))


def _modp_spec(layer):
    return pl.BlockSpec((None, None, 6, D_MODEL), lambda i: (layer, _prompt_batch_of_unit(i), 0, 0))


def _mods_spec(layer, p):
    return pl.BlockSpec((None, None, ROWS_PER_UNIT, D_MODEL), lambda i: (layer, p, 0, 0))


def _modulate0(x, modp, mods):
    return pl.pallas_call(
        _mod0_body,
        grid=(N_UNITS,),
        in_specs=[_unit_spec(D_MODEL), _modp_spec(0), _mods_spec(0, P_SHIFT1), _mods_spec(0, P_SCALE1)],
        out_specs=_unit_spec(D_MODEL),
        out_shape=jax.ShapeDtypeStruct((M_ROWS, D_MODEL), BF16),
        compiler_params=_params("arbitrary"),
        name="modulate0",
    )(x, modp, mods, mods)


def _proj_body(h_ref, w_ref, b_ref, gv_ref, bv_ref, u_ref, v_ref, q_ref, kv_ref):
    h = h_ref[...]
    gd = A_GROUP_DIM
    for c in range(A_GROUPS):
        sl = slice(c * gd, (c + 1) * gd)
        z = _dot(h, w_ref[:, sl]) + b_ref[:, sl]
        u_ref[:, sl] = _gelu_tanh(z).astype(BF16)
    for g in range(A_GROUPS):
        sl = slice(A_WIDTH + g * gd, A_WIDTH + (g + 1) * gd)
        z = _gelu_tanh(_dot(h, w_ref[:, sl]) + b_ref[:, sl])
        vn = _layer_norm(z, gv_ref[g:g + 1, :], bv_ref[g:g + 1, :])
        v_ref[:, g * gd:(g + 1) * gd] = vn.astype(BF16)
    o1 = 2 * A_WIDTH
    o2 = o1 + B_WIDTH
    z = _dot(h, w_ref[:, o1:o2]) + b_ref[:, o1:o2]
    q_ref[...] = (z * (HEAD_DIM ** -0.5)).astype(BF16)
    kv_ref[...] = _dot(h, w_ref[:, o2:]) + b_ref[:, o2:]


def _proj(h, w_bf, b, gv, bv):
    const = lambda shape: pl.BlockSpec(shape, lambda i: (0,) * len(shape))
    return pl.pallas_call(
        _proj_body,
        grid=(N_UNITS,),
        in_specs=[_unit_spec(D_MODEL), const((D_MODEL, IN_COLS)), const((1, IN_COLS)),
                  const((A_GROUPS, A_GROUP_DIM)), const((A_GROUPS, A_GROUP_DIM))],
        out_specs=[_unit_spec(A_WIDTH), _unit_spec(A_WIDTH), _unit_spec(B_WIDTH), _unit_spec(2 * KV_WIDTH)],
        out_shape=[jax.ShapeDtypeStruct((M_ROWS, A_WIDTH), BF16), jax.ShapeDtypeStruct((M_ROWS, A_WIDTH), BF16),
                   jax.ShapeDtypeStruct((M_ROWS, B_WIDTH), BF16), jax.ShapeDtypeStruct((M_ROWS, 2 * KV_WIDTH), F32)],
        compiler_params=_params("arbitrary"),
        name="proj",
    )(h, w_bf, b.reshape(1, IN_COLS), gv, bv)


def _mix_prompt_body(sink_ref, u_ref, v_ref, q_ref, kvc_ref, kvp_ref, ws_ref, bs_ref, o_ref):
    blk = pl.program_id(1)
    gd = A_GROUP_DIM
    tr = lax.broadcasted_iota(jnp.int32, (CHUNK, CHUNK), 0)
    tc = lax.broadcasted_iota(jnp.int32, (CHUNK, CHUNK), 1)
    for g in range(A_GROUPS):
        sl = slice(g * gd, (g + 1) * gd)
        w = jnp.where(tr >= tc, ws_ref[g], 0.0).astype(BF16)
        s = _dot(w, v_ref[:, sl]) + bs_ref[:, g:g + 1]
        o_ref[:, sl] = (u_ref[:, sl].astype(F32) * s).astype(BF16)

    kvc = kvc_ref[...]
    kvp = kvp_ref[...]
    row = lax.broadcasted_iota(jnp.int32, (WINDOW, 2 * WINDOW), 0)
    col = lax.broadcasted_iota(jnp.int32, (WINDOW, 2 * WINDOW), 1)
    dlt = row + WINDOW - col
    mask = (dlt >= 0) & (dlt <= WINDOW) & ((col >= WINDOW) | (blk > 0))
    hd = HEAD_DIM
    for j in range(N_KV_HEADS):
        ks = slice(j * hd, (j + 1) * hd)
        vs = slice(KV_WIDTH + j * hd, KV_WIDTH + (j + 1) * hd)
        kk = jnp.concatenate([kvp[:, ks], kvc[:, ks]], axis=0).astype(BF16)
        vv = jnp.concatenate([kvp[:, vs], kvc[:, vs]], axis=0).astype(BF16)
        for hh in range(GQA):
            h = j * GQA + hh
            qh = q_ref[:, h * hd:(h + 1) * hd]
            s = lax.dot_general(qh, kk, (((1,), (1,)), ((), ())), preferred_element_type=F32)
            s = jnp.where(mask, s, NEG_INF)
            sk = sink_ref[h]
            m = jnp.maximum(jnp.max(s, axis=-1, keepdims=True), sk)
            p = jnp.exp(s - m)
            den = jnp.sum(p, axis=-1, keepdims=True) + jnp.exp(sk - m)
            o = _dot(p.astype(BF16), vv) / den
            o_ref[:, A_WIDTH + h * hd:A_WIDTH + (h + 1) * hd] = o.astype(BF16)


def _mix_prompt(sinks, u, v, q, kv, w_s, b_s_t):
    nb = SEQ // WINDOW
    row_blk = lambda w: pl.BlockSpec((WINDOW, w), lambda b, i, s: (b * nb + i, 0))
    grid_spec = pltpu.PrefetchScalarGridSpec(
        num_scalar_prefetch=1,
        grid=(BATCH, nb),
        in_specs=[row_blk(A_WIDTH), row_blk(A_WIDTH), row_blk(B_WIDTH), row_blk(2 * KV_WIDTH),
                  pl.BlockSpec((WINDOW, 2 * KV_WIDTH), lambda b, i, s: (b * nb + jnp.maximum(i - 1, 0), 0)),
                  pl.BlockSpec((A_GROUPS, CHUNK, CHUNK), lambda b, i, s: (0, 0, 0)),
                  pl.BlockSpec((CHUNK, A_GROUPS), lambda b, i, s: (0, 0))],
        out_specs=pl.BlockSpec((WINDOW, D_MODEL), lambda b, i, s: (b * nb + i, 0)),
    )
    return pl.pallas_call(
        _mix_prompt_body,
        grid_spec=grid_spec,
        out_shape=jax.ShapeDtypeStruct((M_PROMPT, D_MODEL), BF16),
        compiler_params=_params("arbitrary", "arbitrary"),
        name="mix_prompt",
    )(sinks, u, v, q, kv, kv, w_s, b_s_t)


SAMPLE_KEYS = 2 * WINDOW


def _mix_sample_body(u_ref, v_ref, q_ref, kn_ref, ck_ref, cv_ref, ws_ref, bs_ref, sk_ref, a_ref, o_ref):
    gd = A_GROUP_DIM
    ds = DEC_SEQ
    v = v_ref[...].astype(F32)
    u = u_ref[...].astype(F32)
    tr = lax.broadcasted_iota(jnp.int32, (ds, ds), 0)
    tc = lax.broadcasted_iota(jnp.int32, (ds, ds), 1)
    for g in range(A_GROUPS):
        sl = slice(g * gd, (g + 1) * gd)
        w = jnp.where(tr >= tc, ws_ref[g][:ds, :ds], 0.0)
        s = jnp.zeros((ds, gd), F32) + bs_ref[:ds, g:g + 1]
        for t in range(ds):
            s = s + w[:, t:t + 1] * v[t:t + 1, sl]
        a_ref[:, sl] = (u[:, sl] * s).astype(BF16)

    kn = kn_ref[...]
    ck = ck_ref[...]
    cv = cv_ref[...]
    hd = HEAD_DIM
    rows = GQA * ds
    qt = lax.broadcasted_iota(jnp.int32, (rows, SAMPLE_KEYS), 0) & (ds - 1)
    col = lax.broadcasted_iota(jnp.int32, (rows, SAMPLE_KEYS), 1)
    mask = ((col < WINDOW) & (col >= qt)) | ((col >= WINDOW) & (col - WINDOW <= qt))
    pad = jnp.zeros((SAMPLE_KEYS - WINDOW - ds, hd), F32)
    for j in range(N_KV_HEADS):
        ks = slice(j * hd, (j + 1) * hd)
        vs = slice(KV_WIDTH + j * hd, KV_WIDTH + (j + 1) * hd)
        kk = jnp.concatenate([ck[:, ks], kn[:, ks], pad], axis=0).astype(BF16)
        vv = jnp.concatenate([cv[:, ks], kn[:, vs], pad], axis=0).astype(BF16)
        s = lax.dot_general(q_ref[j], kk, (((1,), (1,)), ((), ())), preferred_element_type=F32)
        s = jnp.where(mask, s, NEG_INF)
        sk = sk_ref[j]
        m = jnp.maximum(jnp.max(s, axis=-1, keepdims=True), sk)
        p = jnp.exp(s - m)
        den = jnp.sum(p, axis=-1, keepdims=True) + jnp.exp(sk - m)
        o_ref[j] = (_dot(p.astype(BF16), vv) / den).astype(BF16)


def _mix_sample(u_s, v_s, q_s, kv_s, cache_k, cache_v, w_s, b_s_t, sink_rows):
    rows = GQA * DEC_SEQ
    per_b = lambda *shape: pl.BlockSpec((None,) + shape, lambda b: (b,) + (0,) * len(shape))
    const = lambda *shape: pl.BlockSpec(shape, lambda b: (0,) * len(shape))
    return pl.pallas_call(
        _mix_sample_body,
        grid=(DEC_BATCH,),
        in_specs=[per_b(DEC_SEQ, A_WIDTH), per_b(DEC_SEQ, A_WIDTH), per_b(N_KV_HEADS, rows, HEAD_DIM),
                  per_b(DEC_SEQ, 2 * KV_WIDTH), per_b(WINDOW, KV_WIDTH), per_b(WINDOW, KV_WIDTH),
                  const(A_GROUPS, CHUNK, CHUNK), const(CHUNK, A_GROUPS), const(N_KV_HEADS, rows, 1)],
        out_specs=[per_b(DEC_SEQ, A_WIDTH), per_b(N_KV_HEADS, rows, HEAD_DIM)],
        out_shape=[jax.ShapeDtypeStruct((DEC_BATCH, DEC_SEQ, A_WIDTH), BF16),
                   jax.ShapeDtypeStruct((DEC_BATCH, N_KV_HEADS, rows, HEAD_DIM), BF16)],
        compiler_params=_params("arbitrary"),
        name="mix_sample",
    )(u_s, v_s, q_s, kv_s, cache_k, cache_v, w_s, b_s_t, sink_rows)


def _outproj_body(mp_ref, ms_ref, w_ref, b_ref, y_ref):
    i = pl.program_id(0)

    @pl.when(i < N_PROMPT_UNITS)
    def _():
        y_ref[...] = _dot(mp_ref[...], w_ref[...]) + b_ref[...]

    @pl.when(i >= N_PROMPT_UNITS)
    def _():
        y_ref[...] = _dot(ms_ref[...], w_ref[...]) + b_ref[...]


def _outproj(mix_p, mix_s, w_bf, b):
    return pl.pallas_call(
        _outproj_body,
        grid=(N_UNITS,),
        in_specs=[pl.BlockSpec((ROWS_PER_UNIT, D_MODEL), lambda i: (jnp.minimum(i, N_PROMPT_UNITS - 1), 0)),
                  pl.BlockSpec((M_SAMPLE, D_MODEL), lambda i: (0, 0)),
                  pl.BlockSpec((D_MODEL, D_MODEL), lambda i: (0, 0)),
                  pl.BlockSpec((1, D_MODEL), lambda i: (0, 0))],
        out_specs=_unit_spec(D_MODEL),
        out_shape=jax.ShapeDtypeStruct((M_ROWS, D_MODEL), F32),
        compiler_params=_params("arbitrary"),
        name="outproj",
    )(mix_p, mix_s, w_bf, b.reshape(1, D_MODEL))


def _route_top2(h, wrt_ref, ri_ref, rw_ref):
    logits = [jnp.sum(h * wrt_ref[e:e + 1, :], axis=-1, keepdims=True) for e in range(N_EXPERTS)]
    m1 = logits[0]
    i1 = jnp.zeros_like(m1, dtype=jnp.int32)
    for e in range(1, N_EXPERTS):
        gt = logits[e] > m1
        m1 = jnp.where(gt, logits[e], m1)
        i1 = jnp.where(gt, e, i1)
    m2 = jnp.full_like(m1, -jnp.inf)
    i2 = jnp.zeros_like(i1)
    for e in range(N_EXPERTS):
        ok = (i1 != e) & (logits[e] > m2)
        m2 = jnp.where(ok, logits[e], m2)
        i2 = jnp.where(ok, e, i2)
    w1 = 1.0 / (1.0 + jnp.exp(m2 - m1))
    ri_ref[:, 0:1] = i1
    ri_ref[:, 1:2] = i2
    rw_ref[:, 0:1] = w1
    rw_ref[:, 1:2] = 1.0 - w1


def _finish_unit(x, y, gate, shift, scale, lg_ref, lb_ref, xo_ref, ho_ref, route_refs):
    xn = _layer_norm(ALPHA * x + (1 + gate) * y, lg_ref[...], lb_ref[...])
    xo_ref[...] = xn
    h = xn * (1 + scale) + shift
    ho_ref[...] = h.astype(BF16)
    if route_refs is not None:
        wrt_ref, hf_ref, ri_ref, rw_ref = route_refs
        hf_ref[...] = h
        _route_top2(h, wrt_ref, ri_ref, rw_ref)


def _finish_dispatch(i, x, y, pg, psh, psc, mpg_ref, mpn_ref, gs_ref, shs_ref, scs_ref,
                     lg_ref, lb_ref, xo_ref, ho_ref, route_refs):
    @pl.when(i < N_PROMPT_UNITS)
    def _():
        _finish_unit(x, y, mpg_ref[pg:pg + 1, :], mpn_ref[psh:psh + 1, :], mpn_ref[psc:psc + 1, :],
                     lg_ref, lb_ref, xo_ref, ho_ref, route_refs)

    @pl.when(i >= N_PROMPT_UNITS)
    def _():
        _finish_unit(x, y, gs_ref[...], shs_ref[...], scs_ref[...], lg_ref, lb_ref, xo_ref, ho_ref, route_refs)


def _epilogue_body(*refs, pg, psh, psc, route):
    x_ref, y_ref, mpg_ref, mpn_ref, gs_ref, shs_ref, scs_ref, lg_ref, lb_ref = refs[:9]
    if route:
        wrt_ref, xo_ref, ho_ref, hf_ref, ri_ref, rw_ref = refs[9:]
        route_refs = (wrt_ref, hf_ref, ri_ref, rw_ref)
    else:
        xo_ref, ho_ref = refs[9:]
        route_refs = None
    _finish_dispatch(pl.program_id(0), x_ref[...], y_ref[...], pg, psh, psc, mpg_ref, mpn_ref,
                     gs_ref, shs_ref, scs_ref, lg_ref, lb_ref, xo_ref, ho_ref, route_refs)


def _epilogue_specs(layer, next_layer, pg, psh, psc):
    row = pl.BlockSpec((1, D_MODEL), lambda i, *_: (0, 0))
    modp = lambda l: pl.BlockSpec((None, None, 6, D_MODEL), lambda i, *_: (l, _prompt_batch_of_unit(i), 0, 0))
    mods = lambda l, p: pl.BlockSpec((None, None, ROWS_PER_UNIT, D_MODEL), lambda i, *_: (l, p, 0, 0))
    return [modp(layer), modp(next_layer), mods(layer, pg), mods(next_layer, psh), mods(next_layer, psc), row, row]


def _epilogue(x, y, modp, mods, ln_g, ln_b, layer, next_layer, pg, psh, psc, w_router_t=None):
    route = w_router_t is not None
    unit = lambda w: pl.BlockSpec((ROWS_PER_UNIT, w), lambda i: (i, 0))
    in_specs = [unit(D_MODEL), unit(D_MODEL)] + _epilogue_specs(layer, next_layer, pg, psh, psc)
    args = [x, y, modp, modp, mods, mods, mods, ln_g.reshape(1, D_MODEL), ln_b.reshape(1, D_MODEL)]
    out_specs = [unit(D_MODEL), unit(D_MODEL)]
    out_shape = [jax.ShapeDtypeStruct((M_ROWS, D_MODEL), F32), jax.ShapeDtypeStruct((M_ROWS, D_MODEL), BF16)]
    if route:
        in_specs.append(pl.BlockSpec((N_EXPERTS, D_MODEL), lambda i: (0, 0)))
        args.append(w_router_t)
        out_specs += [unit(D_MODEL), unit(TOP_K), unit(TOP_K)]
        out_shape += [jax.ShapeDtypeStruct((M_ROWS, D_MODEL), F32),
                      jax.ShapeDtypeStruct((M_ROWS, TOP_K), jnp.int32),
                      jax.ShapeDtypeStruct((M_ROWS, TOP_K), F32)]
    return pl.pallas_call(
        functools.partial(_epilogue_body, pg=pg, psh=psh, psc=psc, route=route),
        grid=(N_UNITS,),
        in_specs=in_specs,
        out_specs=out_specs,
        out_shape=out_shape,
        compiler_params=_params("arbitrary"),
        name="epilogue_route" if route else "epilogue",
    )(*args)


def _swiglu_rows(x, wg, wu, wd):
    g = _dot(x, wg)
    u = _dot(x, wu)
    return _dot((_silu(g) * u).astype(BF16), wd)


def _ffn_body(h_ref, wg_ref, wu_ref, wd_ref, o_ref):
    i = pl.program_id(0)
    j = pl.program_id(1)
    wg = wg_ref[...]
    wu = wu_ref[...]
    wd = wd_ref[...]
    for r in range(FFN_ROW_TILE // ROWS_PER_UNIT):
        rows = slice(r * ROWS_PER_UNIT, (r + 1) * ROWS_PER_UNIT)

        @pl.when(i * (FFN_ROW_TILE // ROWS_PER_UNIT) + r < N_UNITS)
        def _(rows=rows):
            c = _swiglu_rows(h_ref[rows, :], wg, wu, wd)

            @pl.when(j == 0)
            def _():
                o_ref[rows, :] = c

            @pl.when(j > 0)
            def _():
                o_ref[rows, :] += c


def _ffn(h, wg, wu, wd):
    nj = D_FF_PAD // FFN_TILE
    return pl.pallas_call(
        _ffn_body,
        grid=(pl.cdiv(M_ROWS, FFN_ROW_TILE), nj),
        in_specs=[pl.BlockSpec((FFN_ROW_TILE, D_MODEL), lambda i, j: (i, 0)),
                  pl.BlockSpec((D_MODEL, FFN_TILE), lambda i, j: (0, j)),
                  pl.BlockSpec((D_MODEL, FFN_TILE), lambda i, j: (0, j)),
                  pl.BlockSpec((FFN_TILE, D_MODEL), lambda i, j: (j, 0))],
        out_specs=pl.BlockSpec((FFN_ROW_TILE, D_MODEL), lambda i, j: (i, 0)),
        out_shape=jax.ShapeDtypeStruct((M_ROWS, D_MODEL), F32),
        compiler_params=_params("arbitrary", "arbitrary"),
        name="ffn_dense",
    )(h, wg, wu, wd)


def _row_copy(src_hbm, row, dst, dst_row, sem):
    return pltpu.make_async_copy(src_hbm.at[pl.ds(row, 1), :], dst.at[pl.ds(dst_row, 1), :], sem)


def _moe_gather_body(src_ref, uv_ref, h_hbm, o_ref, buf, sem):
    i = pl.program_id(0)
    base = i * ROWS_PER_UNIT

    @pl.when(uv_ref[i] > 0)
    def _():
        def start(r, c):
            _row_copy(h_hbm, src_ref[base + r], buf, r, sem).start()
            return c

        def wait(r, c):
            _row_copy(h_hbm, 0, buf, r, sem).wait()
            return c

        lax.fori_loop(0, ROWS_PER_UNIT, start, 0)
        lax.fori_loop(0, ROWS_PER_UNIT, wait, 0)
        o_ref[...] = buf[...].astype(BF16)

    @pl.when(uv_ref[i] <= 0)
    def _():
        o_ref[...] = jnp.zeros_like(o_ref)


def _moe_gather(src_tok, unit_valid, h_f32):
    grid_spec = pltpu.PrefetchScalarGridSpec(
        num_scalar_prefetch=2,
        grid=(MOE_UNITS,),
        in_specs=[pl.BlockSpec(memory_space=pl.ANY)],
        out_specs=pl.BlockSpec((ROWS_PER_UNIT, D_MODEL), lambda i, s, v: (i, 0)),
        scratch_shapes=[pltpu.VMEM((ROWS_PER_UNIT, D_MODEL), F32), pltpu.SemaphoreType.DMA(())],
    )
    return pl.pallas_call(
        _moe_gather_body,
        grid_spec=grid_spec,
        out_shape=jax.ShapeDtypeStruct((MOE_ROWS, D_MODEL), BF16),
        compiler_params=_params("arbitrary"),
        name="moe_gather",
    )(src_tok, unit_valid, h_f32)


def _moe_gmm_body(te_ref, tv_ref, tx_ref, x_ref, wg_ref, wu_ref, wd_ref, ws_ref, o_ref):
    t = pl.program_id(0)
    j = pl.program_id(1)
    nj = pl.num_programs(1)
    valid = tv_ref[t]

    for r in range(MOE_ROW_TILE // ROWS_PER_UNIT):
        @pl.when((j == 0) & (r * ROWS_PER_UNIT >= valid))
        def _(r=r):
            o_ref[r * ROWS_PER_UNIT:(r + 1) * ROWS_PER_UNIT, :] = jnp.zeros((ROWS_PER_UNIT, D_MODEL), F32)

    @pl.when(valid > 0)
    def _():
        wg = wg_ref[...].astype(BF16)
        wu = wu_ref[...].astype(BF16)
        wd = wd_ref[...].astype(BF16)
        for r in range(MOE_ROW_TILE // ROWS_PER_UNIT):
            rows = slice(r * ROWS_PER_UNIT, (r + 1) * ROWS_PER_UNIT)

            @pl.when(r * ROWS_PER_UNIT < valid)
            def _(rows=rows):
                c = _swiglu_rows(x_ref[rows, :], wg, wu, wd)

                @pl.when(j == 0)
                def _():
                    o_ref[rows, :] = c

                @pl.when((j > 0) & (j < nj - 1))
                def _():
                    o_ref[rows, :] += c

                @pl.when(j == nj - 1)
                def _():
                    o_ref[rows, :] = (o_ref[rows, :] + c) * ws_ref[rows, :]


def _moe_gmm(tile_expert, tile_valid, tile_x, xs, w_gu, w_down, ws, lm):
    nj = D_FF_EXPERT // MOE_FF_TILE
    jj = lambda t, j, tv: jnp.where(tv[t] > 0, j, nj - 1)
    grid_spec = pltpu.PrefetchScalarGridSpec(
        num_scalar_prefetch=3,
        grid=(MOE_TILES, nj),
        in_specs=[pl.BlockSpec((MOE_ROW_TILE, D_MODEL), lambda t, j, te, tv, tx: (tx[t], 0)),
                  pl.BlockSpec((None, None, D_MODEL, MOE_FF_TILE),
                               lambda t, j, te, tv, tx: (lm, te[t], 0, jj(t, j, tv))),
                  pl.BlockSpec((None, None, D_MODEL, MOE_FF_TILE),
                               lambda t, j, te, tv, tx: (lm, te[t], 0, nj + jj(t, j, tv))),
                  pl.BlockSpec((None, None, MOE_FF_TILE, D_MODEL),
                               lambda t, j, te, tv, tx: (lm, te[t], jj(t, j, tv), 0)),
                  pl.BlockSpec((MOE_ROW_TILE, 1), lambda t, j, te, tv, tx: (tx[t], 0))],
        out_specs=pl.BlockSpec((MOE_ROW_TILE, D_MODEL), lambda t, j, te, tv, tx: (t, 0)),
    )
    return pl.pallas_call(
        _moe_gmm_body,
        grid_spec=grid_spec,
        out_shape=jax.ShapeDtypeStruct((MOE_ROWS, D_MODEL), F32),
        compiler_params=_params("arbitrary", "arbitrary"),
        name="moe_gmm",
    )(tile_expert, tile_valid, tile_x, xs, w_gu, w_gu, w_down, ws)


def _moe_combine_body(dest_ref, ys_hbm, x_ref, mpg_ref, mpn_ref, gs_ref, shs_ref, scs_ref, lg_ref, lb_ref,
                      xo_ref, ho_ref, buf, sem, *, pg, psh, psc):
    i = pl.program_id(0)
    base = i * ROWS_PER_UNIT * TOP_K

    def start(r, c):
        for k in range(TOP_K):
            _row_copy(ys_hbm, dest_ref[base + TOP_K * r + k], buf.at[k], r, sem).start()
        return c

    def wait(r, c):
        for k in range(TOP_K):
            _row_copy(ys_hbm, 0, buf.at[k], r, sem).wait()
        return c

    lax.fori_loop(0, ROWS_PER_UNIT, start, 0)
    lax.fori_loop(0, ROWS_PER_UNIT, wait, 0)
    y = buf[0] + buf[1]
    _finish_dispatch(i, x_ref[...], y, pg, psh, psc, mpg_ref, mpn_ref, gs_ref, shs_ref, scs_ref,
                     lg_ref, lb_ref, xo_ref, ho_ref, None)


def _moe_combine(dest, ys, x, modp, mods, ln_g, ln_b, layer, next_layer, pg, psh, psc):
    unit = lambda w: pl.BlockSpec((ROWS_PER_UNIT, w), lambda i, d: (i, 0))
    grid_spec = pltpu.PrefetchScalarGridSpec(
        num_scalar_prefetch=1,
        grid=(N_UNITS,),
        in_specs=[pl.BlockSpec(memory_space=pl.ANY), unit(D_MODEL)]
        + _epilogue_specs(layer, next_layer, pg, psh, psc),
        out_specs=[unit(D_MODEL), unit(D_MODEL)],
        scratch_shapes=[pltpu.VMEM((TOP_K, ROWS_PER_UNIT, D_MODEL), F32), pltpu.SemaphoreType.DMA(())],
    )
    return pl.pallas_call(
        functools.partial(_moe_combine_body, pg=pg, psh=psh, psc=psc),
        grid_spec=grid_spec,
        out_shape=[jax.ShapeDtypeStruct((M_ROWS, D_MODEL), F32), jax.ShapeDtypeStruct((M_ROWS, D_MODEL), BF16)],
        compiler_params=_params("arbitrary"),
        name="moe_combine",
    )(dest, ys, x, modp, modp, mods, mods, mods, ln_g.reshape(1, D_MODEL), ln_b.reshape(1, D_MODEL))


def _moe_plan(route_i, route_w):
    e_flat = route_i.reshape(-1)
    onehot = (e_flat[:, None] == jnp.arange(N_EXPERTS, dtype=jnp.int32)[None, :]).astype(jnp.int32)
    csum = jnp.cumsum(onehot, axis=0)
    rank = jnp.sum(csum * onehot, axis=1) - 1
    counts = csum[-1]
    tiles_e = (counts + MOE_ROW_TILE - 1) // MOE_ROW_TILE
    tile_end = jnp.cumsum(tiles_e)
    tile_start = tile_end - tiles_e
    dest = (tile_start[e_flat] * MOE_ROW_TILE + rank).astype(jnp.int32)
    pair_tok = jnp.arange(N_PAIRS, dtype=jnp.int32) // TOP_K
    src_tok = jnp.zeros((MOE_ROWS,), jnp.int32).at[dest].set(pair_tok)
    ws = jnp.zeros((MOE_ROWS,), F32).at[dest].set(route_w.reshape(-1)).reshape(MOE_ROWS, 1)
    n_tiles = tile_end[-1]
    t_ids = jnp.arange(MOE_TILES, dtype=jnp.int32)
    t_eff = jnp.minimum(t_ids, n_tiles - 1)
    tile_expert = jnp.sum((t_eff[:, None] >= tile_end[None, :]).astype(jnp.int32), axis=1)
    rows_left = counts[tile_expert] - (t_eff - tile_start[tile_expert]) * MOE_ROW_TILE
    tile_valid = jnp.where(t_ids < n_tiles, jnp.minimum(rows_left, MOE_ROW_TILE), 0).astype(jnp.int32)
    u_ids = jnp.arange(MOE_UNITS, dtype=jnp.int32)
    upt = MOE_ROW_TILE // ROWS_PER_UNIT
    unit_valid = jnp.clip(tile_valid[u_ids // upt] - (u_ids % upt) * ROWS_PER_UNIT, 0, ROWS_PER_UNIT)
    return dest, src_tok, ws, tile_expert.astype(jnp.int32), tile_valid, t_eff.astype(jnp.int32), unit_valid.astype(jnp.int32)


def kernel(x_prompt, x_sample, cache_win_k, cache_win_v, c_prompt, c_sample, w_ada, b_ada, w_in, b_in, v_norm_g, v_norm_b, w_spatial, b_spatial, attn_sinks, w_out, b_out, ln1_g, ln1_b, ln2_g, ln2_b, w_ffn_gu, w_ffn_down, w_router, w_exp_gu, w_exp_down):
    c_rows = jnp.concatenate([c_prompt, c_sample, jnp.zeros((ADA_ROWS - BATCH - DEC_BATCH, D_MODEL), F32)], axis=0)
    mod = _adaln(c_rows, w_ada, b_ada)
    modp = mod[:, :BATCH].reshape(DEPTH, BATCH, 6, D_MODEL)
    mods = jnp.repeat(mod[:, BATCH:BATCH + DEC_BATCH].reshape(DEPTH, DEC_BATCH, 6, D_MODEL), DEC_SEQ, axis=1)
    mods = mods.transpose(0, 2, 1, 3)

    x = jnp.concatenate([x_prompt.reshape(M_PROMPT, D_MODEL), x_sample.reshape(M_SAMPLE, D_MODEL)], axis=0)
    h = _modulate0(x, modp, mods)

    rows = GQA * DEC_SEQ
    kp_list, vp_list, ks_list, vs_list, cv_list = [], [], [], [], []
    for l in range(DEPTH):
        nl = min(l + 1, DEPTH - 1)
        u, v, q, kv = _proj(h, w_in[l].astype(BF16), b_in[l], v_norm_g[l], v_norm_b[l])
        b_s_t = b_spatial[l].T
        mix_p = _mix_prompt(attn_sinks[l].reshape(-1), u, v, q, kv, w_spatial[l], b_s_t)

        u_s = u[M_PROMPT:].reshape(DEC_BATCH, DEC_SEQ, A_WIDTH)
        v_s = v[M_PROMPT:].reshape(DEC_BATCH, DEC_SEQ, A_WIDTH)
        kv_s = kv[M_PROMPT:].reshape(DEC_BATCH, DEC_SEQ, 2 * KV_WIDTH)
        q_s = q[M_PROMPT:].reshape(DEC_BATCH, DEC_SEQ, N_KV_HEADS, GQA, HEAD_DIM)
        q_s = q_s.transpose(0, 2, 3, 1, 4).reshape(DEC_BATCH, N_KV_HEADS, rows, HEAD_DIM)
        ck = cache_win_k[l].reshape(DEC_BATCH, WINDOW, KV_WIDTH)
        cv = cache_win_v[l].reshape(DEC_BATCH, WINDOW, KV_WIDTH)
        sink_rows = jnp.repeat(attn_sinks[l], DEC_SEQ, axis=1).reshape(N_KV_HEADS, rows, 1)
        a_s, o_s = _mix_sample(u_s, v_s, q_s, kv_s, ck, cv, w_spatial[l], b_s_t, sink_rows)
        o_s = o_s.reshape(DEC_BATCH, N_KV_HEADS, GQA, DEC_SEQ, HEAD_DIM).transpose(0, 3, 1, 2, 4)
        mix_s = jnp.concatenate([a_s.reshape(M_SAMPLE, A_WIDTH), o_s.reshape(M_SAMPLE, B_WIDTH)], axis=1)

        y = _outproj(mix_p, mix_s, w_out[l].astype(BF16), b_out[l])
        if l % 2 == 0:
            x, h = _epilogue(x, y, modp, mods, ln1_g[l], ln1_b[l], l, l, P_GATE1, P_SHIFT2, P_SCALE2)
            ld = l // 2
            pad = ((0, 0), (0, D_FF_PAD - D_FF))
            wg = jnp.pad(w_ffn_gu[ld][:, :D_FF].astype(BF16), pad)
            wu = jnp.pad(w_ffn_gu[ld][:, D_FF:].astype(BF16), pad)
            wd = jnp.pad(w_ffn_down[ld].astype(BF16), ((0, D_FF_PAD - D_FF), (0, 0)))
            f = _ffn(h, wg, wu, wd)
            x, h = _epilogue(x, f, modp, mods, ln2_g[l], ln2_b[l], l, nl, P_GATE2, P_SHIFT1, P_SCALE1)
        else:
            lm = l // 2
            x, h, h_f32, route_i, route_w = _epilogue(x, y, modp, mods, ln1_g[l], ln1_b[l], l, l,
                                                      P_GATE1, P_SHIFT2, P_SCALE2, w_router_t=w_router[lm].T)
            dest, src_tok, ws, tile_expert, tile_valid, tile_x, unit_valid = _moe_plan(route_i, route_w)
            xs = _moe_gather(src_tok, unit_valid, h_f32)
            ys = _moe_gmm(tile_expert, tile_valid, tile_x, xs, w_exp_gu, w_exp_down, ws, lm)
            x, h = _moe_combine(dest, ys, x, modp, mods, ln2_g[l], ln2_b[l], l, nl, P_GATE2, P_SHIFT1, P_SCALE1)

        kv_p = kv[:M_PROMPT].reshape(BATCH, SEQ, 2 * KV_WIDTH)[:, SEQ - WINDOW:]
        kp_list.append(kv_p[..., :KV_WIDTH].reshape(BATCH, WINDOW, N_KV_HEADS, HEAD_DIM))
        vp_list.append(kv_p[..., KV_WIDTH:].reshape(BATCH, WINDOW, N_KV_HEADS, HEAD_DIM))
        k_new = kv_s[..., :KV_WIDTH].reshape(DEC_BATCH, DEC_SEQ, N_KV_HEADS, HEAD_DIM)
        v_new = kv_s[..., KV_WIDTH:].reshape(DEC_BATCH, DEC_SEQ, N_KV_HEADS, HEAD_DIM)
        ks_list.append(jnp.concatenate([cache_win_k[l][:, DEC_SEQ:], k_new], axis=1))
        vs_list.append(jnp.concatenate([cache_win_v[l][:, DEC_SEQ:], v_new], axis=1))
        cv_list.append(v_s.astype(F32))

    y_prompt = x[:M_PROMPT].reshape(BATCH, SEQ, D_MODEL)
    y_sample = x[M_PROMPT:].reshape(DEC_BATCH, DEC_SEQ, D_MODEL)
    return (y_prompt, y_sample, jnp.stack(kp_list), jnp.stack(vp_list), jnp.stack(ks_list), jnp.stack(vs_list),
            jnp.stack(cv_list))
```

```python
import functools

import jax
import jax.numpy as jnp
from jax import lax
from jax.experimental import pallas as pl
from jax.experimental.pallas import tpu as pltpu

F32 = jnp.float32
BF16 = jnp.bfloat16

D_MODEL = 2048
BATCH = 4
SEQ = 2048
DEPTH = 4
DEC_BATCH = 32
DEC_SEQ = 8
CHUNK = 128
A_WIDTH = D_MODEL // 2
A_GROUPS = 4
A_GROUP_DIM = A_WIDTH // A_GROUPS
HEAD_DIM = 64
B_WIDTH = D_MODEL - A_WIDTH
N_HEADS = B_WIDTH // HEAD_DIM
N_KV_HEADS = 2
GQA = N_HEADS // N_KV_HEADS
KV_WIDTH = N_KV_HEADS * HEAD_DIM
WINDOW = 128
IN_COLS = 2 * A_WIDTH + B_WIDTH + 2 * KV_WIDTH
D_FF = 5504
N_EXPERTS = 8
TOP_K = 2
D_FF_EXPERT = 7168
ALPHA = (2.0 * DEPTH) ** 0.25
LN_EPS = 1e-5
NEG_INF = -1e30

M_PROMPT = BATCH * SEQ
M_SAMPLE = DEC_BATCH * DEC_SEQ
M_ROWS = M_PROMPT + M_SAMPLE
ROWS_PER_UNIT = 256
N_PROMPT_UNITS = M_PROMPT // ROWS_PER_UNIT
N_UNITS = M_ROWS // ROWS_PER_UNIT
UNITS_PER_SEQ = SEQ // ROWS_PER_UNIT
ADA_ROWS = 40
P_SHIFT1, P_SCALE1, P_GATE1, P_SHIFT2, P_SCALE2, P_GATE2 = range(6)

D_FF_PAD = 5632
FFN_TILE = 512
FFN_ROW_TILE = 4 * ROWS_PER_UNIT
MOE_FF_TILE = 256
N_PAIRS = M_ROWS * TOP_K
MOE_UNITS = N_PAIRS // ROWS_PER_UNIT + N_EXPERTS
MOE_ROWS = MOE_UNITS * ROWS_PER_UNIT
MOE_TILE_UNITS = 9
MOE_TILES = MOE_UNITS // MOE_TILE_UNITS + N_EXPERTS

VMEM_LIMIT = 56 * 1024 * 1024


def _params(*sem):
    return pltpu.CompilerParams(dimension_semantics=sem, vmem_limit_bytes=VMEM_LIMIT)


def _dot(a, b):
    return jnp.dot(a, b, preferred_element_type=F32)


def _layer_norm(z, g, b):
    mu = jnp.mean(z, axis=-1, keepdims=True)
    zc = z - mu
    var = jnp.mean(zc * zc, axis=-1, keepdims=True)
    return zc * lax.rsqrt(var + LN_EPS) * g + b


def _gelu_tanh(x):
    return x * (0.5 * (1.0 + jnp.tanh(0.7978845608028654 * (x + 0.044715 * (x * x * x)))))


def _silu(x):
    return x * jax.nn.sigmoid(x)


def _prompt_batch_of_unit(i):
    return jnp.minimum(i // UNITS_PER_SEQ, BATCH - 1)


def _adaln_body(c_ref, w_ref, b_ref, o_ref):
    s = _silu(c_ref[...]).astype(BF16)
    o_ref[...] = _dot(s, w_ref[...].astype(BF16)) + b_ref[...]


def _adaln(c_rows, w_ada, b_ada):
    tn = 1024
    return pl.pallas_call(
        _adaln_body,
        grid=(DEPTH, 6 * D_MODEL // tn),
        in_specs=[pl.BlockSpec((ADA_ROWS, D_MODEL), lambda l, n: (0, 0)),
                  pl.BlockSpec((None, D_MODEL, tn), lambda l, n: (l, 0, n)),
                  pl.BlockSpec((None, 1, tn), lambda l, n: (l, 0, n))],
        out_specs=pl.BlockSpec((None, ADA_ROWS, tn), lambda l, n: (l, 0, n)),
        out_shape=jax.ShapeDtypeStruct((DEPTH, ADA_ROWS, 6 * D_MODEL), F32),
        compiler_params=_params("arbitrary", "arbitrary"),
        name="adaln",
    )(c_rows, w_ada, b_ada.reshape(DEPTH, 1, 6 * D_MODEL))


def _mod0_body(x_ref, mp_ref, ss_ref, cs_ref, h_ref):
    i = pl.program_id(0)
    x = x_ref[...]

    @pl.when(i < N_PROMPT_UNITS)
    def _():
        h_ref[...] = (x * (1 + mp_ref[P_SCALE1:P_SCALE1 + 1, :]) + mp_ref[P_SHIFT1:P_SHIFT1 + 1, :]).astype(BF16)

    @pl.when(i >= N_PROMPT_UNITS)
    def _():
        h_ref[...] = (x * (1 + cs_ref[...]) + ss_ref[...]).astype(BF16)


def _unit_spec(width):
    return pl.BlockSpec((ROWS_PER_UNIT, width), lambda i: (i, 0))


def _modp_spec(layer):
    return pl.BlockSpec((None, None, 6, D_MODEL), lambda i: (layer, _prompt_batch_of_unit(i), 0, 0))


def _mods_spec(layer, p):
    return pl.BlockSpec((None, None, ROWS_PER_UNIT, D_MODEL), lambda i: (layer, p, 0, 0))


def _modulate0(x, modp, mods):
    return pl.pallas_call(
        _mod0_body,
        grid=(N_UNITS,),
        in_specs=[_unit_spec(D_MODEL), _modp_spec(0), _mods_spec(0, P_SHIFT1), _mods_spec(0, P_SCALE1)],
        out_specs=_unit_spec(D_MODEL),
        out_shape=jax.ShapeDtypeStruct((M_ROWS, D_MODEL), BF16),
        compiler_params=_params("arbitrary"),
        name="modulate0",
    )(x, modp, mods, mods)


def _proj_body(h_ref, w_ref, b_ref, gv_ref, bv_ref, u_ref, v_ref, q_ref, kv_ref):
    h = h_ref[...]
    gd = A_GROUP_DIM
    for c in range(A_GROUPS):
        sl = slice(c * gd, (c + 1) * gd)
        z = _dot(h, w_ref[:, sl]) + b_ref[:, sl]
        u_ref[:, sl] = _gelu_tanh(z).astype(BF16)
    for g in range(A_GROUPS):
        sl = slice(A_WIDTH + g * gd, A_WIDTH + (g + 1) * gd)
        z = _gelu_tanh(_dot(h, w_ref[:, sl]) + b_ref[:, sl])
        vn = _layer_norm(z, gv_ref[g:g + 1, :], bv_ref[g:g + 1, :])
        v_ref[:, g * gd:(g + 1) * gd] = vn.astype(BF16)
    o1 = 2 * A_WIDTH
    o2 = o1 + B_WIDTH
    z = _dot(h, w_ref[:, o1:o2]) + b_ref[:, o1:o2]
    q_ref[...] = (z * (HEAD_DIM ** -0.5)).astype(BF16)
    kv_ref[...] = _dot(h, w_ref[:, o2:]) + b_ref[:, o2:]


def _proj(h, w_bf, b, gv, bv):
    const = lambda shape: pl.BlockSpec(shape, lambda i: (0,) * len(shape))
    return pl.pallas_call(
        _proj_body,
        grid=(N_UNITS,),
        in_specs=[_unit_spec(D_MODEL), const((D_MODEL, IN_COLS)), const((1, IN_COLS)),
                  const((A_GROUPS, A_GROUP_DIM)), const((A_GROUPS, A_GROUP_DIM))],
        out_specs=[_unit_spec(A_WIDTH), _unit_spec(A_WIDTH), _unit_spec(B_WIDTH), _unit_spec(2 * KV_WIDTH)],
        out_shape=[jax.ShapeDtypeStruct((M_ROWS, A_WIDTH), BF16), jax.ShapeDtypeStruct((M_ROWS, A_WIDTH), BF16),
                   jax.ShapeDtypeStruct((M_ROWS, B_WIDTH), BF16), jax.ShapeDtypeStruct((M_ROWS, 2 * KV_WIDTH), F32)],
        compiler_params=_params("arbitrary"),
        name="proj",
    )(h, w_bf, b.reshape(1, IN_COLS), gv, bv)


def _mix_prompt_body(sink_ref, u_ref, v_ref, q_ref, kvc_ref, kvp_ref, ws_ref, bs_ref, o_ref):
    blk = pl.program_id(1)
    gd = A_GROUP_DIM
    tr = lax.broadcasted_iota(jnp.int32, (CHUNK, CHUNK), 0)
    tc = lax.broadcasted_iota(jnp.int32, (CHUNK, CHUNK), 1)
    for g in range(A_GROUPS):
        sl = slice(g * gd, (g + 1) * gd)
        w = jnp.where(tr >= tc, ws_ref[g], 0.0).astype(BF16)
        s = _dot(w, v_ref[:, sl]) + bs_ref[:, g:g + 1]
        o_ref[:, sl] = (u_ref[:, sl].astype(F32) * s).astype(BF16)

    kvc = kvc_ref[...]
    kvp = kvp_ref[...]
    row = lax.broadcasted_iota(jnp.int32, (WINDOW, 2 * WINDOW), 0)
    col = lax.broadcasted_iota(jnp.int32, (WINDOW, 2 * WINDOW), 1)
    dlt = row + WINDOW - col
    mask = (dlt >= 0) & (dlt <= WINDOW) & ((col >= WINDOW) | (blk > 0))
    hd = HEAD_DIM
    for j in range(N_KV_HEADS):
        ks = slice(j * hd, (j + 1) * hd)
        vs = slice(KV_WIDTH + j * hd, KV_WIDTH + (j + 1) * hd)
        kk = jnp.concatenate([kvp[:, ks], kvc[:, ks]], axis=0).astype(BF16)
        vv = jnp.concatenate([kvp[:, vs], kvc[:, vs]], axis=0).astype(BF16)
        for hh in range(GQA):
            h = j * GQA + hh
            qh = q_ref[:, h * hd:(h + 1) * hd]
            s = lax.dot_general(qh, kk, (((1,), (1,)), ((), ())), preferred_element_type=F32)
            s = jnp.where(mask, s, NEG_INF)
            sk = sink_ref[h]
            m = jnp.maximum(jnp.max(s, axis=-1, keepdims=True), sk)
            p = jnp.exp(s - m)
            den = jnp.sum(p, axis=-1, keepdims=True) + jnp.exp(sk - m)
            o = _dot(p.astype(BF16), vv) / den
            o_ref[:, A_WIDTH + h * hd:A_WIDTH + (h + 1) * hd] = o.astype(BF16)


def _mix_prompt(sinks, u, v, q, kv, w_s, b_s_t):
    nb = SEQ // WINDOW
    row_blk = lambda w: pl.BlockSpec((WINDOW, w), lambda b, i, s: (b * nb + i, 0))
    grid_spec = pltpu.PrefetchScalarGridSpec(
        num_scalar_prefetch=1,
        grid=(BATCH, nb),
        in_specs=[row_blk(A_WIDTH), row_blk(A_WIDTH), row_blk(B_WIDTH), row_blk(2 * KV_WIDTH),
                  pl.BlockSpec((WINDOW, 2 * KV_WIDTH), lambda b, i, s: (b * nb + jnp.maximum(i - 1, 0), 0)),
                  pl.BlockSpec((A_GROUPS, CHUNK, CHUNK), lambda b, i, s: (0, 0, 0)),
                  pl.BlockSpec((CHUNK, A_GROUPS), lambda b, i, s: (0, 0))],
        out_specs=pl.BlockSpec((WINDOW, D_MODEL), lambda b, i, s: (b * nb + i, 0)),
    )
    return pl.pallas_call(
        _mix_prompt_body,
        grid_spec=grid_spec,
        out_shape=jax.ShapeDtypeStruct((M_PROMPT, D_MODEL), BF16),
        compiler_params=_params("arbitrary", "arbitrary"),
        name="mix_prompt",
    )(sinks, u, v, q, kv, kv, w_s, b_s_t)


SAMPLE_KEYS = 2 * WINDOW


def _mix_sample_body(u_ref, v_ref, q_ref, kn_ref, ck_ref, cv_ref, ws_ref, bs_ref, sk_ref, a_ref, o_ref):
    gd = A_GROUP_DIM
    ds = DEC_SEQ
    v = v_ref[...].astype(F32)
    u = u_ref[...].astype(F32)
    tr = lax.broadcasted_iota(jnp.int32, (ds, ds), 0)
    tc = lax.broadcasted_iota(jnp.int32, (ds, ds), 1)
    for g in range(A_GROUPS):
        sl = slice(g * gd, (g + 1) * gd)
        w = jnp.where(tr >= tc, ws_ref[g][:ds, :ds], 0.0)
        s = jnp.zeros((ds, gd), F32) + bs_ref[:ds, g:g + 1]
        for t in range(ds):
            s = s + w[:, t:t + 1] * v[t:t + 1, sl]
        a_ref[:, sl] = (u[:, sl] * s).astype(BF16)

    kn = kn_ref[...]
    ck = ck_ref[...]
    cv = cv_ref[...]
    hd = HEAD_DIM
    rows = GQA * ds
    qt = lax.broadcasted_iota(jnp.int32, (rows, SAMPLE_KEYS), 0) & (ds - 1)
    col = lax.broadcasted_iota(jnp.int32, (rows, SAMPLE_KEYS), 1)
    mask = ((col < WINDOW) & (col >= qt)) | ((col >= WINDOW) & (col - WINDOW <= qt))
    pad = jnp.zeros((SAMPLE_KEYS - WINDOW - ds, hd), F32)
    for j in range(N_KV_HEADS):
        ks = slice(j * hd, (j + 1) * hd)
        vs = slice(KV_WIDTH + j * hd, KV_WIDTH + (j + 1) * hd)
        kk = jnp.concatenate([ck[:, ks], kn[:, ks], pad], axis=0).astype(BF16)
        vv = jnp.concatenate([cv[:, ks], kn[:, vs], pad], axis=0).astype(BF16)
        s = lax.dot_general(q_ref[j], kk, (((1,), (1,)), ((), ())), preferred_element_type=F32)
        s = jnp.where(mask, s, NEG_INF)
        sk = sk_ref[j]
        m = jnp.maximum(jnp.max(s, axis=-1, keepdims=True), sk)
        p = jnp.exp(s - m)
        den = jnp.sum(p, axis=-1, keepdims=True) + jnp.exp(sk - m)
        o_ref[j] = (_dot(p.astype(BF16), vv) / den).astype(BF16)


def _mix_sample(u_s, v_s, q_s, kv_s, cache_k, cache_v, w_s, b_s_t, sink_rows):
    rows = GQA * DEC_SEQ
    per_b = lambda *shape: pl.BlockSpec((None,) + shape, lambda b: (b,) + (0,) * len(shape))
    const = lambda *shape: pl.BlockSpec(shape, lambda b: (0,) * len(shape))
    return pl.pallas_call(
        _mix_sample_body,
        grid=(DEC_BATCH,),
        in_specs=[per_b(DEC_SEQ, A_WIDTH), per_b(DEC_SEQ, A_WIDTH), per_b(N_KV_HEADS, rows, HEAD_DIM),
                  per_b(DEC_SEQ, 2 * KV_WIDTH), per_b(WINDOW, KV_WIDTH), per_b(WINDOW, KV_WIDTH),
                  const(A_GROUPS, CHUNK, CHUNK), const(CHUNK, A_GROUPS), const(N_KV_HEADS, rows, 1)],
        out_specs=[per_b(DEC_SEQ, A_WIDTH), per_b(N_KV_HEADS, rows, HEAD_DIM)],
        out_shape=[jax.ShapeDtypeStruct((DEC_BATCH, DEC_SEQ, A_WIDTH), BF16),
                   jax.ShapeDtypeStruct((DEC_BATCH, N_KV_HEADS, rows, HEAD_DIM), BF16)],
        compiler_params=_params("arbitrary"),
        name="mix_sample",
    )(u_s, v_s, q_s, kv_s, cache_k, cache_v, w_s, b_s_t, sink_rows)


def _outproj_body(mp_ref, ms_ref, w_ref, b_ref, y_ref):
    i = pl.program_id(0)

    @pl.when(i < N_PROMPT_UNITS)
    def _():
        y_ref[...] = _dot(mp_ref[...], w_ref[...]) + b_ref[...]

    @pl.when(i >= N_PROMPT_UNITS)
    def _():
        y_ref[...] = _dot(ms_ref[...], w_ref[...]) + b_ref[...]


def _outproj(mix_p, mix_s, w_bf, b):
    return pl.pallas_call(
        _outproj_body,
        grid=(N_UNITS,),
        in_specs=[pl.BlockSpec((ROWS_PER_UNIT, D_MODEL), lambda i: (jnp.minimum(i, N_PROMPT_UNITS - 1), 0)),
                  pl.BlockSpec((M_SAMPLE, D_MODEL), lambda i: (0, 0)),
                  pl.BlockSpec((D_MODEL, D_MODEL), lambda i: (0, 0)),
                  pl.BlockSpec((1, D_MODEL), lambda i: (0, 0))],
        out_specs=_unit_spec(D_MODEL),
        out_shape=jax.ShapeDtypeStruct((M_ROWS, D_MODEL), F32),
        compiler_params=_params("arbitrary"),
        name="outproj",
    )(mix_p, mix_s, w_bf, b.reshape(1, D_MODEL))


def _route_top2(h, wrt_ref, ri_ref, rw_ref):
    logits = [jnp.sum(h * wrt_ref[e:e + 1, :], axis=-1, keepdims=True) for e in range(N_EXPERTS)]
    m1 = logits[0]
    i1 = jnp.zeros_like(m1, dtype=jnp.int32)
    for e in range(1, N_EXPERTS):
        gt = logits[e] > m1
        m1 = jnp.where(gt, logits[e], m1)
        i1 = jnp.where(gt, e, i1)
    m2 = jnp.full_like(m1, -jnp.inf)
    i2 = jnp.zeros_like(i1)
    for e in range(N_EXPERTS):
        ok = (i1 != e) & (logits[e] > m2)
        m2 = jnp.where(ok, logits[e], m2)
        i2 = jnp.where(ok, e, i2)
    w1 = 1.0 / (1.0 + jnp.exp(m2 - m1))
    ri_ref[:, 0:1] = i1
    ri_ref[:, 1:2] = i2
    rw_ref[:, 0:1] = w1
    rw_ref[:, 1:2] = 1.0 - w1


def _finish_unit(x, y, gate, shift, scale, lg_ref, lb_ref, xo_ref, ho_ref, route_refs):
    xn = _layer_norm(ALPHA * x + (1 + gate) * y, lg_ref[...], lb_ref[...])
    xo_ref[...] = xn
    h = xn * (1 + scale) + shift
    ho_ref[...] = h.astype(BF16)
    if route_refs is not None:
        wrt_ref, hf_ref, ri_ref, rw_ref = route_refs
        hf_ref[...] = h
        _route_top2(h, wrt_ref, ri_ref, rw_ref)


def _finish_dispatch(i, x, y, pg, psh, psc, mpg_ref, mpn_ref, gs_ref, shs_ref, scs_ref,
                     lg_ref, lb_ref, xo_ref, ho_ref, route_refs):
    @pl.when(i < N_PROMPT_UNITS)
    def _():
        _finish_unit(x, y, mpg_ref[pg:pg + 1, :], mpn_ref[psh:psh + 1, :], mpn_ref[psc:psc + 1, :],
                     lg_ref, lb_ref, xo_ref, ho_ref, route_refs)

    @pl.when(i >= N_PROMPT_UNITS)
    def _():
        _finish_unit(x, y, gs_ref[...], shs_ref[...], scs_ref[...], lg_ref, lb_ref, xo_ref, ho_ref, route_refs)


def _epilogue_body(*refs, pg, psh, psc, route):
    x_ref, y_ref, mpg_ref, mpn_ref, gs_ref, shs_ref, scs_ref, lg_ref, lb_ref = refs[:9]
    if route:
        wrt_ref, xo_ref, ho_ref, hf_ref, ri_ref, rw_ref = refs[9:]
        route_refs = (wrt_ref, hf_ref, ri_ref, rw_ref)
    else:
        xo_ref, ho_ref = refs[9:]
        route_refs = None
    _finish_dispatch(pl.program_id(0), x_ref[...], y_ref[...], pg, psh, psc, mpg_ref, mpn_ref,
                     gs_ref, shs_ref, scs_ref, lg_ref, lb_ref, xo_ref, ho_ref, route_refs)


def _epilogue_specs(layer, next_layer, pg, psh, psc):
    row = pl.BlockSpec((1, D_MODEL), lambda i, *_: (0, 0))
    modp = lambda l: pl.BlockSpec((None, None, 6, D_MODEL), lambda i, *_: (l, _prompt_batch_of_unit(i), 0, 0))
    mods = lambda l, p: pl.BlockSpec((None, None, ROWS_PER_UNIT, D_MODEL), lambda i, *_: (l, p, 0, 0))
    return [modp(layer), modp(next_layer), mods(layer, pg), mods(next_layer, psh), mods(next_layer, psc), row, row]


def _epilogue(x, y, modp, mods, ln_g, ln_b, layer, next_layer, pg, psh, psc, w_router_t=None):
    route = w_router_t is not None
    unit = lambda w: pl.BlockSpec((ROWS_PER_UNIT, w), lambda i: (i, 0))
    in_specs = [unit(D_MODEL), unit(D_MODEL)] + _epilogue_specs(layer, next_layer, pg, psh, psc)
    args = [x, y, modp, modp, mods, mods, mods, ln_g.reshape(1, D_MODEL), ln_b.reshape(1, D_MODEL)]
    out_specs = [unit(D_MODEL), unit(D_MODEL)]
    out_shape = [jax.ShapeDtypeStruct((M_ROWS, D_MODEL), F32), jax.ShapeDtypeStruct((M_ROWS, D_MODEL), BF16)]
    if route:
        in_specs.append(pl.BlockSpec((N_EXPERTS, D_MODEL), lambda i: (0, 0)))
        args.append(w_router_t)
        out_specs += [unit(D_MODEL), unit(TOP_K), unit(TOP_K)]
        out_shape += [jax.ShapeDtypeStruct((M_ROWS, D_MODEL), F32),
                      jax.ShapeDtypeStruct((M_ROWS, TOP_K), jnp.int32),
                      jax.ShapeDtypeStruct((M_ROWS, TOP_K), F32)]
    return pl.pallas_call(
        functools.partial(_epilogue_body, pg=pg, psh=psh, psc=psc, route=route),
        grid=(N_UNITS,),
        in_specs=in_specs,
        out_specs=out_specs,
        out_shape=out_shape,
        compiler_params=_params("arbitrary"),
        name="epilogue_route" if route else "epilogue",
    )(*args)


def _swiglu_rows(x, wg, wu, wd):
    g = _dot(x, wg)
    u = _dot(x, wu)
    return _dot((_silu(g) * u).astype(BF16), wd)


def _ffn_body(h_ref, wg_ref, wu_ref, wd_ref, o_ref):
    i = pl.program_id(0)
    j = pl.program_id(1)
    upt = FFN_ROW_TILE // ROWS_PER_UNIT
    full = (i + 1) * upt <= N_UNITS

    def run(n_units):
        @pl.when(j == 0)
        def _():
            o_ref[:n_units * ROWS_PER_UNIT, :] = jnp.zeros((n_units * ROWS_PER_UNIT, D_MODEL), F32)

        for r in range(n_units):
            rows = slice(r * ROWS_PER_UNIT, (r + 1) * ROWS_PER_UNIT)
            o_ref[rows, :] += _swiglu_rows(h_ref[rows, :], wg_ref[...], wu_ref[...], wd_ref[...])

    pl.when(full)(functools.partial(run, upt))
    pl.when(jnp.logical_not(full))(functools.partial(run, N_UNITS % upt))


def _ffn(h, wg, wu, wd):
    nj = D_FF_PAD // FFN_TILE
    return pl.pallas_call(
        _ffn_body,
        grid=(pl.cdiv(M_ROWS, FFN_ROW_TILE), nj),
        in_specs=[pl.BlockSpec((FFN_ROW_TILE, D_MODEL), lambda i, j: (i, 0)),
                  pl.BlockSpec((D_MODEL, FFN_TILE), lambda i, j: (0, j)),
                  pl.BlockSpec((D_MODEL, FFN_TILE), lambda i, j: (0, j)),
                  pl.BlockSpec((FFN_TILE, D_MODEL), lambda i, j: (j, 0))],
        out_specs=pl.BlockSpec((FFN_ROW_TILE, D_MODEL), lambda i, j: (i, 0)),
        out_shape=jax.ShapeDtypeStruct((M_ROWS, D_MODEL), F32),
        compiler_params=_params("arbitrary", "arbitrary"),
        name="ffn_dense",
    )(h, wg, wu, wd)


def _row_copy(src_hbm, row, dst, dst_row, sem):
    return pltpu.make_async_copy(src_hbm.at[pl.ds(row, 1), :], dst.at[pl.ds(dst_row, 1), :], sem)


def _moe_gather_body(src_ref, uv_ref, h_hbm, o_ref, buf, sem):
    i = pl.program_id(0)
    base = i * ROWS_PER_UNIT

    @pl.when(uv_ref[i] > 0)
    def _():
        def start(r, c):
            _row_copy(h_hbm, src_ref[base + r], buf, r, sem).start()
            return c

        def wait(r, c):
            _row_copy(h_hbm, 0, buf, r, sem).wait()
            return c

        lax.fori_loop(0, ROWS_PER_UNIT, start, 0)
        lax.fori_loop(0, ROWS_PER_UNIT, wait, 0)
        o_ref[...] = buf[...].astype(BF16)

    @pl.when(uv_ref[i] <= 0)
    def _():
        o_ref[...] = jnp.zeros_like(o_ref)


def _moe_gather(src_tok, unit_valid, h_f32):
    grid_spec = pltpu.PrefetchScalarGridSpec(
        num_scalar_prefetch=2,
        grid=(MOE_UNITS,),
        in_specs=[pl.BlockSpec(memory_space=pl.ANY)],
        out_specs=pl.BlockSpec((ROWS_PER_UNIT, D_MODEL), lambda i, s, v: (i, 0)),
        scratch_shapes=[pltpu.VMEM((ROWS_PER_UNIT, D_MODEL), F32), pltpu.SemaphoreType.DMA(())],
    )
    return pl.pallas_call(
        _moe_gather_body,
        grid_spec=grid_spec,
        out_shape=jax.ShapeDtypeStruct((MOE_ROWS, D_MODEL), BF16),
        compiler_params=_params("arbitrary"),
        name="moe_gather",
    )(src_tok, unit_valid, h_f32)


def _unit_copy(src, src_unit, dst, dst_unit, sem):
    rows = lambda u: pl.ds(pl.multiple_of(u * ROWS_PER_UNIT, ROWS_PER_UNIT), ROWS_PER_UNIT)
    return pltpu.make_async_copy(src.at[rows(src_unit), :], dst.at[rows(dst_unit), :], sem)


def _moe_gmm_body(te_ref, tu_ref, tn_ref, nu_ref, xs_hbm, wg_ref, wu_ref, wd_ref, ys_hbm,
                  xsc, acc, wgu_b, wd_b, sem_in, sem_out):
    t = pl.program_id(0)
    j = pl.program_id(1)
    nj = pl.num_programs(1)
    n = tn_ref[t]
    u0 = tu_ref[t]
    tf = MOE_FF_TILE

    @pl.when((j == 0) & (n > 0))
    def _():
        def start(u, c):
            _unit_copy(xs_hbm, u0 + u, xsc, u, sem_in).start()
            return c

        def zero(u, c):
            acc[pl.ds(pl.multiple_of(u * ROWS_PER_UNIT, ROWS_PER_UNIT), ROWS_PER_UNIT), :] = jnp.zeros(
                (ROWS_PER_UNIT, D_MODEL), F32)
            return c

        def wait(u, c):
            _unit_copy(xs_hbm, u0 + u, xsc, u, sem_in).wait()
            return c

        lax.fori_loop(0, n, start, 0)
        lax.fori_loop(0, n, zero, 0)
        lax.fori_loop(0, n, wait, 0)

    @pl.when((j == 0) & (t == 0))
    def _():
        def start(u, c):
            _unit_copy(acc, 0, ys_hbm, u, sem_out).start()
            return c

        def wait(u, c):
            _unit_copy(acc, 0, ys_hbm, u, sem_out).wait()
            return c

        lax.fori_loop(nu_ref[0], MOE_UNITS, start, 0)
        lax.fori_loop(nu_ref[0], MOE_UNITS, wait, 0)

    @pl.when(n > 0)
    def _():
        wgu_b[:, :tf] = wg_ref[...].astype(BF16)
        wgu_b[:, tf:] = wu_ref[...].astype(BF16)
        wd_b[...] = wd_ref[...].astype(BF16)

        def unit_rows(r0):
            rs = pl.ds(r0, ROWS_PER_UNIT)
            gu = _dot(xsc[rs, :], wgu_b[...])
            a = (_silu(gu[:, :tf]) * gu[:, tf:]).astype(BF16)
            acc[rs, :] += _dot(a, wd_b[...])

        def pair(p, c):
            r0 = pl.multiple_of(p * (2 * ROWS_PER_UNIT), 2 * ROWS_PER_UNIT)
            unit_rows(r0)
            unit_rows(r0 + ROWS_PER_UNIT)
            return c

        lax.fori_loop(0, lax.shift_right_logical(n, 1), pair, 0)

        @pl.when((n & 1) == 1)
        def _():
            unit_rows(pl.multiple_of((n - 1) * ROWS_PER_UNIT, ROWS_PER_UNIT))

    @pl.when((j == nj - 1) & (n > 0))
    def _():
        def start(u, c):
            _unit_copy(acc, u, ys_hbm, u0 + u, sem_out).start()
            return c

        def wait(u, c):
            _unit_copy(acc, u, ys_hbm, u0 + u, sem_out).wait()
            return c

        lax.fori_loop(0, n, start, 0)
        lax.fori_loop(0, n, wait, 0)


def _moe_gmm(tile_expert, tile_unit0, tile_nunits, total_units, xs, w_gu, w_down, lm):
    nj = D_FF_EXPERT // MOE_FF_TILE
    jj = lambda t, j, tn: jnp.where(tn[t] > 0, j, nj - 1)
    tile_rows = MOE_TILE_UNITS * ROWS_PER_UNIT
    grid_spec = pltpu.PrefetchScalarGridSpec(
        num_scalar_prefetch=4,
        grid=(MOE_TILES, nj),
        in_specs=[pl.BlockSpec(memory_space=pl.ANY),
                  pl.BlockSpec((None, None, D_MODEL, MOE_FF_TILE),
                               lambda t, j, te, tu, tn, nu: (lm, te[t], 0, jj(t, j, tn))),
                  pl.BlockSpec((None, None, D_MODEL, MOE_FF_TILE),
                               lambda t, j, te, tu, tn, nu: (lm, te[t], 0, nj + jj(t, j, tn))),
                  pl.BlockSpec((None, None, MOE_FF_TILE, D_MODEL),
                               lambda t, j, te, tu, tn, nu: (lm, te[t], jj(t, j, tn), 0))],
        out_specs=pl.BlockSpec(memory_space=pl.ANY),
        scratch_shapes=[pltpu.VMEM((tile_rows, D_MODEL), BF16), pltpu.VMEM((tile_rows, D_MODEL), F32),
                        pltpu.VMEM((D_MODEL, 2 * MOE_FF_TILE), BF16), pltpu.VMEM((MOE_FF_TILE, D_MODEL), BF16),
                        pltpu.SemaphoreType.DMA(()), pltpu.SemaphoreType.DMA(())],
    )
    return pl.pallas_call(
        _moe_gmm_body,
        grid_spec=grid_spec,
        out_shape=jax.ShapeDtypeStruct((MOE_ROWS, D_MODEL), F32),
        compiler_params=_params("arbitrary", "arbitrary"),
        name="moe_gmm",
    )(tile_expert, tile_unit0, tile_nunits, total_units, xs, w_gu, w_gu, w_down)


def _moe_combine_body(dest_ref, ys_hbm, rw_ref, x_ref, mpg_ref, mpn_ref, gs_ref, shs_ref, scs_ref, lg_ref, lb_ref,
                      xo_ref, ho_ref, buf, sem, *, pg, psh, psc):
    i = pl.program_id(0)
    base = i * ROWS_PER_UNIT * TOP_K

    def start(r, c):
        for k in range(TOP_K):
            _row_copy(ys_hbm, dest_ref[base + TOP_K * r + k], buf.at[k], r, sem).start()
        return c

    def wait(r, c):
        for k in range(TOP_K):
            _row_copy(ys_hbm, 0, buf.at[k], r, sem).wait()
        return c

    lax.fori_loop(0, ROWS_PER_UNIT, start, 0)
    lax.fori_loop(0, ROWS_PER_UNIT, wait, 0)
    y = rw_ref[:, 0:1] * buf[0] + rw_ref[:, 1:2] * buf[1]
    _finish_dispatch(i, x_ref[...], y, pg, psh, psc, mpg_ref, mpn_ref, gs_ref, shs_ref, scs_ref,
                     lg_ref, lb_ref, xo_ref, ho_ref, None)


def _moe_combine(dest, ys, route_w, x, modp, mods, ln_g, ln_b, layer, next_layer, pg, psh, psc):
    unit = lambda w: pl.BlockSpec((ROWS_PER_UNIT, w), lambda i, d: (i, 0))
    grid_spec = pltpu.PrefetchScalarGridSpec(
        num_scalar_prefetch=1,
        grid=(N_UNITS,),
        in_specs=[pl.BlockSpec(memory_space=pl.ANY), unit(TOP_K), unit(D_MODEL)]
        + _epilogue_specs(layer, next_layer, pg, psh, psc),
        out_specs=[unit(D_MODEL), unit(D_MODEL)],
        scratch_shapes=[pltpu.VMEM((TOP_K, ROWS_PER_UNIT, D_MODEL), F32), pltpu.SemaphoreType.DMA(())],
    )
    return pl.pallas_call(
        functools.partial(_moe_combine_body, pg=pg, psh=psh, psc=psc),
        grid_spec=grid_spec,
        out_shape=[jax.ShapeDtypeStruct((M_ROWS, D_MODEL), F32), jax.ShapeDtypeStruct((M_ROWS, D_MODEL), BF16)],
        compiler_params=_params("arbitrary"),
        name="moe_combine",
    )(dest, ys, route_w, x, modp, modp, mods, mods, mods, ln_g.reshape(1, D_MODEL), ln_b.reshape(1, D_MODEL))


def _moe_plan(route_i):
    i32 = jnp.int32
    e_flat = route_i.reshape(-1)
    onehot = (e_flat[:, None] == jnp.arange(N_EXPERTS, dtype=i32)[None, :]).astype(i32)
    csum = jnp.cumsum(onehot, axis=0)
    rank = jnp.sum(csum * onehot, axis=1) - 1
    counts = csum[-1]
    units_e = (counts + ROWS_PER_UNIT - 1) // ROWS_PER_UNIT
    unit_end = jnp.cumsum(units_e)
    unit_start = unit_end - units_e
    dest = (unit_start[e_flat] * ROWS_PER_UNIT + rank).astype(i32)
    pair_tok = jnp.arange(N_PAIRS, dtype=i32) // TOP_K
    src_tok = jnp.zeros((MOE_ROWS,), i32).at[dest].set(pair_tok)
    total_units = unit_end[-1:].astype(i32)
    unit_valid = (jnp.arange(MOE_UNITS, dtype=i32) < total_units).astype(i32)
    tiles_e = (units_e + MOE_TILE_UNITS - 1) // MOE_TILE_UNITS
    tile_end = jnp.cumsum(tiles_e)
    tile_start = tile_end - tiles_e
    n_tiles = tile_end[-1]
    t_ids = jnp.arange(MOE_TILES, dtype=i32)
    t_eff = jnp.minimum(t_ids, n_tiles - 1)
    tile_expert = jnp.sum((t_eff[:, None] >= tile_end[None, :]).astype(i32), axis=1)
    k = t_eff - tile_start[tile_expert]
    tile_unit0 = unit_start[tile_expert] + k * MOE_TILE_UNITS
    units_left = units_e[tile_expert] - k * MOE_TILE_UNITS
    tile_nunits = jnp.where(t_ids < n_tiles, jnp.minimum(units_left, MOE_TILE_UNITS), 0)
    return (dest, src_tok, unit_valid, total_units, tile_expert.astype(i32), tile_unit0.astype(i32),
            tile_nunits.astype(i32))


def kernel(x_prompt, x_sample, cache_win_k, cache_win_v, c_prompt, c_sample, w_ada, b_ada, w_in, b_in, v_norm_g, v_norm_b, w_spatial, b_spatial, attn_sinks, w_out, b_out, ln1_g, ln1_b, ln2_g, ln2_b, w_ffn_gu, w_ffn_down, w_router, w_exp_gu, w_exp_down):
    c_rows = jnp.concatenate([c_prompt, c_sample, jnp.zeros((ADA_ROWS - BATCH - DEC_BATCH, D_MODEL), F32)], axis=0)
    mod = _adaln(c_rows, w_ada, b_ada)
    modp = mod[:, :BATCH].reshape(DEPTH, BATCH, 6, D_MODEL)
    mods = jnp.repeat(mod[:, BATCH:BATCH + DEC_BATCH].reshape(DEPTH, DEC_BATCH, 6, D_MODEL), DEC_SEQ, axis=1)
    mods = mods.transpose(0, 2, 1, 3)

    x = jnp.concatenate([x_prompt.reshape(M_PROMPT, D_MODEL), x_sample.reshape(M_SAMPLE, D_MODEL)], axis=0)
    h = _modulate0(x, modp, mods)

    rows = GQA * DEC_SEQ
    kp_list, vp_list, ks_list, vs_list, cv_list = [], [], [], [], []
    for l in range(DEPTH):
        nl = min(l + 1, DEPTH - 1)
        u, v, q, kv = _proj(h, w_in[l].astype(BF16), b_in[l], v_norm_g[l], v_norm_b[l])
        b_s_t = b_spatial[l].T
        mix_p = _mix_prompt(attn_sinks[l].reshape(-1), u, v, q, kv, w_spatial[l], b_s_t)

        u_s = u[M_PROMPT:].reshape(DEC_BATCH, DEC_SEQ, A_WIDTH)
        v_s = v[M_PROMPT:].reshape(DEC_BATCH, DEC_SEQ, A_WIDTH)
        kv_s = kv[M_PROMPT:].reshape(DEC_BATCH, DEC_SEQ, 2 * KV_WIDTH)
        q_s = q[M_PROMPT:].reshape(DEC_BATCH, DEC_SEQ, N_KV_HEADS, GQA, HEAD_DIM)
        q_s = q_s.transpose(0, 2, 3, 1, 4).reshape(DEC_BATCH, N_KV_HEADS, rows, HEAD_DIM)
        ck = cache_win_k[l].reshape(DEC_BATCH, WINDOW, KV_WIDTH)
        cv = cache_win_v[l].reshape(DEC_BATCH, WINDOW, KV_WIDTH)
        sink_rows = jnp.repeat(attn_sinks[l], DEC_SEQ, axis=1).reshape(N_KV_HEADS, rows, 1)
        a_s, o_s = _mix_sample(u_s, v_s, q_s, kv_s, ck, cv, w_spatial[l], b_s_t, sink_rows)
        o_s = o_s.reshape(DEC_BATCH, N_KV_HEADS, GQA, DEC_SEQ, HEAD_DIM).transpose(0, 3, 1, 2, 4)
        mix_s = jnp.concatenate([a_s.reshape(M_SAMPLE, A_WIDTH), o_s.reshape(M_SAMPLE, B_WIDTH)], axis=1)

        y = _outproj(mix_p, mix_s, w_out[l].astype(BF16), b_out[l])
        if l % 2 == 0:
            x, h = _epilogue(x, y, modp, mods, ln1_g[l], ln1_b[l], l, l, P_GATE1, P_SHIFT2, P_SCALE2)
            ld = l // 2
            pad = ((0, 0), (0, D_FF_PAD - D_FF))
            wg = jnp.pad(w_ffn_gu[ld][:, :D_FF].astype(BF16), pad)
            wu = jnp.pad(w_ffn_gu[ld][:, D_FF:].astype(BF16), pad)
            wd = jnp.pad(w_ffn_down[ld].astype(BF16), ((0, D_FF_PAD - D_FF), (0, 0)))
            f = _ffn(h, wg, wu, wd)
            x, h = _epilogue(x, f, modp, mods, ln2_g[l], ln2_b[l], l, nl, P_GATE2, P_SHIFT1, P_SCALE1)
        else:
            lm = l // 2
            x, h, h_f32, route_i, route_w = _epilogue(x, y, modp, mods, ln1_g[l], ln1_b[l], l, l,
                                                      P_GATE1, P_SHIFT2, P_SCALE2, w_router_t=w_router[lm].T)
            dest, src_tok, unit_valid, total_units, tile_expert, tile_unit0, tile_nunits = _moe_plan(route_i)
            xs = _moe_gather(src_tok, unit_valid, h_f32)
            ys = _moe_gmm(tile_expert, tile_unit0, tile_nunits, total_units, xs, w_exp_gu, w_exp_down, lm)
            x, h = _moe_combine(dest, ys, route_w, x, modp, mods, ln2_g[l], ln2_b[l], l, nl,
                                P_GATE2, P_SHIFT1, P_SCALE1)

        kv_p = kv[:M_PROMPT].reshape(BATCH, SEQ, 2 * KV_WIDTH)[:, SEQ - WINDOW:]
        kp_list.append(kv_p[..., :KV_WIDTH].reshape(BATCH, WINDOW, N_KV_HEADS, HEAD_DIM))
        vp_list.append(kv_p[..., KV_WIDTH:].reshape(BATCH, WINDOW, N_KV_HEADS, HEAD_DIM))
        k_new = kv_s[..., :KV_WIDTH].reshape(DEC_BATCH, DEC_SEQ, N_KV_HEADS, HEAD_DIM)
        v_new = kv_s[..., KV_WIDTH:].reshape(DEC_BATCH, DEC_SEQ, N_KV_HEADS, HEAD_DIM)
        ks_list.append(jnp.concatenate([cache_win_k[l][:, DEC_SEQ:], k_new], axis=1))
        vs_list.append(jnp.concatenate([cache_win_v[l][:, DEC_SEQ:], v_new], axis=1))
        cv_list.append(v_s.astype(F32))

    y_prompt = x[:M_PROMPT].reshape(BATCH, SEQ, D_MODEL)
    y_sample = x[M_PROMPT:].reshape(DEC_BATCH, DEC_SEQ, D_MODEL)
    return (y_prompt, y_sample, jnp.stack(kp_list), jnp.stack(vp_list), jnp.stack(ks_list), jnp.stack(vs_list),
            jnp.stack(cv_list))
```

```python
import functools

import jax
import jax.numpy as jnp
from jax import lax
from jax.experimental import pallas as pl
from jax.experimental.pallas import tpu as pltpu

F32 = jnp.float32
BF16 = jnp.bfloat16

D_MODEL = 2048
BATCH = 4
SEQ = 2048
DEPTH = 4
DEC_BATCH = 32
DEC_SEQ = 8
CHUNK = 128
A_WIDTH = D_MODEL // 2
A_GROUPS = 4
A_GROUP_DIM = A_WIDTH // A_GROUPS
HEAD_DIM = 64
B_WIDTH = D_MODEL - A_WIDTH
N_HEADS = B_WIDTH // HEAD_DIM
N_KV_HEADS = 2
GQA = N_HEADS // N_KV_HEADS
KV_WIDTH = N_KV_HEADS * HEAD_DIM
WINDOW = 128
IN_COLS = 2 * A_WIDTH + B_WIDTH + 2 * KV_WIDTH
D_FF = 5504
N_EXPERTS = 8
TOP_K = 2
D_FF_EXPERT = 7168
ALPHA = (2.0 * DEPTH) ** 0.25
LN_EPS = 1e-5
NEG_INF = -1e30

M_PROMPT = BATCH * SEQ
M_SAMPLE = DEC_BATCH * DEC_SEQ
M_ROWS = M_PROMPT + M_SAMPLE
ROWS_PER_UNIT = 256
N_PROMPT_UNITS = M_PROMPT // ROWS_PER_UNIT
N_UNITS = M_ROWS // ROWS_PER_UNIT
UNITS_PER_SEQ = SEQ // ROWS_PER_UNIT
ADA_ROWS = 40
P_SHIFT1, P_SCALE1, P_GATE1, P_SHIFT2, P_SCALE2, P_GATE2 = range(6)

D_FF_PAD = 5632
FFN_TILE = 512
FFN_ROW_TILE = 4 * ROWS_PER_UNIT
MOE_FF_TILE = 256
N_PAIRS = M_ROWS * TOP_K
MOE_UNITS = N_PAIRS // ROWS_PER_UNIT + N_EXPERTS
MOE_ROWS = MOE_UNITS * ROWS_PER_UNIT
MOE_TILE_UNITS = 9
MOE_TILES = MOE_UNITS // MOE_TILE_UNITS + N_EXPERTS

VMEM_LIMIT = 56 * 1024 * 1024


def _params(*sem):
    return pltpu.CompilerParams(dimension_semantics=sem, vmem_limit_bytes=VMEM_LIMIT)


def _dot(a, b):
    return jnp.dot(a, b, preferred_element_type=F32)


def _layer_norm(z, g, b):
    mu = jnp.mean(z, axis=-1, keepdims=True)
    zc = z - mu
    var = jnp.mean(zc * zc, axis=-1, keepdims=True)
    return zc * lax.rsqrt(var + LN_EPS) * g + b


def _gelu_tanh(x):
    return x * (0.5 * (1.0 + jnp.tanh(0.7978845608028654 * (x + 0.044715 * (x * x * x)))))


def _silu(x):
    return x * jax.nn.sigmoid(x)


def _prompt_batch_of_unit(i):
    return jnp.minimum(i // UNITS_PER_SEQ, BATCH - 1)


def _adaln_body(c_ref, w_ref, b_ref, o_ref):
    s = _silu(c_ref[...]).astype(BF16)
    o_ref[...] = _dot(s, w_ref[...].astype(BF16)) + b_ref[...]


def _adaln(c_rows, w_ada, b_ada):
    tn = 1024
    return pl.pallas_call(
        _adaln_body,
        grid=(DEPTH, 6 * D_MODEL // tn),
        in_specs=[pl.BlockSpec((ADA_ROWS, D_MODEL), lambda l, n: (0, 0)),
                  pl.BlockSpec((None, D_MODEL, tn), lambda l, n: (l, 0, n)),
                  pl.BlockSpec((None, 1, tn), lambda l, n: (l, 0, n))],
        out_specs=pl.BlockSpec((None, ADA_ROWS, tn), lambda l, n: (l, 0, n)),
        out_shape=jax.ShapeDtypeStruct((DEPTH, ADA_ROWS, 6 * D_MODEL), F32),
        compiler_params=_params("arbitrary", "arbitrary"),
        name="adaln",
    )(c_rows, w_ada, b_ada.reshape(DEPTH, 1, 6 * D_MODEL))


def _mod0_body(x_ref, mp_ref, ss_ref, cs_ref, h_ref):
    i = pl.program_id(0)
    x = x_ref[...]

    @pl.when(i < N_PROMPT_UNITS)
    def _():
        h_ref[...] = (x * (1 + mp_ref[P_SCALE1:P_SCALE1 + 1, :]) + mp_ref[P_SHIFT1:P_SHIFT1 + 1, :]).astype(BF16)

    @pl.when(i >= N_PROMPT_UNITS)
    def _():
        h_ref[...] = (x * (1 + cs_ref[...]) + ss_ref[...]).astype(BF16)


def _unit_spec(width):
    return pl.BlockSpec((ROWS_PER_UNIT, width), lambda i: (i, 0))


def _modp_spec(layer):
    return pl.BlockSpec((None, None, 6, D_MODEL), lambda i: (layer, _prompt_batch_of_unit(i), 0, 0))


def _mods_spec(layer, p):
    return pl.BlockSpec((None, None, ROWS_PER_UNIT, D_MODEL), lambda i: (layer, p, 0, 0))


def _modulate0(x, modp, mods):
    return pl.pallas_call(
        _mod0_body,
        grid=(N_UNITS,),
        in_specs=[_unit_spec(D_MODEL), _modp_spec(0), _mods_spec(0, P_SHIFT1), _mods_spec(0, P_SCALE1)],
        out_specs=_unit_spec(D_MODEL),
        out_shape=jax.ShapeDtypeStruct((M_ROWS, D_MODEL), BF16),
        compiler_params=_params("arbitrary"),
        name="modulate0",
    )(x, modp, mods, mods)


def _cast_weight_once(w32_ref, w_ref, chunk):
    @pl.when(pl.program_id(0) == 0)
    def _():
        for c in range(0, w32_ref.shape[1], chunk):
            w_ref[:, c:c + chunk] = w32_ref[:, c:c + chunk].astype(BF16)


def _resident(shape):
    return pl.BlockSpec(shape, lambda i: (0,) * len(shape), pipeline_mode=pl.Buffered(1))


def _proj_body(h_ref, w32_ref, b_ref, gv_ref, bv_ref, u_ref, v_ref, q_ref, kv_ref, w_ref):
    _cast_weight_once(w32_ref, w_ref, A_GROUP_DIM)
    h = h_ref[...]
    gd = A_GROUP_DIM
    for c in range(A_GROUPS):
        sl = slice(c * gd, (c + 1) * gd)
        z = _dot(h, w_ref[:, sl]) + b_ref[:, sl]
        u_ref[:, sl] = _gelu_tanh(z).astype(BF16)
    for g in range(A_GROUPS):
        sl = slice(A_WIDTH + g * gd, A_WIDTH + (g + 1) * gd)
        z = _gelu_tanh(_dot(h, w_ref[:, sl]) + b_ref[:, sl])
        vn = _layer_norm(z, gv_ref[g:g + 1, :], bv_ref[g:g + 1, :])
        v_ref[:, g * gd:(g + 1) * gd] = vn.astype(BF16)
    o1 = 2 * A_WIDTH
    o2 = o1 + B_WIDTH
    z = _dot(h, w_ref[:, o1:o2]) + b_ref[:, o1:o2]
    q_ref[...] = (z * (HEAD_DIM ** -0.5)).astype(BF16)
    kv_ref[...] = _dot(h, w_ref[:, o2:]) + b_ref[:, o2:]


def _proj(h, w, b, gv, bv):
    const = lambda shape: pl.BlockSpec(shape, lambda i: (0,) * len(shape))
    return pl.pallas_call(
        _proj_body,
        grid=(N_UNITS,),
        in_specs=[_unit_spec(D_MODEL), _resident((D_MODEL, IN_COLS)), const((1, IN_COLS)),
                  const((A_GROUPS, A_GROUP_DIM)), const((A_GROUPS, A_GROUP_DIM))],
        out_specs=[_unit_spec(A_WIDTH), _unit_spec(A_WIDTH), _unit_spec(B_WIDTH), _unit_spec(2 * KV_WIDTH)],
        out_shape=[jax.ShapeDtypeStruct((M_ROWS, A_WIDTH), BF16), jax.ShapeDtypeStruct((M_ROWS, A_WIDTH), BF16),
                   jax.ShapeDtypeStruct((M_ROWS, B_WIDTH), BF16), jax.ShapeDtypeStruct((M_ROWS, 2 * KV_WIDTH), F32)],
        scratch_shapes=[pltpu.VMEM((D_MODEL, IN_COLS), BF16)],
        compiler_params=_params("arbitrary"),
        name="proj",
    )(h, w, b.reshape(1, IN_COLS), gv, bv)


def _mix_prompt_body(sink_ref, u_ref, v_ref, q_ref, kvc_ref, kvp_ref, ws_ref, bs_ref, o_ref):
    blk = pl.program_id(1)
    gd = A_GROUP_DIM
    tr = lax.broadcasted_iota(jnp.int32, (CHUNK, CHUNK), 0)
    tc = lax.broadcasted_iota(jnp.int32, (CHUNK, CHUNK), 1)
    for g in range(A_GROUPS):
        sl = slice(g * gd, (g + 1) * gd)
        w = jnp.where(tr >= tc, ws_ref[g], 0.0).astype(BF16)
        s = _dot(w, v_ref[:, sl]) + bs_ref[:, g:g + 1]
        o_ref[:, sl] = (u_ref[:, sl].astype(F32) * s).astype(BF16)

    kvc = kvc_ref[...]
    kvp = kvp_ref[...]
    row = lax.broadcasted_iota(jnp.int32, (WINDOW, 2 * WINDOW), 0)
    col = lax.broadcasted_iota(jnp.int32, (WINDOW, 2 * WINDOW), 1)
    dlt = row + WINDOW - col
    mask = (dlt >= 0) & (dlt <= WINDOW) & ((col >= WINDOW) | (blk > 0))
    hd = HEAD_DIM
    for j in range(N_KV_HEADS):
        ks = slice(j * hd, (j + 1) * hd)
        vs = slice(KV_WIDTH + j * hd, KV_WIDTH + (j + 1) * hd)
        kk = jnp.concatenate([kvp[:, ks], kvc[:, ks]], axis=0).astype(BF16)
        vv = jnp.concatenate([kvp[:, vs], kvc[:, vs]], axis=0).astype(BF16)
        for hh in range(GQA):
            h = j * GQA + hh
            qh = q_ref[:, h * hd:(h + 1) * hd]
            s = lax.dot_general(qh, kk, (((1,), (1,)), ((), ())), preferred_element_type=F32)
            s = jnp.where(mask, s, NEG_INF)
            sk = sink_ref[h]
            m = jnp.maximum(jnp.max(s, axis=-1, keepdims=True), sk)
            p = jnp.exp(s - m)
            den = jnp.sum(p, axis=-1, keepdims=True) + jnp.exp(sk - m)
            o = _dot(p.astype(BF16), vv) / den
            o_ref[:, A_WIDTH + h * hd:A_WIDTH + (h + 1) * hd] = o.astype(BF16)


def _mix_prompt(sinks, u, v, q, kv, w_s, b_s_t):
    nb = SEQ // WINDOW
    row_blk = lambda w: pl.BlockSpec((WINDOW, w), lambda b, i, s: (b * nb + i, 0))
    grid_spec = pltpu.PrefetchScalarGridSpec(
        num_scalar_prefetch=1,
        grid=(BATCH, nb),
        in_specs=[row_blk(A_WIDTH), row_blk(A_WIDTH), row_blk(B_WIDTH), row_blk(2 * KV_WIDTH),
                  pl.BlockSpec((WINDOW, 2 * KV_WIDTH), lambda b, i, s: (b * nb + jnp.maximum(i - 1, 0), 0)),
                  pl.BlockSpec((A_GROUPS, CHUNK, CHUNK), lambda b, i, s: (0, 0, 0)),
                  pl.BlockSpec((CHUNK, A_GROUPS), lambda b, i, s: (0, 0))],
        out_specs=pl.BlockSpec((WINDOW, D_MODEL), lambda b, i, s: (b * nb + i, 0)),
    )
    return pl.pallas_call(
        _mix_prompt_body,
        grid_spec=grid_spec,
        out_shape=jax.ShapeDtypeStruct((M_PROMPT, D_MODEL), BF16),
        compiler_params=_params("arbitrary", "arbitrary"),
        name="mix_prompt",
    )(sinks, u, v, q, kv, kv, w_s, b_s_t)


SAMPLE_KEYS = 2 * WINDOW


def _mix_sample_body(u_ref, v_ref, q_ref, kn_ref, ck_ref, cv_ref, ws_ref, bs_ref, sk_ref, a_ref, o_ref):
    gd = A_GROUP_DIM
    ds = DEC_SEQ
    v = v_ref[...].astype(F32)
    u = u_ref[...].astype(F32)
    tr = lax.broadcasted_iota(jnp.int32, (ds, ds), 0)
    tc = lax.broadcasted_iota(jnp.int32, (ds, ds), 1)
    for g in range(A_GROUPS):
        sl = slice(g * gd, (g + 1) * gd)
        w = jnp.where(tr >= tc, ws_ref[g][:ds, :ds], 0.0)
        s = jnp.zeros((ds, gd), F32) + bs_ref[:ds, g:g + 1]
        for t in range(ds):
            s = s + w[:, t:t + 1] * v[t:t + 1, sl]
        a_ref[:, sl] = (u[:, sl] * s).astype(BF16)

    kn = kn_ref[...]
    ck = ck_ref[...]
    cv = cv_ref[...]
    hd = HEAD_DIM
    rows = GQA * ds
    qt = lax.broadcasted_iota(jnp.int32, (rows, SAMPLE_KEYS), 0) & (ds - 1)
    col = lax.broadcasted_iota(jnp.int32, (rows, SAMPLE_KEYS), 1)
    mask = ((col < WINDOW) & (col >= qt)) | ((col >= WINDOW) & (col - WINDOW <= qt))
    pad = jnp.zeros((SAMPLE_KEYS - WINDOW - ds, hd), F32)
    for j in range(N_KV_HEADS):
        ks = slice(j * hd, (j + 1) * hd)
        vs = slice(KV_WIDTH + j * hd, KV_WIDTH + (j + 1) * hd)
        kk = jnp.concatenate([ck[:, ks], kn[:, ks], pad], axis=0).astype(BF16)
        vv = jnp.concatenate([cv[:, ks], kn[:, vs], pad], axis=0).astype(BF16)
        s = lax.dot_general(q_ref[j], kk, (((1,), (1,)), ((), ())), preferred_element_type=F32)
        s = jnp.where(mask, s, NEG_INF)
        sk = sk_ref[j]
        m = jnp.maximum(jnp.max(s, axis=-1, keepdims=True), sk)
        p = jnp.exp(s - m)
        den = jnp.sum(p, axis=-1, keepdims=True) + jnp.exp(sk - m)
        o_ref[j] = (_dot(p.astype(BF16), vv) / den).astype(BF16)


def _mix_sample(u_s, v_s, q_s, kv_s, cache_k, cache_v, w_s, b_s_t, sink_rows):
    rows = GQA * DEC_SEQ
    per_b = lambda *shape: pl.BlockSpec((None,) + shape, lambda b: (b,) + (0,) * len(shape))
    const = lambda *shape: pl.BlockSpec(shape, lambda b: (0,) * len(shape))
    return pl.pallas_call(
        _mix_sample_body,
        grid=(DEC_BATCH,),
        in_specs=[per_b(DEC_SEQ, A_WIDTH), per_b(DEC_SEQ, A_WIDTH), per_b(N_KV_HEADS, rows, HEAD_DIM),
                  per_b(DEC_SEQ, 2 * KV_WIDTH), per_b(WINDOW, KV_WIDTH), per_b(WINDOW, KV_WIDTH),
                  const(A_GROUPS, CHUNK, CHUNK), const(CHUNK, A_GROUPS), const(N_KV_HEADS, rows, 1)],
        out_specs=[per_b(DEC_SEQ, A_WIDTH), per_b(N_KV_HEADS, rows, HEAD_DIM)],
        out_shape=[jax.ShapeDtypeStruct((DEC_BATCH, DEC_SEQ, A_WIDTH), BF16),
                   jax.ShapeDtypeStruct((DEC_BATCH, N_KV_HEADS, rows, HEAD_DIM), BF16)],
        compiler_params=_params("arbitrary"),
        name="mix_sample",
    )(u_s, v_s, q_s, kv_s, cache_k, cache_v, w_s, b_s_t, sink_rows)


def _route_top2(h, wrt_ref, ri_ref, rw_ref):
    logits = [jnp.sum(h * wrt_ref[e:e + 1, :], axis=-1, keepdims=True) for e in range(N_EXPERTS)]
    m1 = logits[0]
    i1 = jnp.zeros_like(m1, dtype=jnp.int32)
    for e in range(1, N_EXPERTS):
        gt = logits[e] > m1
        m1 = jnp.where(gt, logits[e], m1)
        i1 = jnp.where(gt, e, i1)
    m2 = jnp.full_like(m1, -jnp.inf)
    i2 = jnp.zeros_like(i1)
    for e in range(N_EXPERTS):
        ok = (i1 != e) & (logits[e] > m2)
        m2 = jnp.where(ok, logits[e], m2)
        i2 = jnp.where(ok, e, i2)
    w1 = 1.0 / (1.0 + jnp.exp(m2 - m1))
    ri_ref[:, 0:1] = i1
    ri_ref[:, 1:2] = i2
    rw_ref[:, 0:1] = w1
    rw_ref[:, 1:2] = 1.0 - w1


def _finish_unit(x, y, gate, shift, scale, lg_ref, lb_ref, xo_ref, ho_ref, route_refs):
    xn = _layer_norm(ALPHA * x + (1 + gate) * y, lg_ref[...], lb_ref[...])
    xo_ref[...] = xn
    h = xn * (1 + scale) + shift
    ho_ref[...] = h.astype(BF16)
    if route_refs is not None:
        wrt_ref, hf_ref, ri_ref, rw_ref = route_refs
        hf_ref[...] = h
        _route_top2(h, wrt_ref, ri_ref, rw_ref)


def _finish_dispatch(i, x, y_of, pg, psh, psc, mpg_ref, mpn_ref, gs_ref, shs_ref, scs_ref,
                     lg_ref, lb_ref, xo_ref, ho_ref, route_refs):
    @pl.when(i < N_PROMPT_UNITS)
    def _():
        _finish_unit(x, y_of(True), mpg_ref[pg:pg + 1, :], mpn_ref[psh:psh + 1, :], mpn_ref[psc:psc + 1, :],
                     lg_ref, lb_ref, xo_ref, ho_ref, route_refs)

    @pl.when(i >= N_PROMPT_UNITS)
    def _():
        _finish_unit(x, y_of(False), gs_ref[...], shs_ref[...], scs_ref[...], lg_ref, lb_ref, xo_ref, ho_ref,
                     route_refs)


def _epilogue_body(*refs, pg, psh, psc, route, project):
    refs = list(refs)
    if project:
        mp_ref, ms_ref, w32_ref, b_ref = refs[:4]
        w_ref = refs.pop()
        refs = refs[4:]
        _cast_weight_once(w32_ref, w_ref, ROWS_PER_UNIT)
        y_of = lambda prompt: _dot((mp_ref if prompt else ms_ref)[...], w_ref[...]) + b_ref[...]
    else:
        y_ref = refs.pop(0)
        y_of = lambda prompt: y_ref[...]
    x_ref, mpg_ref, mpn_ref, gs_ref, shs_ref, scs_ref, lg_ref, lb_ref = refs[:8]
    if route:
        wrt_ref, xo_ref, ho_ref, hf_ref, ri_ref, rw_ref = refs[8:]
        route_refs = (wrt_ref, hf_ref, ri_ref, rw_ref)
    else:
        xo_ref, ho_ref = refs[8:]
        route_refs = None
    _finish_dispatch(pl.program_id(0), x_ref[...], y_of, pg, psh, psc, mpg_ref, mpn_ref,
                     gs_ref, shs_ref, scs_ref, lg_ref, lb_ref, xo_ref, ho_ref, route_refs)


def _epilogue_specs(layer, next_layer, pg, psh, psc):
    row = pl.BlockSpec((1, D_MODEL), lambda i, *_: (0, 0))
    modp = lambda l: pl.BlockSpec((None, None, 6, D_MODEL), lambda i, *_: (l, _prompt_batch_of_unit(i), 0, 0))
    mods = lambda l, p: pl.BlockSpec((None, None, ROWS_PER_UNIT, D_MODEL), lambda i, *_: (l, p, 0, 0))
    return [modp(layer), modp(next_layer), mods(layer, pg), mods(next_layer, psh), mods(next_layer, psc), row, row]


def _epilogue(x, y, modp, mods, ln_g, ln_b, layer, next_layer, pg, psh, psc, w_router_t=None):
    route = w_router_t is not None
    project = isinstance(y, tuple)
    unit = lambda w: pl.BlockSpec((ROWS_PER_UNIT, w), lambda i: (i, 0))
    if project:
        mix_p, mix_s, w_out, b_out = y
        in_specs = [pl.BlockSpec((ROWS_PER_UNIT, D_MODEL), lambda i: (jnp.minimum(i, N_PROMPT_UNITS - 1), 0)),
                    pl.BlockSpec((M_SAMPLE, D_MODEL), lambda i: (0, 0)),
                    _resident((D_MODEL, D_MODEL)),
                    pl.BlockSpec((1, D_MODEL), lambda i: (0, 0))]
        args = [mix_p, mix_s, w_out, b_out.reshape(1, D_MODEL)]
        scratch = [pltpu.VMEM((D_MODEL, D_MODEL), BF16)]
    else:
        in_specs = [unit(D_MODEL)]
        args = [y]
        scratch = []
    in_specs += [unit(D_MODEL)] + _epilogue_specs(layer, next_layer, pg, psh, psc)
    args += [x, modp, modp, mods, mods, mods, ln_g.reshape(1, D_MODEL), ln_b.reshape(1, D_MODEL)]
    out_specs = [unit(D_MODEL), unit(D_MODEL)]
    out_shape = [jax.ShapeDtypeStruct((M_ROWS, D_MODEL), F32), jax.ShapeDtypeStruct((M_ROWS, D_MODEL), BF16)]
    if route:
        in_specs.append(pl.BlockSpec((N_EXPERTS, D_MODEL), lambda i: (0, 0)))
        args.append(w_router_t)
        out_specs += [unit(D_MODEL), unit(TOP_K), unit(TOP_K)]
        out_shape += [jax.ShapeDtypeStruct((M_ROWS, D_MODEL), F32),
                      jax.ShapeDtypeStruct((M_ROWS, TOP_K), jnp.int32),
                      jax.ShapeDtypeStruct((M_ROWS, TOP_K), F32)]
    return pl.pallas_call(
        functools.partial(_epilogue_body, pg=pg, psh=psh, psc=psc, route=route, project=project),
        grid=(N_UNITS,),
        in_specs=in_specs,
        out_specs=out_specs,
        out_shape=out_shape,
        scratch_shapes=scratch,
        compiler_params=_params("arbitrary"),
        name=("outproj_" if project else "") + ("epilogue_route" if route else "epilogue"),
    )(*args)


def _swiglu_rows(x, wg, wu, wd):
    g = _dot(x, wg)
    u = _dot(x, wu)
    return _dot((_silu(g) * u).astype(BF16), wd)


def _ffn_body(h_ref, wg_ref, wu_ref, wd_ref, o_ref):
    i = pl.program_id(0)
    j = pl.program_id(1)
    upt = FFN_ROW_TILE // ROWS_PER_UNIT
    full = (i + 1) * upt <= N_UNITS

    def run(n_units):
        @pl.when(j == 0)
        def _():
            o_ref[:n_units * ROWS_PER_UNIT, :] = jnp.zeros((n_units * ROWS_PER_UNIT, D_MODEL), F32)

        for r in range(n_units):
            rows = slice(r * ROWS_PER_UNIT, (r + 1) * ROWS_PER_UNIT)
            o_ref[rows, :] += _swiglu_rows(h_ref[rows, :], wg_ref[...], wu_ref[...], wd_ref[...])

    pl.when(full)(functools.partial(run, upt))
    pl.when(jnp.logical_not(full))(functools.partial(run, N_UNITS % upt))


def _ffn(h, wg, wu, wd):
    nj = D_FF_PAD // FFN_TILE
    return pl.pallas_call(
        _ffn_body,
        grid=(pl.cdiv(M_ROWS, FFN_ROW_TILE), nj),
        in_specs=[pl.BlockSpec((FFN_ROW_TILE, D_MODEL), lambda i, j: (i, 0)),
                  pl.BlockSpec((D_MODEL, FFN_TILE), lambda i, j: (0, j)),
                  pl.BlockSpec((D_MODEL, FFN_TILE), lambda i, j: (0, j)),
                  pl.BlockSpec((FFN_TILE, D_MODEL), lambda i, j: (j, 0))],
        out_specs=pl.BlockSpec((FFN_ROW_TILE, D_MODEL), lambda i, j: (i, 0)),
        out_shape=jax.ShapeDtypeStruct((M_ROWS, D_MODEL), F32),
        compiler_params=_params("arbitrary", "arbitrary"),
        name="ffn_dense",
    )(h, wg, wu, wd)


def _row_copy(src_hbm, row, dst, dst_row, sem):
    return pltpu.make_async_copy(src_hbm.at[pl.ds(row, 1), :], dst.at[pl.ds(dst_row, 1), :], sem)


ROW_DMA_UNROLL = 8


def _moe_gather_body(src_ref, nu_ref, h_hbm, o_ref, buf, sem):
    i = pl.program_id(0)
    total = nu_ref[0]
    slot = i & 1

    def issue(unit, s):
        base = unit * ROWS_PER_UNIT

        def start(r, c):
            _row_copy(h_hbm, src_ref[base + r], buf.at[s], r, sem.at[s]).start()
            return c

        lax.fori_loop(0, ROWS_PER_UNIT, start, 0, unroll=ROW_DMA_UNROLL)

    @pl.when((i == 0) & (total > 0))
    def _():
        issue(0, 0)

    @pl.when(i + 1 < total)
    def _():
        issue(i + 1, 1 - slot)

    @pl.when(i < total)
    def _():
        def wait(r, c):
            _row_copy(h_hbm, 0, buf.at[slot], r, sem.at[slot]).wait()
            return c

        lax.fori_loop(0, ROWS_PER_UNIT, wait, 0, unroll=ROW_DMA_UNROLL)
        o_ref[...] = buf[slot].astype(BF16)

    @pl.when(i >= total)
    def _():
        o_ref[...] = jnp.zeros_like(o_ref)


def _moe_gather(src_tok, total_units, h_f32):
    grid_spec = pltpu.PrefetchScalarGridSpec(
        num_scalar_prefetch=2,
        grid=(MOE_UNITS,),
        in_specs=[pl.BlockSpec(memory_space=pl.ANY)],
        out_specs=pl.BlockSpec((ROWS_PER_UNIT, D_MODEL), lambda i, s, v: (i, 0)),
        scratch_shapes=[pltpu.VMEM((2, ROWS_PER_UNIT, D_MODEL), F32), pltpu.SemaphoreType.DMA((2,))],
    )
    return pl.pallas_call(
        _moe_gather_body,
        grid_spec=grid_spec,
        out_shape=jax.ShapeDtypeStruct((MOE_ROWS, D_MODEL), BF16),
        compiler_params=_params("arbitrary"),
        name="moe_gather",
    )(src_tok, total_units, h_f32)


def _unit_copy(src, src_unit, dst, dst_unit, sem):
    rows = lambda u: pl.ds(pl.multiple_of(u * ROWS_PER_UNIT, ROWS_PER_UNIT), ROWS_PER_UNIT)
    return pltpu.make_async_copy(src.at[rows(src_unit), :], dst.at[rows(dst_unit), :], sem)


def _moe_gmm_body(te_ref, tu_ref, tn_ref, nu_ref, xs_hbm, wg_ref, wu_ref, wd_ref, ys_hbm,
                  xsc, acc, wgu_b, wd_b, sem_in, sem_out):
    t = pl.program_id(0)
    j = pl.program_id(1)
    nj = pl.num_programs(1)
    n = tn_ref[t]
    u0 = tu_ref[t]
    tf = MOE_FF_TILE

    @pl.when((j == 0) & (n > 0))
    def _():
        def start(u, c):
            _unit_copy(xs_hbm, u0 + u, xsc, u, sem_in).start()
            return c

        def zero(u, c):
            acc[pl.ds(pl.multiple_of(u * ROWS_PER_UNIT, ROWS_PER_UNIT), ROWS_PER_UNIT), :] = jnp.zeros(
                (ROWS_PER_UNIT, D_MODEL), F32)
            return c

        def wait(u, c):
            _unit_copy(xs_hbm, u0 + u, xsc, u, sem_in).wait()
            return c

        lax.fori_loop(0, n, start, 0)
        lax.fori_loop(0, n, zero, 0)
        lax.fori_loop(0, n, wait, 0)

    @pl.when((j == 0) & (t == 0))
    def _():
        def start(u, c):
            _unit_copy(acc, 0, ys_hbm, u, sem_out).start()
            return c

        def wait(u, c):
            _unit_copy(acc, 0, ys_hbm, u, sem_out).wait()
            return c

        lax.fori_loop(nu_ref[0], MOE_UNITS, start, 0)
        lax.fori_loop(nu_ref[0], MOE_UNITS, wait, 0)

    @pl.when(n > 0)
    def _():
        wgu_b[:, :tf] = wg_ref[...].astype(BF16)
        wgu_b[:, tf:] = wu_ref[...].astype(BF16)
        wd_b[...] = wd_ref[...].astype(BF16)

        def unit_rows(r0):
            rs = pl.ds(r0, ROWS_PER_UNIT)
            gu = _dot(xsc[rs, :], wgu_b[...])
            a = (_silu(gu[:, :tf]) * gu[:, tf:]).astype(BF16)
            acc[rs, :] += _dot(a, wd_b[...])

        def pair(p, c):
            r0 = pl.multiple_of(p * (2 * ROWS_PER_UNIT), 2 * ROWS_PER_UNIT)
            unit_rows(r0)
            unit_rows(r0 + ROWS_PER_UNIT)
            return c

        lax.fori_loop(0, lax.shift_right_logical(n, 1), pair, 0)

        @pl.when((n & 1) == 1)
        def _():
            unit_rows(pl.multiple_of((n - 1) * ROWS_PER_UNIT, ROWS_PER_UNIT))

    @pl.when((j == nj - 1) & (n > 0))
    def _():
        def start(u, c):
            _unit_copy(acc, u, ys_hbm, u0 + u, sem_out).start()
            return c

        def wait(u, c):
            _unit_copy(acc, u, ys_hbm, u0 + u, sem_out).wait()
            return c

        lax.fori_loop(0, n, start, 0)
        lax.fori_loop(0, n, wait, 0)


def _moe_gmm(tile_expert, tile_unit0, tile_nunits, total_units, xs, w_gu, w_down, lm):
    nj = D_FF_EXPERT // MOE_FF_TILE
    jj = lambda t, j, tn: jnp.where(tn[t] > 0, j, nj - 1)
    tile_rows = MOE_TILE_UNITS * ROWS_PER_UNIT
    grid_spec = pltpu.PrefetchScalarGridSpec(
        num_scalar_prefetch=4,
        grid=(MOE_TILES, nj),
        in_specs=[pl.BlockSpec(memory_space=pl.ANY),
                  pl.BlockSpec((None, None, D_MODEL, MOE_FF_TILE),
                               lambda t, j, te, tu, tn, nu: (lm, te[t], 0, jj(t, j, tn))),
                  pl.BlockSpec((None, None, D_MODEL, MOE_FF_TILE),
                               lambda t, j, te, tu, tn, nu: (lm, te[t], 0, nj + jj(t, j, tn))),
                  pl.BlockSpec((None, None, MOE_FF_TILE, D_MODEL),
                               lambda t, j, te, tu, tn, nu: (lm, te[t], jj(t, j, tn), 0))],
        out_specs=pl.BlockSpec(memory_space=pl.ANY),
        scratch_shapes=[pltpu.VMEM((tile_rows, D_MODEL), BF16), pltpu.VMEM((tile_rows, D_MODEL), F32),
                        pltpu.VMEM((D_MODEL, 2 * MOE_FF_TILE), BF16), pltpu.VMEM((MOE_FF_TILE, D_MODEL), BF16),
                        pltpu.SemaphoreType.DMA(()), pltpu.SemaphoreType.DMA(())],
    )
    return pl.pallas_call(
        _moe_gmm_body,
        grid_spec=grid_spec,
        out_shape=jax.ShapeDtypeStruct((MOE_ROWS, D_MODEL), F32),
        compiler_params=_params("arbitrary", "arbitrary"),
        name="moe_gmm",
    )(tile_expert, tile_unit0, tile_nunits, total_units, xs, w_gu, w_gu, w_down)


def _moe_combine_body(dest_ref, ys_hbm, rw_ref, x_ref, mpg_ref, mpn_ref, gs_ref, shs_ref, scs_ref, lg_ref, lb_ref,
                      xo_ref, ho_ref, buf, sem, *, pg, psh, psc):
    i = pl.program_id(0)
    slot = i & 1

    def issue(unit, s):
        base = unit * (ROWS_PER_UNIT * TOP_K)

        def start(r, c):
            for k in range(TOP_K):
                _row_copy(ys_hbm, dest_ref[base + TOP_K * r + k], buf.at[s, k], r, sem.at[s]).start()
            return c

        lax.fori_loop(0, ROWS_PER_UNIT, start, 0, unroll=ROW_DMA_UNROLL // TOP_K)

    @pl.when(i == 0)
    def _():
        issue(0, 0)

    @pl.when(i + 1 < pl.num_programs(0))
    def _():
        issue(i + 1, 1 - slot)

    def wait(r, c):
        for k in range(TOP_K):
            _row_copy(ys_hbm, 0, buf.at[slot, k], r, sem.at[slot]).wait()
        return c

    lax.fori_loop(0, ROWS_PER_UNIT, wait, 0, unroll=ROW_DMA_UNROLL // TOP_K)
    y = rw_ref[:, 0:1] * buf[slot, 0] + rw_ref[:, 1:2] * buf[slot, 1]
    _finish_dispatch(i, x_ref[...], lambda prompt: y, pg, psh, psc, mpg_ref, mpn_ref, gs_ref, shs_ref, scs_ref,
                     lg_ref, lb_ref, xo_ref, ho_ref, None)


def _moe_combine(dest, ys, route_w, x, modp, mods, ln_g, ln_b, layer, next_layer, pg, psh, psc):
    unit = lambda w: pl.BlockSpec((ROWS_PER_UNIT, w), lambda i, d: (i, 0))
    grid_spec = pltpu.PrefetchScalarGridSpec(
        num_scalar_prefetch=1,
        grid=(N_UNITS,),
        in_specs=[pl.BlockSpec(memory_space=pl.ANY), unit(TOP_K), unit(D_MODEL)]
        + _epilogue_specs(layer, next_layer, pg, psh, psc),
        out_specs=[unit(D_MODEL), unit(D_MODEL)],
        scratch_shapes=[pltpu.VMEM((2, TOP_K, ROWS_PER_UNIT, D_MODEL), F32), pltpu.SemaphoreType.DMA((2,))],
    )
    return pl.pallas_call(
        functools.partial(_moe_combine_body, pg=pg, psh=psh, psc=psc),
        grid_spec=grid_spec,
        out_shape=[jax.ShapeDtypeStruct((M_ROWS, D_MODEL), F32), jax.ShapeDtypeStruct((M_ROWS, D_MODEL), BF16)],
        compiler_params=_params("arbitrary"),
        name="moe_combine",
    )(dest, ys, route_w, x, modp, modp, mods, mods, mods, ln_g.reshape(1, D_MODEL), ln_b.reshape(1, D_MODEL))


def _moe_plan(route_i):
    i32 = jnp.int32
    e_flat = route_i.reshape(-1)
    onehot = (e_flat[:, None] == jnp.arange(N_EXPERTS, dtype=i32)[None, :]).astype(i32)
    csum = jnp.cumsum(onehot, axis=0)
    rank = jnp.sum(csum * onehot, axis=1) - 1
    counts = csum[-1]
    units_e = (counts + ROWS_PER_UNIT - 1) // ROWS_PER_UNIT
    unit_end = jnp.cumsum(units_e)
    unit_start = unit_end - units_e
    dest = (unit_start[e_flat] * ROWS_PER_UNIT + rank).astype(i32)
    pair_tok = jnp.arange(N_PAIRS, dtype=i32) // TOP_K
    src_tok = jnp.zeros((MOE_ROWS,), i32).at[dest].set(pair_tok)
    total_units = unit_end[-1:].astype(i32)
    tiles_e = (units_e + MOE_TILE_UNITS - 1) // MOE_TILE_UNITS
    tile_end = jnp.cumsum(tiles_e)
    tile_start = tile_end - tiles_e
    n_tiles = tile_end[-1]
    t_ids = jnp.arange(MOE_TILES, dtype=i32)
    t_eff = jnp.minimum(t_ids, n_tiles - 1)
    tile_expert = jnp.sum((t_eff[:, None] >= tile_end[None, :]).astype(i32), axis=1)
    k = t_eff - tile_start[tile_expert]
    tile_unit0 = unit_start[tile_expert] + k * MOE_TILE_UNITS
    units_left = units_e[tile_expert] - k * MOE_TILE_UNITS
    tile_nunits = jnp.where(t_ids < n_tiles, jnp.minimum(units_left, MOE_TILE_UNITS), 0)
    return dest, src_tok, total_units, tile_expert.astype(i32), tile_unit0.astype(i32), tile_nunits.astype(i32)


def kernel(x_prompt, x_sample, cache_win_k, cache_win_v, c_prompt, c_sample, w_ada, b_ada, w_in, b_in, v_norm_g, v_norm_b, w_spatial, b_spatial, attn_sinks, w_out, b_out, ln1_g, ln1_b, ln2_g, ln2_b, w_ffn_gu, w_ffn_down, w_router, w_exp_gu, w_exp_down):
    c_rows = jnp.concatenate([c_prompt, c_sample, jnp.zeros((ADA_ROWS - BATCH - DEC_BATCH, D_MODEL), F32)], axis=0)
    mod = _adaln(c_rows, w_ada, b_ada)
    modp = mod[:, :BATCH].reshape(DEPTH, BATCH, 6, D_MODEL)
    mods = jnp.repeat(mod[:, BATCH:BATCH + DEC_BATCH].reshape(DEPTH, DEC_BATCH, 6, D_MODEL), DEC_SEQ, axis=1)
    mods = mods.transpose(0, 2, 1, 3)

    x = jnp.concatenate([x_prompt.reshape(M_PROMPT, D_MODEL), x_sample.reshape(M_SAMPLE, D_MODEL)], axis=0)
    h = _modulate0(x, modp, mods)

    rows = GQA * DEC_SEQ
    kp_list, vp_list, ks_list, vs_list, cv_list = [], [], [], [], []
    for l in range(DEPTH):
        nl = min(l + 1, DEPTH - 1)
        u, v, q, kv = _proj(h, w_in[l], b_in[l], v_norm_g[l], v_norm_b[l])
        b_s_t = b_spatial[l].T
        mix_p = _mix_prompt(attn_sinks[l].reshape(-1), u, v, q, kv, w_spatial[l], b_s_t)

        u_s = u[M_PROMPT:].reshape(DEC_BATCH, DEC_SEQ, A_WIDTH)
        v_s = v[M_PROMPT:].reshape(DEC_BATCH, DEC_SEQ, A_WIDTH)
        kv_s = kv[M_PROMPT:].reshape(DEC_BATCH, DEC_SEQ, 2 * KV_WIDTH)
        q_s = q[M_PROMPT:].reshape(DEC_BATCH, DEC_SEQ, N_KV_HEADS, GQA, HEAD_DIM)
        q_s = q_s.transpose(0, 2, 3, 1, 4).reshape(DEC_BATCH, N_KV_HEADS, rows, HEAD_DIM)
        ck = cache_win_k[l].reshape(DEC_BATCH, WINDOW, KV_WIDTH)
        cv = cache_win_v[l].reshape(DEC_BATCH, WINDOW, KV_WIDTH)
        sink_rows = jnp.repeat(attn_sinks[l], DEC_SEQ, axis=1).reshape(N_KV_HEADS, rows, 1)
        a_s, o_s = _mix_sample(u_s, v_s, q_s, kv_s, ck, cv, w_spatial[l], b_s_t, sink_rows)
        o_s = o_s.reshape(DEC_BATCH, N_KV_HEADS, GQA, DEC_SEQ, HEAD_DIM).transpose(0, 3, 1, 2, 4)
        mix_s = jnp.concatenate([a_s.reshape(M_SAMPLE, A_WIDTH), o_s.reshape(M_SAMPLE, B_WIDTH)], axis=1)

        y = (mix_p, mix_s, w_out[l], b_out[l])
        if l % 2 == 0:
            x, h = _epilogue(x, y, modp, mods, ln1_g[l], ln1_b[l], l, l, P_GATE1, P_SHIFT2, P_SCALE2)
            ld = l // 2
            pad = ((0, 0), (0, D_FF_PAD - D_FF))
            wg = jnp.pad(w_ffn_gu[ld][:, :D_FF].astype(BF16), pad)
            wu = jnp.pad(w_ffn_gu[ld][:, D_FF:].astype(BF16), pad)
            wd = jnp.pad(w_ffn_down[ld].astype(BF16), ((0, D_FF_PAD - D_FF), (0, 0)))
            f = _ffn(h, wg, wu, wd)
            x, h = _epilogue(x, f, modp, mods, ln2_g[l], ln2_b[l], l, nl, P_GATE2, P_SHIFT1, P_SCALE1)
        else:
            lm = l // 2
            x, h, h_f32, route_i, route_w = _epilogue(x, y, modp, mods, ln1_g[l], ln1_b[l], l, l,
                                                      P_GATE1, P_SHIFT2, P_SCALE2, w_router_t=w_router[lm].T)
            dest, src_tok, total_units, tile_expert, tile_unit0, tile_nunits = _moe_plan(route_i)
            xs = _moe_gather(src_tok, total_units, h_f32)
            ys = _moe_gmm(tile_expert, tile_unit0, tile_nunits, total_units, xs, w_exp_gu, w_exp_down, lm)
            x, h = _moe_combine(dest, ys, route_w, x, modp, mods, ln2_g[l], ln2_b[l], l, nl,
                                P_GATE2, P_SHIFT1, P_SCALE1)

        kv_p = kv[:M_PROMPT].reshape(BATCH, SEQ, 2 * KV_WIDTH)[:, SEQ - WINDOW:]
        kp_list.append(kv_p[..., :KV_WIDTH].reshape(BATCH, WINDOW, N_KV_HEADS, HEAD_DIM))
        vp_list.append(kv_p[..., KV_WIDTH:].reshape(BATCH, WINDOW, N_KV_HEADS, HEAD_DIM))
        k_new = kv_s[..., :KV_WIDTH].reshape(DEC_BATCH, DEC_SEQ, N_KV_HEADS, HEAD_DIM)
        v_new = kv_s[..., KV_WIDTH:].reshape(DEC_BATCH, DEC_SEQ, N_KV_HEADS, HEAD_DIM)
        ks_list.append(jnp.concatenate([cache_win_k[l][:, DEC_SEQ:], k_new], axis=1))
        vs_list.append(jnp.concatenate([cache_win_v[l][:, DEC_SEQ:], v_new], axis=1))
        cv_list.append(v_s.astype(F32))

    y_prompt = x[:M_PROMPT].reshape(BATCH, SEQ, D_MODEL)
    y_sample = x[M_PROMPT:].reshape(DEC_BATCH, DEC_SEQ, D_MODEL)
    return (y_prompt, y_sample, jnp.stack(kp_list), jnp.stack(vp_list), jnp.stack(ks_list), jnp.stack(vs_list),
            jnp.stack(cv_list))
```

```python
import functools

import jax
import jax.numpy as jnp
from jax import lax
from jax.experimental import pallas as pl
from jax.experimental.pallas import tpu as pltpu

F32 = jnp.float32
BF16 = jnp.bfloat16

D_MODEL = 2048
BATCH = 4
SEQ = 2048
DEPTH = 4
DEC_BATCH = 32
DEC_SEQ = 8
CHUNK = 128
A_WIDTH = D_MODEL // 2
A_GROUPS = 4
A_GROUP_DIM = A_WIDTH // A_GROUPS
HEAD_DIM = 64
B_WIDTH = D_MODEL - A_WIDTH
N_HEADS = B_WIDTH // HEAD_DIM
N_KV_HEADS = 2
GQA = N_HEADS // N_KV_HEADS
KV_WIDTH = N_KV_HEADS * HEAD_DIM
WINDOW = 128
IN_COLS = 2 * A_WIDTH + B_WIDTH + 2 * KV_WIDTH
D_FF = 5504
N_EXPERTS = 8
TOP_K = 2
D_FF_EXPERT = 7168
ALPHA = (2.0 * DEPTH) ** 0.25
LN_EPS = 1e-5
NEG_INF = -1e30

M_PROMPT = BATCH * SEQ
M_SAMPLE = DEC_BATCH * DEC_SEQ
M_ROWS = M_PROMPT + M_SAMPLE
ROWS_PER_UNIT = 256
N_PROMPT_UNITS = M_PROMPT // ROWS_PER_UNIT
N_UNITS = M_ROWS // ROWS_PER_UNIT
UNITS_PER_SEQ = SEQ // ROWS_PER_UNIT
ADA_ROWS = 40
P_SHIFT1, P_SCALE1, P_GATE1, P_SHIFT2, P_SCALE2, P_GATE2 = range(6)

D_FF_PAD = 5632
FFN_TILE = 512
FFN_ROW_TILE = 4 * ROWS_PER_UNIT
MOE_FF_TILE = 256
N_PAIRS = M_ROWS * TOP_K
MOE_UNITS = N_PAIRS // ROWS_PER_UNIT + N_EXPERTS
MOE_ROWS = MOE_UNITS * ROWS_PER_UNIT
MOE_TILE_UNITS = 9
MOE_TILES =MOE_UNITS // MOE_TILE_UNITS + N_EXPERTS

VMEM_LIMIT = 56 * 1024 * 1024


def _params(*sem):
    return pltpu.CompilerParams(dimension_semantics=sem, vmem_limit_bytes=VMEM_LIMIT)


def _dot(a, b):
    return jnp.dot(a, b, preferred_element_type=F32)


def _layer_norm(z, g, b):
    mu = jnp.mean(z, axis=-1, keepdims=True)
    zc = z - mu
    var = jnp.mean(zc * zc, axis=-1, keepdims=True)
    return zc * lax.rsqrt(var + LN_EPS) * g + b


def _gelu_tanh(x):
    return x * (0.5 * (1.0 + jnp.tanh(0.7978845608028654 * (x + 0.044715 * (x * x * x)))))


def _silu(x):
    return x * jax.nn.sigmoid(x)


def _prompt_batch_of_unit(i):
    return jnp.minimum(i // UNITS_PER_SEQ, BATCH - 1)


def _adaln_body(c_ref, w_ref, b_ref, o_ref):
    s = _silu(c_ref[...]).astype(BF16)
    o_ref[...] = _dot(s, w_ref[...].astype(BF16)) + b_ref[...]


def _adaln(c_rows, w_ada, b_ada):
    tn = 1024
    return pl.pallas_call(
        _adaln_body,
        grid=(DEPTH, 6 * D_MODEL // tn),
        in_specs=[pl.BlockSpec((ADA_ROWS, D_MODEL), lambda l, n: (0, 0)),
                  pl.BlockSpec((None, D_MODEL, tn), lambda l, n: (l, 0, n)),
                  pl.BlockSpec((None, 1, tn), lambda l, n: (l, 0, n))],
        out_specs=pl.BlockSpec((None, ADA_ROWS, tn), lambda l, n: (l, 0, n)),
        out_shape=jax.ShapeDtypeStruct((DEPTH, ADA_ROWS, 6 * D_MODEL), F32),
        compiler_params=_params("arbitrary", "arbitrary"),
        name="adaln",
    )(c_rows, w_ada, b_ada.reshape(DEPTH, 1, 6 * D_MODEL))


def _mod0_body(x_ref, mp_ref, ss_ref, cs_ref, h_ref):
    i = pl.program_id(0)
    x = x_ref[...]

    @pl.when(i < N_PROMPT_UNITS)
    def _():
        h_ref[...] = (x * (1 + mp_ref[P_SCALE1:P_SCALE1 + 1, :]) + mp_ref[P_SHIFT1:P_SHIFT1 + 1, :]).astype(BF16)

    @pl.when(i >= N_PROMPT_UNITS)
    def _():
        h_ref[...] = (x * (1 + cs_ref[...]) + ss_ref[...]).astype(BF16)


def _unit_spec(width):
    return pl.BlockSpec((ROWS_PER_UNIT, width), lambda i: (i, 0))


def _modp_spec(layer):
    return pl.BlockSpec((None, None, 6, D_MODEL), lambda i: (layer, _prompt_batch_of_unit(i), 0, 0))


def _mods_spec(layer, p):
    return pl.BlockSpec((None, ROWS_PER_UNIT, D_MODEL), lambda i, *_: (layer, 0, p))


def _modulate0(x, modp, mods):
    return pl.pallas_call(
        _mod0_body,
        grid=(N_UNITS,),
        in_specs=[_unit_spec(D_MODEL), _modp_spec(0), _mods_spec(0, P_SHIFT1), _mods_spec(0, P_SCALE1)],
        out_specs=_unit_spec(D_MODEL),
        out_shape=jax.ShapeDtypeStruct((M_ROWS, D_MODEL), BF16),
        compiler_params=_params("arbitrary"),
        name="modulate0",
    )(x, modp, mods, mods)


def _cast_weight_once(w32_ref, w_ref, chunk):
    @pl.when(pl.program_id(0) == 0)
    def _():
        for c in range(0, w32_ref.shape[1], chunk):
            w_ref[:, c:c + chunk] = w32_ref[:, c:c + chunk].astype(BF16)


def _resident(layer, shape):
    return pl.BlockSpec((None,) + shape, lambda i: (layer,) + (0,) * len(shape), pipeline_mode=pl.Buffered(1))


def _proj_body(h_ref, w32_ref, b_ref, gv_ref, bv_ref, u_ref, v_ref, q_ref, kv_ref, w_ref):
    _cast_weight_once(w32_ref, w_ref, A_GROUP_DIM)
    h = h_ref[...]
    gd = A_GROUP_DIM
    for c in range(A_GROUPS):
        sl = slice(c * gd, (c + 1) * gd)
        z = _dot(h, w_ref[:, sl]) + b_ref[:, sl]
        u_ref[:, sl] = _gelu_tanh(z).astype(BF16)
    for g in range(A_GROUPS):
        sl = slice(A_WIDTH + g * gd, A_WIDTH + (g + 1) * gd)
        z = _gelu_tanh(_dot(h, w_ref[:, sl]) + b_ref[:, sl])
        vn = _layer_norm(z, gv_ref[g:g + 1, :], bv_ref[g:g + 1, :])
        v_ref[:, g * gd:(g + 1) * gd] = vn.astype(BF16)
    o1 = 2 * A_WIDTH
    o2 = o1 + B_WIDTH
    z = _dot(h, w_ref[:, o1:o2]) + b_ref[:, o1:o2]
    q_ref[...] = (z * (HEAD_DIM ** -0.5)).astype(BF16)
    kv_ref[...] = _dot(h, w_ref[:, o2:]) + b_ref[:, o2:]


def _proj(h, w_all, layer, b, gv, bv):
    const = lambda shape: pl.BlockSpec(shape, lambda i: (0,) * len(shape))
    return pl.pallas_call(
        _proj_body,
        grid=(N_UNITS,),
        in_specs=[_unit_spec(D_MODEL), _resident(layer, (D_MODEL, IN_COLS)), const((1, IN_COLS)),
                  const((A_GROUPS, A_GROUP_DIM)), const((A_GROUPS, A_GROUP_DIM))],
        out_specs=[_unit_spec(A_WIDTH), _unit_spec(A_WIDTH), _unit_spec(B_WIDTH), _unit_spec(2 * KV_WIDTH)],
        out_shape=[jax.ShapeDtypeStruct((M_ROWS, A_WIDTH), BF16), jax.ShapeDtypeStruct((M_ROWS, A_WIDTH), BF16),
                   jax.ShapeDtypeStruct((M_ROWS, B_WIDTH), BF16), jax.ShapeDtypeStruct((M_ROWS, 2 * KV_WIDTH), F32)],
        scratch_shapes=[pltpu.VMEM((D_MODEL, IN_COLS), BF16)],
        compiler_params=_params("arbitrary"),
        name="proj",
    )(h, w_all, b.reshape(1, IN_COLS), gv, bv)


def _mix_prompt_body(sink_ref, u_ref, v_ref, q_ref, kvc_ref, kvp_ref, ws_ref, bs_ref, o_ref):
    blk = pl.program_id(1)
    gd = A_GROUP_DIM
    tr = lax.broadcasted_iota(jnp.int32, (CHUNK, CHUNK), 0)
    tc = lax.broadcasted_iota(jnp.int32, (CHUNK, CHUNK), 1)
    for g in range(A_GROUPS):
        sl = slice(g * gd, (g + 1) * gd)
        w = jnp.where(tr >= tc, ws_ref[g], 0.0).astype(BF16)
        s = _dot(w, v_ref[:, sl]) + bs_ref[:, g:g + 1]
        o_ref[:, sl] = (u_ref[:, sl].astype(F32) * s).astype(BF16)

    kvc = kvc_ref[...]
    kvp = kvp_ref[...]
    row = lax.broadcasted_iota(jnp.int32, (WINDOW, 2 * WINDOW), 0)
    col = lax.broadcasted_iota(jnp.int32, (WINDOW, 2 * WINDOW), 1)
    dlt = row + WINDOW - col
    mask = (dlt >= 0) & (dlt <= WINDOW) & ((col >= WINDOW) | (blk > 0))
    hd = HEAD_DIM
    for j in range(N_KV_HEADS):
        ks = slice(j * hd, (j + 1) * hd)
        vs = slice(KV_WIDTH + j * hd, KV_WIDTH + (j + 1) * hd)
        kk = jnp.concatenate([kvp[:, ks], kvc[:, ks]], axis=0).astype(BF16)
        vv = jnp.concatenate([kvp[:, vs], kvc[:, vs]], axis=0).astype(BF16)
        for hh in range(GQA):
            h = j * GQA + hh
            qh = q_ref[:, h * hd:(h + 1) * hd]
            s = lax.dot_general(qh, kk, (((1,), (1,)), ((), ())), preferred_element_type=F32)
            s = jnp.where(mask, s, NEG_INF)
            sk = sink_ref[h]
            m = jnp.maximum(jnp.max(s, axis=-1, keepdims=True), sk)
            p = jnp.exp(s - m)
            den = jnp.sum(p, axis=-1, keepdims=True) + jnp.exp(sk - m)
            o = _dot(p.astype(BF16), vv) / den
            o_ref[:, A_WIDTH + h * hd:A_WIDTH + (h + 1) * hd] = o.astype(BF16)


def _mix_prompt(sinks, u, v, q, kv, w_s, b_s_t):
    nb = SEQ // WINDOW
    row_blk = lambda w: pl.BlockSpec((WINDOW, w), lambda b, i, s: (b * nb + i, 0))
    grid_spec = pltpu.PrefetchScalarGridSpec(
        num_scalar_prefetch=1,
        grid=(BATCH, nb),
        in_specs=[row_blk(A_WIDTH), row_blk(A_WIDTH), row_blk(B_WIDTH), row_blk(2 * KV_WIDTH),
                  pl.BlockSpec((WINDOW, 2 * KV_WIDTH), lambda b, i, s: (b * nb + jnp.maximum(i - 1, 0), 0)),
                  pl.BlockSpec((A_GROUPS, CHUNK, CHUNK), lambda b, i, s: (0, 0, 0)),
                  pl.BlockSpec((CHUNK, A_GROUPS), lambda b, i, s: (0, 0))],
        out_specs=pl.BlockSpec((WINDOW, D_MODEL), lambda b, i, s: (b * nb + i, 0)),
    )
    return pl.pallas_call(
        _mix_prompt_body,
        grid_spec=grid_spec,
        out_shape=jax.ShapeDtypeStruct((M_PROMPT, D_MODEL), BF16),
        compiler_params=_params("arbitrary", "arbitrary"),
        name="mix_prompt",
    )(sinks, u, v, q, kv, kv, w_s, b_s_t)


SAMPLE_KEYS = 2 * WINDOW


def _mix_sample_body(u_ref, v_ref, q_ref, kn_ref, ck_ref, cv_ref, ws_ref, bs_ref, sk_ref, a_ref, o_ref):
    gd = A_GROUP_DIM
    ds = DEC_SEQ
    v = v_ref[...].astype(F32)
    u = u_ref[...].astype(F32)
    tr = lax.broadcasted_iota(jnp.int32, (ds, ds), 0)
    tc = lax.broadcasted_iota(jnp.int32, (ds, ds), 1)
    for g in range(A_GROUPS):
        sl = slice(g * gd, (g + 1) * gd)
        w = jnp.where(tr >= tc, ws_ref[g][:ds, :ds], 0.0)
        s = jnp.zeros((ds, gd), F32) + bs_ref[:ds, g:g + 1]
        for t in range(ds):
            s = s + w[:, t:t + 1] * v[t:t + 1, sl]
        a_ref[:, sl] = (u[:, sl] * s).astype(BF16)

    kn = kn_ref[...]
    ck = ck_ref[...]
    cv = cv_ref[...]
    hd = HEAD_DIM
    rows = GQA * ds
    qt = lax.broadcasted_iota(jnp.int32, (rows, SAMPLE_KEYS), 0) & (ds - 1)
    col = lax.broadcasted_iota(jnp.int32, (rows, SAMPLE_KEYS), 1)
    mask = ((col < WINDOW) & (col >= qt)) | ((col >= WINDOW) & (col - WINDOW <= qt))
    pad = jnp.zeros((SAMPLE_KEYS - WINDOW - ds, hd), F32)
    for j in range(N_KV_HEADS):
        ks = slice(j * hd, (j + 1) * hd)
        vs = slice(KV_WIDTH + j * hd, KV_WIDTH + (j + 1) * hd)
        kk = jnp.concatenate([ck[:, ks], kn[:, ks], pad], axis=0).astype(BF16)
        vv = jnp.concatenate([cv[:, ks], kn[:, vs], pad], axis=0).astype(BF16)
        s = lax.dot_general(q_ref[j], kk, (((1,), (1,)), ((), ())), preferred_element_type=F32)
        s = jnp.where(mask, s, NEG_INF)
        sk = sk_ref[j]
        m = jnp.maximum(jnp.max(s, axis=-1, keepdims=True), sk)
        p = jnp.exp(s - m)
        den = jnp.sum(p, axis=-1, keepdims=True) + jnp.exp(sk - m)
        o_ref[j] = (_dot(p.astype(BF16), vv) / den).astype(BF16)


def _mix_sample(u_s, v_s, q_s, kv_s, cache_k, cache_v, w_s, b_s_t, sink_rows):
    rows = GQA * DEC_SEQ
    per_b = lambda *shape: pl.BlockSpec((None,) + shape, lambda b: (b,) + (0,) * len(shape))
    const = lambda *shape: pl.BlockSpec(shape, lambda b: (0,) * len(shape))
    return pl.pallas_call(
        _mix_sample_body,
        grid=(DEC_BATCH,),
        in_specs=[per_b(DEC_SEQ, A_WIDTH), per_b(DEC_SEQ, A_WIDTH), per_b(N_KV_HEADS, rows, HEAD_DIM),
                  per_b(DEC_SEQ, 2 * KV_WIDTH), per_b(WINDOW, KV_WIDTH), per_b(WINDOW, KV_WIDTH),
                  const(A_GROUPS, CHUNK, CHUNK), const(CHUNK, A_GROUPS), const(N_KV_HEADS, rows, 1)],
        out_specs=[per_b(DEC_SEQ, A_WIDTH), per_b(N_KV_HEADS, rows, HEAD_DIM)],
        out_shape=[jax.ShapeDtypeStruct((DEC_BATCH, DEC_SEQ, A_WIDTH), BF16),
                   jax.ShapeDtypeStruct((DEC_BATCH, N_KV_HEADS, rows, HEAD_DIM), BF16)],
        compiler_params=_params("arbitrary"),
        name="mix_sample",
    )(u_s, v_s, q_s, kv_s, cache_k, cache_v, w_s, b_s_t, sink_rows)


def _route_top2(h, wrt_ref, ri_ref, rw_ref):
    logits = [jnp.sum(h * wrt_ref[e:e + 1, :], axis=-1, keepdims=True) for e in range(N_EXPERTS)]
    m1 = logits[0]
    i1 = jnp.zeros_like(m1, dtype=jnp.int32)
    for e in range(1, N_EXPERTS):
        gt = logits[e] > m1
        m1 = jnp.where(gt, logits[e], m1)
        i1 = jnp.where(gt, e, i1)
    m2 = jnp.full_like(m1, -jnp.inf)
    i2 = jnp.zeros_like(i1)
    for e in range(N_EXPERTS):
        ok = (i1 != e) & (logits[e] > m2)
        m2 = jnp.where(ok, logits[e], m2)
        i2 = jnp.where(ok, e, i2)
    w1 = 1.0 / (1.0 + jnp.exp(m2 - m1))
    ri_ref[:, 0:1] = i1
    ri_ref[:, 1:2] = i2
    rw_ref[:, 0:1] = w1
    rw_ref[:, 1:2] = 1.0 - w1


def _finish_unit(x, y, gate, shift, scale, lg_ref, lb_ref, xo_ref, ho_ref, route_refs):
    xn = _layer_norm(ALPHA * x + (1 + gate) * y, lg_ref[...], lb_ref[...])
    xo_ref[...] = xn
    h = xn * (1 + scale) + shift
    ho_ref[...] = h.astype(BF16)
    if route_refs is not None:
        wrt_ref, hf_ref, ri_ref, rw_ref = route_refs
        hf_ref[...] = h
        _route_top2(h, wrt_ref, ri_ref, rw_ref)


def _finish_dispatch(i, x, y_of, pg, psh, psc, mpg_ref, mpn_ref, gs_ref, shs_ref, scs_ref,
                     lg_ref, lb_ref, xo_ref, ho_ref, route_refs):
    @pl.when(i < N_PROMPT_UNITS)
    def _():
        _finish_unit(x, y_of(True), mpg_ref[pg:pg + 1, :], mpn_ref[psh:psh + 1, :], mpn_ref[psc:psc + 1, :],
                     lg_ref, lb_ref, xo_ref, ho_ref, route_refs)

    @pl.when(i >= N_PROMPT_UNITS)
    def _():
        _finish_unit(x, y_of(False), gs_ref[...], shs_ref[...], scs_ref[...], lg_ref, lb_ref, xo_ref, ho_ref,
                     route_refs)


def _epilogue_body(*refs, pg, psh, psc, route, project):
    refs = list(refs)
    if project:
        mp_ref, ms_ref, w32_ref, b_ref = refs[:4]
        w_ref = refs.pop()
        refs = refs[4:]
        _cast_weight_once(w32_ref, w_ref, ROWS_PER_UNIT)
        y_of = lambda prompt: _dot((mp_ref if prompt else ms_ref)[...], w_ref[...]) + b_ref[...]
    else:
        y_ref = refs.pop(0)
        y_of = lambda prompt: y_ref[...]
    x_ref, mpg_ref, mpn_ref, gs_ref, shs_ref, scs_ref, lg_ref, lb_ref = refs[:8]
    if route:
        wrt_ref, xo_ref, ho_ref, hf_ref, ri_ref, rw_ref = refs[8:]
        route_refs = (wrt_ref, hf_ref, ri_ref, rw_ref)
    else:
        xo_ref, ho_ref = refs[8:]
        route_refs = None
    _finish_dispatch(pl.program_id(0), x_ref[...], y_of, pg, psh, psc, mpg_ref, mpn_ref,
                     gs_ref, shs_ref, scs_ref, lg_ref, lb_ref, xo_ref, ho_ref, route_refs)


def _epilogue_specs(layer, next_layer, pg, psh, psc):
    row = pl.BlockSpec((1, D_MODEL), lambda i, *_: (0, 0))
    modp = lambda l: pl.BlockSpec((None, None, 6, D_MODEL), lambda i, *_: (l, _prompt_batch_of_unit(i), 0, 0))
    return [modp(layer), modp(next_layer), _mods_spec(layer, pg), _mods_spec(next_layer, psh),
            _mods_spec(next_layer, psc), row, row]


def _epilogue(x, y, modp, mods, ln_g, ln_b, layer, next_layer, pg, psh, psc, w_router_t=None):
    route = w_router_t is not None
    project = isinstance(y, tuple)
    unit = lambda w: pl.BlockSpec((ROWS_PER_UNIT, w), lambda i: (i, 0))
    if project:
        mix_p, mix_s, w_out, b_out = y
        in_specs = [pl.BlockSpec((ROWS_PER_UNIT, D_MODEL), lambda i: (jnp.minimum(i, N_PROMPT_UNITS - 1), 0)),
                    pl.BlockSpec((M_SAMPLE, D_MODEL), lambda i: (0, 0)),
                    _resident(layer, (D_MODEL, D_MODEL)),
                    pl.BlockSpec((1, D_MODEL), lambda i: (0, 0))]
        args = [mix_p, mix_s, w_out, b_out.reshape(1, D_MODEL)]
        scratch = [pltpu.VMEM((D_MODEL, D_MODEL), BF16)]
    else:
        in_specs = [unit(D_MODEL)]
        args = [y]
        scratch = []
    in_specs += [unit(D_MODEL)] + _epilogue_specs(layer, next_layer, pg, psh, psc)
    args += [x, modp, modp, mods, mods, mods, ln_g.reshape(1, D_MODEL), ln_b.reshape(1, D_MODEL)]
    out_specs = [unit(D_MODEL), unit(D_MODEL)]
    out_shape = [jax.ShapeDtypeStruct((M_ROWS, D_MODEL), F32), jax.ShapeDtypeStruct((M_ROWS, D_MODEL), BF16)]
    if route:
        in_specs.append(pl.BlockSpec((N_EXPERTS, D_MODEL), lambda i: (0, 0)))
        args.append(w_router_t)
        out_specs += [unit(D_MODEL), unit(TOP_K), unit(TOP_K)]
        out_shape += [jax.ShapeDtypeStruct((M_ROWS, D_MODEL), F32),
                      jax.ShapeDtypeStruct((M_ROWS, TOP_K), jnp.int32),
                      jax.ShapeDtypeStruct((M_ROWS, TOP_K), F32)]
    return pl.pallas_call(
        functools.partial(_epilogue_body, pg=pg, psh=psh, psc=psc, route=route, project=project),
        grid=(N_UNITS,),
        in_specs=in_specs,
        out_specs=out_specs,
        out_shape=out_shape,
        scratch_shapes=scratch,
        compiler_params=_params("arbitrary"),
        name=("outproj_" if project else "") + ("epilogue_route" if route else "epilogue"),
    )(*args)


def _swiglu_rows(x, wg, wu, wd):
    g = _dot(x, wg)
    u = _dot(x, wu)
    return _dot((_silu(g) * u).astype(BF16), wd)


def _ffn_body(h_ref, wg_ref, wu_ref, wd_ref, o_ref):
    i = pl.program_id(0)
    j = pl.program_id(1)
    upt = FFN_ROW_TILE // ROWS_PER_UNIT
    full = (i + 1) * upt <= N_UNITS

    def run(n_units):
        @pl.when(j == 0)
        def _():
            o_ref[:n_units * ROWS_PER_UNIT, :] = jnp.zeros((n_units * ROWS_PER_UNIT, D_MODEL), F32)

        for r in range(n_units):
            rows = slice(r * ROWS_PER_UNIT, (r + 1) * ROWS_PER_UNIT)
            o_ref[rows, :] += _swiglu_rows(h_ref[rows, :], wg_ref[...], wu_ref[...], wd_ref[...])

    pl.when(full)(functools.partial(run, upt))
    pl.when(jnp.logical_not(full))(functools.partial(run, N_UNITS % upt))


def _ffn(h, wg, wu, wd):
    nj = D_FF_PAD // FFN_TILE
    return pl.pallas_call(
        _ffn_body,
        grid=(pl.cdiv(M_ROWS, FFN_ROW_TILE), nj),
        in_specs=[pl.BlockSpec((FFN_ROW_TILE, D_MODEL), lambda i, j: (i, 0)),
                  pl.BlockSpec((D_MODEL, FFN_TILE), lambda i, j: (0, j)),
                  pl.BlockSpec((D_MODEL, FFN_TILE), lambda i, j: (0, j)),
                  pl.BlockSpec((FFN_TILE, D_MODEL), lambda i, j: (j, 0))],
        out_specs=pl.BlockSpec((FFN_ROW_TILE, D_MODEL), lambda i, j: (i, 0)),
        out_shape=jax.ShapeDtypeStruct((M_ROWS, D_MODEL), F32),
        compiler_params=_params("arbitrary", "arbitrary"),
        name="ffn_dense",
    )(h, wg, wu, wd)


def _row_copy(src_hbm, row, dst, dst_row, sem):
    return pltpu.make_async_copy(src_hbm.at[pl.ds(row, 1), :], dst.at[pl.ds(dst_row, 1), :], sem)


ROW_DMA_UNROLL = 8


def _moe_gather_body(src_ref, nu_ref, h_hbm, o_ref, buf, sem):
    i = pl.program_id(0)
    total = nu_ref[0]
    slot = i & 1

    def issue(unit, s):
        base = unit * ROWS_PER_UNIT

        def start(r, c):
            _row_copy(h_hbm, src_ref[base + r], buf.at[s], r, sem.at[s]).start()
            return c

        lax.fori_loop(0, ROWS_PER_UNIT, start, 0, unroll=ROW_DMA_UNROLL)

    @pl.when((i == 0) & (total > 0))
    def _():
        issue(0, 0)

    @pl.when(i + 1 < total)
    def _():
        issue(i + 1, 1 - slot)

    @pl.when(i < total)
    def _():
        def wait(r, c):
            _row_copy(h_hbm, 0, buf.at[slot], r, sem.at[slot]).wait()
            return c

        lax.fori_loop(0, ROWS_PER_UNIT, wait, 0, unroll=ROW_DMA_UNROLL)
        o_ref[...] = buf[slot].astype(BF16)

    @pl.when(i >= total)
    def _():
        o_ref[...] = jnp.zeros_like(o_ref)


def _moe_gather(src_tok, total_units, h_f32):
    grid_spec = pltpu.PrefetchScalarGridSpec(
        num_scalar_prefetch=2,
        grid=(MOE_UNITS,),
        in_specs=[pl.BlockSpec(memory_space=pl.ANY)],
        out_specs=pl.BlockSpec((ROWS_PER_UNIT, D_MODEL), lambda i, s, v: (i, 0)),
        scratch_shapes=[pltpu.VMEM((2, ROWS_PER_UNIT, D_MODEL), F32), pltpu.SemaphoreType.DMA((2,))],
    )
    return pl.pallas_call(
        _moe_gather_body,
        grid_spec=grid_spec,
        out_shape=jax.ShapeDtypeStruct((MOE_ROWS, D_MODEL), BF16),
        compiler_params=_params("arbitrary"),
        name="moe_gather",
    )(src_tok, total_units, h_f32)


def _unit_copy(src, src_unit, dst, dst_unit, sem):
    rows = lambda u: pl.ds(pl.multiple_of(u * ROWS_PER_UNIT, ROWS_PER_UNIT), ROWS_PER_UNIT)
    return pltpu.make_async_copy(src.at[rows(src_unit), :], dst.at[rows(dst_unit), :], sem)


def _moe_gmm_body(te_ref, tu_ref, tn_ref, nu_ref, xs_hbm, wg_ref, wu_ref, wd_ref, ys_hbm,
                  xsc, acc, wgu_b, wd_b, sem_in, sem_out):
    t = pl.program_id(0)
    j = pl.program_id(1)
    nj = pl.num_programs(1)
    n = tn_ref[t]
    u0 = tu_ref[t]
    tf = MOE_FF_TILE

    @pl.when((j == 0) & (n > 0))
    def _():
        def start(u, c):
            _unit_copy(xs_hbm, u0 + u, xsc, u, sem_in).start()
            return c

        def zero(u, c):
            acc[pl.ds(pl.multiple_of(u * ROWS_PER_UNIT, ROWS_PER_UNIT), ROWS_PER_UNIT), :] = jnp.zeros(
                (ROWS_PER_UNIT, D_MODEL), F32)
            return c

        def wait(u, c):
            _unit_copy(xs_hbm, u0 + u, xsc, u, sem_in).wait()
            return c

        lax.fori_loop(0, n, start, 0)
        lax.fori_loop(0, n, zero, 0)
        lax.fori_loop(0, n, wait, 0)

    @pl.when((j == 0) & (t == 0))
    def _():
        def start(u, c):
            _unit_copy(acc, 0, ys_hbm, u, sem_out).start()
            return c

        def wait(u, c):
            _unit_copy(acc, 0, ys_hbm, u, sem_out).wait()
            return c

        lax.fori_loop(nu_ref[0], MOE_UNITS, start, 0)
        lax.fori_loop(nu_ref[0], MOE_UNITS, wait, 0)

    @pl.when(n > 0)
    def _():
        wgu_b[:, :tf] = wg_ref[...].astype(BF16)
        wgu_b[:, tf:] = wu_ref[...].astype(BF16)
        wd_b[...] = wd_ref[...].astype(BF16)

        def unit_rows(r0):
            rs = pl.ds(r0, ROWS_PER_UNIT)
            gu = _dot(xsc[rs, :], wgu_b[...])
            a = (_silu(gu[:, :tf]) * gu[:, tf:]).astype(BF16)
            acc[rs, :] += _dot(a, wd_b[...])

        def units_from(u_first, k):
            for h in range(k):
                unit_rows(pl.multiple_of((u_first + h) * ROWS_PER_UNIT, ROWS_PER_UNIT))

        for b in reversed(range(MOE_TILE_UNITS.bit_length())):
            size = 1 << b
            done = lax.shift_left(lax.shift_right_logical(n, b + 1), b + 1)
            pl.when((n & size) != 0)(functools.partial(units_from, done, size))

    @pl.when((j == nj - 1) & (n > 0))
    def _():
        def start(u, c):
            _unit_copy(acc, u, ys_hbm, u0 + u, sem_out).start()
            return c

        def wait(u, c):
            _unit_copy(acc, u, ys_hbm, u0 + u, sem_out).wait()
            return c

        lax.fori_loop(0, n, start, 0)
        lax.fori_loop(0, n, wait, 0)


def _moe_gmm(tile_expert, tile_unit0, tile_nunits, total_units, xs, w_gu, w_down, lm):
    nj = D_FF_EXPERT // MOE_FF_TILE
    jj = lambda t, j, tn: jnp.where(tn[t] > 0, j, nj - 1)
    tile_rows = MOE_TILE_UNITS * ROWS_PER_UNIT
    grid_spec = pltpu.PrefetchScalarGridSpec(
        num_scalar_prefetch=4,
        grid=(MOE_TILES, nj),
        in_specs=[pl.BlockSpec(memory_space=pl.ANY),
                  pl.BlockSpec((None, None, D_MODEL, MOE_FF_TILE),
                               lambda t, j, te, tu, tn, nu: (lm, te[t], 0, jj(t, j, tn))),
                  pl.BlockSpec((None, None, D_MODEL, MOE_FF_TILE),
                               lambda t, j, te, tu, tn, nu: (lm, te[t], 0, nj + jj(t, j, tn))),
                  pl.BlockSpec((None, None, MOE_FF_TILE, D_MODEL),
                               lambda t, j, te, tu, tn, nu: (lm, te[t], jj(t, j, tn), 0))],
        out_specs=pl.BlockSpec(memory_space=pl.ANY),
        scratch_shapes=[pltpu.VMEM((tile_rows, D_MODEL), BF16), pltpu.VMEM((tile_rows, D_MODEL), F32),
                        pltpu.VMEM((D_MODEL, 2 * MOE_FF_TILE), BF16), pltpu.VMEM((MOE_FF_TILE, D_MODEL), BF16),
                        pltpu.SemaphoreType.DMA(()), pltpu.SemaphoreType.DMA(())],
    )
    return pl.pallas_call(
        _moe_gmm_body,
        grid_spec=grid_spec,
        out_shape=jax.ShapeDtypeStruct((MOE_ROWS, D_MODEL), F32),
        compiler_params=_params("arbitrary", "arbitrary"),
        name="moe_gmm",
    )(tile_expert, tile_unit0, tile_nunits, total_units, xs, w_gu, w_gu, w_down)


def _moe_combine_body(dest_ref, ys_hbm, rw_ref, x_ref, mpg_ref, mpn_ref, gs_ref, shs_ref, scs_ref, lg_ref, lb_ref,
                      xo_ref, ho_ref, buf, sem, *, pg, psh, psc):
    i = pl.program_id(0)
    slot = i & 1

    def issue(unit, s):
        base = unit * (ROWS_PER_UNIT * TOP_K)

        def start(r, c):
            for k in range(TOP_K):
                _row_copy(ys_hbm, dest_ref[base + TOP_K * r + k], buf.at[s, k], r, sem.at[s]).start()
            return c

        lax.fori_loop(0, ROWS_PER_UNIT, start, 0, unroll=ROW_DMA_UNROLL // TOP_K)

    @pl.when(i == 0)
    def _():
        issue(0, 0)

    @pl.when(i + 1 < pl.num_programs(0))
    def _():
        issue(i + 1, 1 - slot)

    def wait(r, c):
        for k in range(TOP_K):
            _row_copy(ys_hbm, 0, buf.at[slot, k], r, sem.at[slot]).wait()
        return c

    lax.fori_loop(0, ROWS_PER_UNIT, wait, 0, unroll=ROW_DMA_UNROLL // TOP_K)
    y = rw_ref[:, 0:1] * buf[slot, 0] + rw_ref[:, 1:2] * buf[slot, 1]
    _finish_dispatch(i, x_ref[...], lambda prompt: y, pg, psh, psc, mpg_ref, mpn_ref, gs_ref, shs_ref, scs_ref,
                     lg_ref, lb_ref, xo_ref, ho_ref, None)


def _moe_combine(dest, ys, route_w, x, modp, mods, ln_g, ln_b, layer, next_layer, pg, psh, psc):
    unit = lambda w: pl.BlockSpec((ROWS_PER_UNIT, w), lambda i, d: (i, 0))
    grid_spec = pltpu.PrefetchScalarGridSpec(
        num_scalar_prefetch=1,
        grid=(N_UNITS,),
        in_specs=[pl.BlockSpec(memory_space=pl.ANY), unit(TOP_K), unit(D_MODEL)]
        + _epilogue_specs(layer, next_layer, pg, psh, psc),
        out_specs=[unit(D_MODEL), unit(D_MODEL)],
        scratch_shapes=[pltpu.VMEM((2, TOP_K, ROWS_PER_UNIT, D_MODEL), F32), pltpu.SemaphoreType.DMA((2,))],
    )
    return pl.pallas_call(
        functools.partial(_moe_combine_body, pg=pg, psh=psh, psc=psc),
        grid_spec=grid_spec,
        out_shape=[jax.ShapeDtypeStruct((M_ROWS, D_MODEL), F32), jax.ShapeDtypeStruct((M_ROWS, D_MODEL), BF16)],
        compiler_params=_params("arbitrary"),
        name="moe_combine",
    )(dest, ys, route_w, x, modp, modp, mods, mods, mods, ln_g.reshape(1, D_MODEL), ln_b.reshape(1, D_MODEL))


def _moe_plan(route_i):
    i32 = jnp.int32
    e_flat = route_i.reshape(-1)
    onehot = (e_flat[:, None] == jnp.arange(N_EXPERTS, dtype=i32)[None, :]).astype(i32)
    csum = jnp.cumsum(onehot, axis=0)
    rank = jnp.sum(csum * onehot, axis=1) - 1
    counts = csum[-1]
    units_e = (counts + ROWS_PER_UNIT - 1) // ROWS_PER_UNIT
    unit_end = jnp.cumsum(units_e)
    unit_start = unit_end - units_e
    dest = (unit_start[e_flat] * ROWS_PER_UNIT + rank).astype(i32)
    pair_tok = jnp.arange(N_PAIRS, dtype=i32) // TOP_K
    src_tok = jnp.zeros((MOE_ROWS,), i32).at[dest].set(pair_tok)
    total_units = unit_end[-1:].astype(i32)
    tiles_e = (units_e + MOE_TILE_UNITS - 1) // MOE_TILE_UNITS
    tile_end = jnp.cumsum(tiles_e)
    tile_start = tile_end - tiles_e
    n_tiles = tile_end[-1]
    t_ids = jnp.arange(MOE_TILES, dtype=i32)
    t_eff = jnp.minimum(t_ids, n_tiles - 1)
    tile_expert = jnp.sum((t_eff[:, None] >= tile_end[None, :]).astype(i32), axis=1)
    k = t_eff - tile_start[tile_expert]
    tile_unit0 = unit_start[tile_expert] + k * MOE_TILE_UNITS
    units_left = units_e[tile_expert] - k * MOE_TILE_UNITS
    tile_nunits = jnp.where(t_ids < n_tiles, jnp.minimum(units_left, MOE_TILE_UNITS), 0)
    return dest, src_tok, total_units, tile_expert.astype(i32), tile_unit0.astype(i32), tile_nunits.astype(i32)


def kernel(x_prompt, x_sample, cache_win_k, cache_win_v, c_prompt, c_sample, w_ada, b_ada, w_in, b_in, v_norm_g, v_norm_b, w_spatial, b_spatial, attn_sinks, w_out, b_out, ln1_g, ln1_b, ln2_g, ln2_b, w_ffn_gu, w_ffn_down, w_router, w_exp_gu, w_exp_down):
    c_rows = jnp.concatenate([c_prompt, c_sample, jnp.zeros((ADA_ROWS - BATCH - DEC_BATCH, D_MODEL), F32)], axis=0)
    mod = _adaln(c_rows, w_ada, b_ada)
    modp = mod[:, :BATCH].reshape(DEPTH, BATCH, 6, D_MODEL)
    mods = jnp.repeat(mod[:, BATCH:BATCH + DEC_BATCH], DEC_SEQ, axis=1)

    x = jnp.concatenate([x_prompt.reshape(M_PROMPT, D_MODEL), x_sample.reshape(M_SAMPLE, D_MODEL)], axis=0)
    h = _modulate0(x, modp, mods)

    rows = GQA * DEC_SEQ
    kp_list, vp_list, ks_list, vs_list, cv_list = [], [], [], [], []
    for l in range(DEPTH):
        nl = min(l + 1, DEPTH - 1)
        u, v, q, kv = _proj(h, w_in, l, b_in[l], v_norm_g[l], v_norm_b[l])
        b_s_t = b_spatial[l].T
        mix_p = _mix_prompt(attn_sinks[l].reshape(-1), u, v, q, kv, w_spatial[l], b_s_t)

        u_s = u[M_PROMPT:].reshape(DEC_BATCH, DEC_SEQ, A_WIDTH)
        v_s = v[M_PROMPT:].reshape(DEC_BATCH, DEC_SEQ, A_WIDTH)
        kv_s = kv[M_PROMPT:].reshape(DEC_BATCH, DEC_SEQ, 2 * KV_WIDTH)
        q_s = q[M_PROMPT:].reshape(DEC_BATCH, DEC_SEQ, N_KV_HEADS, GQA, HEAD_DIM)
        q_s = q_s.transpose(0, 2, 3, 1, 4).reshape(DEC_BATCH, N_KV_HEADS, rows, HEAD_DIM)
        ck = cache_win_k[l].reshape(DEC_BATCH, WINDOW, KV_WIDTH)
        cv = cache_win_v[l].reshape(DEC_BATCH, WINDOW, KV_WIDTH)
        sink_rows = jnp.repeat(attn_sinks[l], DEC_SEQ, axis=1).reshape(N_KV_HEADS, rows, 1)
        a_s, o_s = _mix_sample(u_s, v_s, q_s, kv_s, ck, cv, w_spatial[l], b_s_t, sink_rows)
        o_s = o_s.reshape(DEC_BATCH, N_KV_HEADS, GQA, DEC_SEQ, HEAD_DIM).transpose(0, 3, 1, 2, 4)
        mix_s = jnp.concatenate([a_s.reshape(M_SAMPLE, A_WIDTH), o_s.reshape(M_SAMPLE, B_WIDTH)], axis=1)

        y = (mix_p, mix_s, w_out, b_out[l])
        if l % 2 == 0:
            x, h = _epilogue(x, y, modp, mods, ln1_g[l], ln1_b[l], l, l, P_GATE1, P_SHIFT2, P_SCALE2)
            ld = l // 2
            pad = ((0, 0), (0, D_FF_PAD - D_FF))
            wg = jnp.pad(w_ffn_gu[ld][:, :D_FF].astype(BF16), pad)
            wu = jnp.pad(w_ffn_gu[ld][:, D_FF:].astype(BF16), pad)
            wd = jnp.pad(w_ffn_down[ld].astype(BF16), ((0, D_FF_PAD - D_FF), (0, 0)))
            f = _ffn(h, wg, wu, wd)
            x, h = _epilogue(x, f, modp, mods, ln2_g[l], ln2_b[l], l, nl, P_GATE2, P_SHIFT1, P_SCALE1)
        else:
            lm = l // 2
            x, h, h_f32, route_i, route_w = _epilogue(x, y, modp, mods, ln1_g[l], ln1_b[l], l, l,
                                                      P_GATE1, P_SHIFT2, P_SCALE2, w_router_t=w_router[lm].T)
            dest, src_tok, total_units, tile_expert, tile_unit0, tile_nunits = _moe_plan(route_i)
            xs = _moe_gather(src_tok, total_units, h_f32)
            ys = _moe_gmm(tile_expert, tile_unit0, tile_nunits, total_units, xs, w_exp_gu, w_exp_down, lm)
            x, h = _moe_combine(dest, ys, route_w, x, modp, mods, ln2_g[l], ln2_b[l], l, nl,
                                P_GATE2, P_SHIFT1, P_SCALE1)

        kv_p = kv[:M_PROMPT].reshape(BATCH, SEQ, 2 * KV_WIDTH)[:, SEQ - WINDOW:]
        kp_list.append(kv_p[..., :KV_WIDTH].reshape(BATCH, WINDOW, N_KV_HEADS, HEAD_DIM))
        vp_list.append(kv_p[..., KV_WIDTH:].reshape(BATCH, WINDOW, N_KV_HEADS, HEAD_DIM))
        k_new = kv_s[..., :KV_WIDTH].reshape(DEC_BATCH, DEC_SEQ, N_KV_HEADS, HEAD_DIM)
        v_new = kv_s[..., KV_WIDTH:].reshape(DEC_BATCH, DEC_SEQ, N_KV_HEADS, HEAD_DIM)
        ks_list.append(jnp.concatenate([cache_win_k[l][:, DEC_SEQ:], k_new], axis=1))
        vs_list.append(jnp.concatenate([cache_win_v[l][:, DEC_SEQ:], v_new], axis=1))
        cv_list.append(v_s.astype(F32))

    y_prompt = x[:M_PROMPT].reshape(BATCH, SEQ, D_MODEL)
    y_sample = x[M_PROMPT:].reshape(DEC_BATCH, DEC_SEQ, D_MODEL)
    return (y_prompt, y_sample, jnp.stack(kp_list), jnp.stack(vp_list), jnp.stack(ks_list), jnp.stack(vs_list),
            jnp.stack(cv_list))
```

```python
import functools

import jax
import jax.numpy as jnp
from jax import lax
from jax.experimental import pallas as pl
from jax.experimental.pallas import tpu as pltpu

F32 = jnp.float32
BF16 = jnp.bfloat16

D_MODEL = 2048
BATCH = 4
SEQ = 2048
DEPTH = 4
DEC_BATCH = 32
DEC_SEQ = 8
CHUNK = 128
A_WIDTH = D_MODEL // 2
A_GROUPS = 4
A_GROUP_DIM = A_WIDTH // A_GROUPS
HEAD_DIM = 64
B_WIDTH = D_MODEL - A_WIDTH
N_HEADS = B_WIDTH // HEAD_DIM
N_KV_HEADS = 2
GQA = N_HEADS // N_KV_HEADS
KV_WIDTH = N_KV_HEADS * HEAD_DIM
WINDOW = 128
IN_COLS = 2 * A_WIDTH + B_WIDTH + 2 * KV_WIDTH
D_FF = 5504
N_EXPERTS = 8
TOP_K = 2
D_FF_EXPERT = 7168
ALPHA = (2.0 * DEPTH) ** 0.25
LN_EPS = 1e-5
NEG_INF = -1e30

M_PROMPT = BATCH * SEQ
M_SAMPLE = DEC_BATCH * DEC_SEQ
M_ROWS = M_PROMPT + M_SAMPLE
ROWS_PER_UNIT = 256
N_PROMPT_UNITS = M_PROMPT // ROWS_PER_UNIT
N_UNITS = M_ROWS // ROWS_PER_UNIT
UNITS_PER_SEQ = SEQ // ROWS_PER_UNIT
ADA_ROWS = 40
P_SHIFT1, P_SCALE1, P_GATE1, P_SHIFT2, P_SCALE2, P_GATE2 = range(6)

D_FF_PAD = 5632
FFN_TILE = 512
FFN_ROW_TILE = 4 * ROWS_PER_UNIT
MOE_FF_TILE = 256
N_PAIRS = M_ROWS * TOP_K
MOE_UNITS = N_PAIRS // ROWS_PER_UNIT + N_EXPERTS
MOE_ROWS = MOE_UNITS * ROWS_PER_UNIT
MOE_TILE_UNITS = 9
MOE_TILES =MOE_UNITS // MOE_TILE_UNITS + N_EXPERTS

VMEM_LIMIT = 56 * 1024 * 1024


def _params(*sem):
    return pltpu.CompilerParams(dimension_semantics=sem, vmem_limit_bytes=VMEM_LIMIT)


def _dot(a, b):
    return jnp.dot(a, b, preferred_element_type=F32)


def _layer_norm(z, g, b):
    mu = jnp.mean(z, axis=-1, keepdims=True)
    zc = z - mu
    var = jnp.mean(zc * zc, axis=-1, keepdims=True)
    return zc * lax.rsqrt(var + LN_EPS) * g + b


def _gelu_tanh(x):
    return x * (0.5 * (1.0 + jnp.tanh(0.7978845608028654 * (x + 0.044715 * (x * x * x)))))


def _silu(x):
    return x * jax.nn.sigmoid(x)


def _prompt_batch_of_unit(i):
    return jnp.minimum(i // UNITS_PER_SEQ, BATCH - 1)


def _adaln_body(c_ref, w_ref, b_ref, o_ref):
    s = _silu(c_ref[...]).astype(BF16)
    o_ref[...] = _dot(s, w_ref[...].astype(BF16)) + b_ref[...]


def _adaln(c_rows, w_ada, b_ada):
    tn = 1024
    return pl.pallas_call(
        _adaln_body,
        grid=(DEPTH, 6 * D_MODEL // tn),
        in_specs=[pl.BlockSpec((ADA_ROWS, D_MODEL), lambda l, n: (0, 0)),
                  pl.BlockSpec((None, D_MODEL, tn), lambda l, n: (l, 0, n)),
                  pl.BlockSpec((None, 1, tn), lambda l, n: (l, 0, n))],
        out_specs=pl.BlockSpec((None, ADA_ROWS, tn), lambda l, n: (l, 0, n)),
        out_shape=jax.ShapeDtypeStruct((DEPTH, ADA_ROWS, 6 * D_MODEL), F32),
        compiler_params=_params("arbitrary", "arbitrary"),
        name="adaln",
    )(c_rows, w_ada, b_ada.reshape(DEPTH, 1, 6 * D_MODEL))


def _mod0_body(x_ref, mp_ref, ss_ref, cs_ref, h_ref):
    i = pl.program_id(0)
    x = x_ref[...]

    @pl.when(i < N_PROMPT_UNITS)
    def _():
        h_ref[...] = (x * (1 + mp_ref[P_SCALE1:P_SCALE1 + 1, :]) + mp_ref[P_SHIFT1:P_SHIFT1 + 1, :]).astype(BF16)

    @pl.when(i >= N_PROMPT_UNITS)
    def _():
        h_ref[...] = (x * (1 + cs_ref[...]) + ss_ref[...]).astype(BF16)


def _unit_spec(width):
    return pl.BlockSpec((ROWS_PER_UNIT, width), lambda i: (i, 0))


def _modp_spec(layer):
    return pl.BlockSpec((None, None, 6, D_MODEL), lambda i: (layer, _prompt_batch_of_unit(i), 0, 0))


def _mods_spec(layer, p):
    return pl.BlockSpec((None, ROWS_PER_UNIT, D_MODEL), lambda i, *_: (layer, 0, p))


def _modulate0(x, modp, mods):
    return pl.pallas_call(
        _mod0_body,
        grid=(N_UNITS,),
        in_specs=[_unit_spec(D_MODEL), _modp_spec(0), _mods_spec(0, P_SHIFT1), _mods_spec(0, P_SCALE1)],
        out_specs=_unit_spec(D_MODEL),
        out_shape=jax.ShapeDtypeStruct((M_ROWS, D_MODEL), BF16),
        compiler_params=_params("arbitrary"),
        name="modulate0",
    )(x, modp, mods, mods)


def _cast_weight_once(w32_ref, w_ref, chunk):
    @pl.when(pl.program_id(0) == 0)
    def _():
        for c in range(0, w32_ref.shape[1], chunk):
            w_ref[:, c:c + chunk] = w32_ref[:, c:c + chunk].astype(BF16)


def _resident(layer, shape):
    return pl.BlockSpec((None,) + shape, lambda i: (layer,) + (0,) * len(shape), pipeline_mode=pl.Buffered(1))


def _proj_body(h_ref, w32_ref, b_ref, gv_ref, bv_ref, u_ref, v_ref, q_ref, kv_ref, w_ref):
    _cast_weight_once(w32_ref, w_ref, A_GROUP_DIM)
    h = h_ref[...]
    gd = A_GROUP_DIM
    for c in range(A_GROUPS):
        sl = slice(c * gd, (c + 1) * gd)
        z = _dot(h, w_ref[:, sl]) + b_ref[:, sl]
        u_ref[:, sl] = _gelu_tanh(z).astype(BF16)
    for g in range(A_GROUPS):
        sl = slice(A_WIDTH + g * gd, A_WIDTH + (g + 1) * gd)
        z = _gelu_tanh(_dot(h, w_ref[:, sl]) + b_ref[:, sl])
        vn = _layer_norm(z, gv_ref[g:g + 1, :], bv_ref[g:g + 1, :])
        v_ref[:, g * gd:(g + 1) * gd] = vn.astype(BF16)
    o1 = 2 * A_WIDTH
    o2 = o1 + B_WIDTH
    z = _dot(h, w_ref[:, o1:o2]) + b_ref[:, o1:o2]
    q_ref[...] = (z * (HEAD_DIM ** -0.5)).astype(BF16)
    kv_ref[...] = _dot(h, w_ref[:, o2:]) + b_ref[:, o2:]


def _proj(h, w_all, layer, b, gv, bv):
    const = lambda shape: pl.BlockSpec(shape, lambda i: (0,) * len(shape))
    return pl.pallas_call(
        _proj_body,
        grid=(N_UNITS,),
        in_specs=[_unit_spec(D_MODEL), _resident(layer, (D_MODEL, IN_COLS)), const((1, IN_COLS)),
                  const((A_GROUPS, A_GROUP_DIM)), const((A_GROUPS, A_GROUP_DIM))],
        out_specs=[_unit_spec(A_WIDTH), _unit_spec(A_WIDTH), _unit_spec(B_WIDTH), _unit_spec(2 * KV_WIDTH)],
        out_shape=[jax.ShapeDtypeStruct((M_ROWS, A_WIDTH), BF16), jax.ShapeDtypeStruct((M_ROWS, A_WIDTH), BF16),
                   jax.ShapeDtypeStruct((M_ROWS, B_WIDTH), BF16), jax.ShapeDtypeStruct((M_ROWS, 2 * KV_WIDTH), F32)],
        scratch_shapes=[pltpu.VMEM((D_MODEL, IN_COLS), BF16)],
        compiler_params=_params("arbitrary"),
        name="proj",
    )(h, w_all, b.reshape(1, IN_COLS), gv, bv)


def _mix_prompt_body(sink_ref, u_ref, v_ref, q_ref, kvc_ref, kvp_ref, ws_ref, bs_ref, o_ref):
    blk = pl.program_id(1)
    gd = A_GROUP_DIM
    tr = lax.broadcasted_iota(jnp.int32, (CHUNK, CHUNK), 0)
    tc = lax.broadcasted_iota(jnp.int32, (CHUNK, CHUNK), 1)
    for g in range(A_GROUPS):
        sl = slice(g * gd, (g + 1) * gd)
        w = jnp.where(tr >= tc, ws_ref[g], 0.0).astype(BF16)
        s = _dot(w, v_ref[:, sl]) + bs_ref[:, g:g + 1]
        o_ref[:, sl] = (u_ref[:, sl].astype(F32) * s).astype(BF16)

    kvc = kvc_ref[...]
    kvp = kvp_ref[...]
    row = lax.broadcasted_iota(jnp.int32, (WINDOW, 2 * WINDOW), 0)
    col = lax.broadcasted_iota(jnp.int32, (WINDOW, 2 * WINDOW), 1)
    dlt = row + WINDOW - col
    mask = (dlt >= 0) & (dlt <= WINDOW) & ((col >= WINDOW) | (blk > 0))
    hd = HEAD_DIM
    for j in range(N_KV_HEADS):
        ks = slice(j * hd, (j + 1) * hd)
        vs = slice(KV_WIDTH + j * hd, KV_WIDTH + (j + 1) * hd)
        kk = jnp.concatenate([kvp[:, ks], kvc[:, ks]], axis=0).astype(BF16)
        vv = jnp.concatenate([kvp[:, vs], kvc[:, vs]], axis=0).astype(BF16)
        for hh in range(GQA):
            h = j * GQA + hh
            qh = q_ref[:, h * hd:(h + 1) * hd]
            s = lax.dot_general(qh, kk, (((1,), (1,)), ((), ())), preferred_element_type=F32)
            s = jnp.where(mask, s, NEG_INF)
            sk = sink_ref[h]
            m = jnp.maximum(jnp.max(s, axis=-1, keepdims=True), sk)
            p = jnp.exp(s - m)
            den = jnp.sum(p, axis=-1, keepdims=True) + jnp.exp(sk - m)
            o = _dot(p.astype(BF16), vv) / den
            o_ref[:, A_WIDTH + h * hd:A_WIDTH + (h + 1) * hd] = o.astype(BF16)


def _mix_prompt(sinks, u, v, q, kv, w_s, b_s_t):
    nb = SEQ // WINDOW
    row_blk = lambda w: pl.BlockSpec((WINDOW, w), lambda b, i, s: (b * nb + i, 0))
    grid_spec = pltpu.PrefetchScalarGridSpec(
        num_scalar_prefetch=1,
        grid=(BATCH, nb),
        in_specs=[row_blk(A_WIDTH), row_blk(A_WIDTH), row_blk(B_WIDTH), row_blk(2 * KV_WIDTH),
                  pl.BlockSpec((WINDOW, 2 * KV_WIDTH), lambda b, i, s: (b * nb + jnp.maximum(i - 1, 0), 0)),
                  pl.BlockSpec((A_GROUPS, CHUNK, CHUNK), lambda b, i, s: (0, 0, 0)),
                  pl.BlockSpec((CHUNK, A_GROUPS), lambda b, i, s: (0, 0))],
        out_specs=pl.BlockSpec((WINDOW, D_MODEL), lambda b, i, s: (b * nb + i, 0)),
    )
    return pl.pallas_call(
        _mix_prompt_body,
        grid_spec=grid_spec,
        out_shape=jax.ShapeDtypeStruct((M_PROMPT, D_MODEL), BF16),
        compiler_params=_params("arbitrary", "arbitrary"),
        name="mix_prompt",
    )(sinks, u, v, q, kv, kv, w_s, b_s_t)


SAMPLE_KEYS = 2 * WINDOW


def _mix_sample_body(u_ref, v_ref, q_ref, kn_ref, ck_ref, cv_ref, ws_ref, bs_ref, sk_ref, a_ref, o_ref):
    gd = A_GROUP_DIM
    ds = DEC_SEQ
    v = v_ref[...].astype(F32)
    u = u_ref[...].astype(F32)
    tr = lax.broadcasted_iota(jnp.int32, (ds, ds), 0)
    tc = lax.broadcasted_iota(jnp.int32, (ds, ds), 1)
    for g in range(A_GROUPS):
        sl = slice(g * gd, (g + 1) * gd)
        w = jnp.where(tr >= tc, ws_ref[g][:ds, :ds], 0.0)
        s = jnp.zeros((ds, gd), F32) + bs_ref[:ds, g:g + 1]
        for t in range(ds):
            s = s + w[:, t:t + 1] * v[t:t + 1, sl]
        a_ref[:, sl] = (u[:, sl] * s).astype(BF16)

    kn = kn_ref[...]
    ck = ck_ref[...]
    cv = cv_ref[...]
    hd = HEAD_DIM
    rows = GQA * ds
    qt = lax.broadcasted_iota(jnp.int32, (rows, SAMPLE_KEYS), 0) & (ds - 1)
    col = lax.broadcasted_iota(jnp.int32, (rows, SAMPLE_KEYS), 1)
    mask = ((col < WINDOW) & (col >= qt)) | ((col >= WINDOW) & (col - WINDOW <= qt))
    pad = jnp.zeros((SAMPLE_KEYS - WINDOW - ds, hd), F32)
    for j in range(N_KV_HEADS):
        ks = slice(j * hd, (j + 1) * hd)
        vs = slice(KV_WIDTH + j * hd, KV_WIDTH + (j + 1) * hd)
        kk = jnp.concatenate([ck[:, ks], kn[:, ks], pad], axis=0).astype(BF16)
        vv = jnp.concatenate([cv[:, ks], kn[:, vs], pad], axis=0).astype(BF16)
        s = lax.dot_general(q_ref[j], kk, (((1,), (1,)), ((), ())), preferred_element_type=F32)
        s = jnp.where(mask, s, NEG_INF)
        sk = sk_ref[j]
        m = jnp.maximum(jnp.max(s, axis=-1, keepdims=True), sk)
        p = jnp.exp(s - m)
        den = jnp.sum(p, axis=-1, keepdims=True) + jnp.exp(sk - m)
        o_ref[j] = (_dot(p.astype(BF16), vv) / den).astype(BF16)


def _mix_sample(u_s, v_s, q_s, kv_s, cache_k, cache_v, w_s, b_s_t, sink_rows):
    rows = GQA * DEC_SEQ
    per_b = lambda *shape: pl.BlockSpec((None,) + shape, lambda b: (b,) + (0,) * len(shape))
    const = lambda *shape: pl.BlockSpec(shape, lambda b: (0,) * len(shape))
    return pl.pallas_call(
        _mix_sample_body,
        grid=(DEC_BATCH,),
        in_specs=[per_b(DEC_SEQ, A_WIDTH), per_b(DEC_SEQ, A_WIDTH), per_b(N_KV_HEADS, rows, HEAD_DIM),
                  per_b(DEC_SEQ, 2 * KV_WIDTH), per_b(WINDOW, KV_WIDTH), per_b(WINDOW, KV_WIDTH),
                  const(A_GROUPS, CHUNK, CHUNK), const(CHUNK, A_GROUPS), const(N_KV_HEADS, rows, 1)],
        out_specs=[per_b(DEC_SEQ, A_WIDTH), per_b(N_KV_HEADS, rows, HEAD_DIM)],
        out_shape=[jax.ShapeDtypeStruct((DEC_BATCH, DEC_SEQ, A_WIDTH), BF16),
                   jax.ShapeDtypeStruct((DEC_BATCH, N_KV_HEADS, rows, HEAD_DIM), BF16)],
        compiler_params=_params("arbitrary"),
        name="mix_sample",
    )(u_s, v_s, q_s, kv_s, cache_k, cache_v, w_s, b_s_t, sink_rows)


PACKED_WIDTH = D_MODEL // 2


def _pack_bf16_pairs(h):
    bits = lambda v: lax.bitcast_convert_type(v.astype(BF16).astype(F32), jnp.uint32)
    lo = lax.shift_right_logical(bits(h[:, :PACKED_WIDTH]), jnp.uint32(16))
    hi = bits(h[:, PACKED_WIDTH:]) & jnp.uint32(0xFFFF0000)
    return lo | hi


def _unpack_bf16_pairs(w):
    lo = lax.bitcast_convert_type(lax.shift_left(w, jnp.uint32(16)), F32).astype(BF16)
    hi = lax.bitcast_convert_type(w & jnp.uint32(0xFFFF0000), F32).astype(BF16)
    return lo, hi


def _route_top2(h, rs, wrt_ref, ri_ref, rw_ref):
    logits = [jnp.sum(h * wrt_ref[e:e + 1, :], axis=-1, keepdims=True) for e in range(N_EXPERTS)]
    m1 = logits[0]
    i1 = jnp.zeros_like(m1, dtype=jnp.int32)
    for e in range(1, N_EXPERTS):
        gt = logits[e] > m1
        m1 = jnp.where(gt, logits[e], m1)
        i1 = jnp.where(gt, e, i1)
    m2 = jnp.full_like(m1, -jnp.inf)
    i2 = jnp.zeros_like(i1)
    for e in range(N_EXPERTS):
        ok = (i1 != e) & (logits[e] > m2)
        m2 = jnp.where(ok, logits[e], m2)
        i2 = jnp.where(ok, e, i2)
    w1 = 1.0 / (1.0 + jnp.exp(m2 - m1))
    ri_ref[rs, 0:1] = i1
    ri_ref[rs, 1:2] = i2
    rw_ref[rs, 0:1] = w1
    rw_ref[rs, 1:2] = 1.0 - w1


FINISH_SPLIT = 2


def _finish_rows(rs, x, y, gate, shift, scale, lg_ref, lb_ref, xo_ref, ho_ref, route_refs):
    xn = _layer_norm(ALPHA * x + (1 + gate) * y, lg_ref[...], lb_ref[...])
    xo_ref[rs, :] = xn
    h = xn * (1 + scale) + shift
    ho_ref[rs, :] = h.astype(BF16)
    if route_refs is not None:
        wrt_ref, hp_ref, ri_ref, rw_ref = route_refs
        hp_ref[rs, :] = _pack_bf16_pairs(h)
        _route_top2(h, rs, wrt_ref, ri_ref, rw_ref)


def _finish_dispatch(i, x_ref, y_of, pg, psh, psc, mpg_ref, mpn_ref, gs_ref, shs_ref, scs_ref,
                     lg_ref, lb_ref, xo_ref, ho_ref, route_refs):
    n = ROWS_PER_UNIT // FINISH_SPLIT
    slices = [slice(k * n, (k + 1) * n) for k in range(FINISH_SPLIT)]

    @pl.when(i < N_PROMPT_UNITS)
    def _():
        for rs in slices:
            _finish_rows(rs, x_ref[rs, :], y_of(True, rs), mpg_ref[pg:pg + 1, :], mpn_ref[psh:psh + 1, :],
                         mpn_ref[psc:psc + 1, :], lg_ref, lb_ref, xo_ref, ho_ref, route_refs)

    @pl.when(i >= N_PROMPT_UNITS)
    def _():
        for rs in slices:
            _finish_rows(rs, x_ref[rs, :], y_of(False, rs), gs_ref[rs, :], shs_ref[rs, :], scs_ref[rs, :],
                         lg_ref, lb_ref, xo_ref, ho_ref, route_refs)


def _epilogue_body(*refs, pg, psh, psc, route, project):
    refs = list(refs)
    if project:
        mp_ref, ms_ref, w32_ref, b_ref = refs[:4]
        w_ref = refs.pop()
        refs = refs[4:]
        _cast_weight_once(w32_ref, w_ref, ROWS_PER_UNIT)
        y_of = lambda prompt, rs: _dot((mp_ref if prompt else ms_ref)[rs, :], w_ref[...]) + b_ref[...]
    else:
        y_ref = refs.pop(0)
        y_of = lambda prompt, rs: y_ref[rs, :]
    x_ref, mpg_ref, mpn_ref, gs_ref, shs_ref, scs_ref, lg_ref, lb_ref = refs[:8]
    if route:
        wrt_ref, xo_ref, ho_ref, hp_ref, ri_ref, rw_ref = refs[8:]
        route_refs = (wrt_ref, hp_ref, ri_ref, rw_ref)
    else:
        xo_ref, ho_ref = refs[8:]
        route_refs = None
    _finish_dispatch(pl.program_id(0), x_ref, y_of, pg, psh, psc, mpg_ref, mpn_ref,
                     gs_ref, shs_ref, scs_ref, lg_ref, lb_ref, xo_ref, ho_ref, route_refs)


def _epilogue_specs(layer, next_layer, pg, psh, psc):
    row = pl.BlockSpec((1, D_MODEL), lambda i, *_: (0, 0))
    modp = lambda l: pl.BlockSpec((None, None, 6, D_MODEL), lambda i, *_: (l, _prompt_batch_of_unit(i), 0, 0))
    return [modp(layer), modp(next_layer), _mods_spec(layer, pg), _mods_spec(next_layer, psh),
            _mods_spec(next_layer, psc), row, row]


def _epilogue(x, y, modp, mods, ln_g, ln_b, layer, next_layer, pg, psh, psc, w_router_t=None):
    route = w_router_t is not None
    project = isinstance(y, tuple)
    unit = lambda w: pl.BlockSpec((ROWS_PER_UNIT, w), lambda i: (i, 0))
    if project:
        mix_p, mix_s, w_out, b_out = y
        in_specs = [pl.BlockSpec((ROWS_PER_UNIT, D_MODEL), lambda i: (jnp.minimum(i, N_PROMPT_UNITS - 1), 0)),
                    pl.BlockSpec((M_SAMPLE, D_MODEL), lambda i: (0, 0)),
                    _resident(layer, (D_MODEL, D_MODEL)),
                    pl.BlockSpec((1, D_MODEL), lambda i: (0, 0))]
        args = [mix_p, mix_s, w_out, b_out.reshape(1, D_MODEL)]
        scratch = [pltpu.VMEM((D_MODEL, D_MODEL), BF16)]
    else:
        in_specs = [unit(D_MODEL)]
        args = [y]
        scratch = []
    in_specs += [unit(D_MODEL)] + _epilogue_specs(layer, next_layer, pg, psh, psc)
    args += [x, modp, modp, mods, mods, mods, ln_g.reshape(1, D_MODEL), ln_b.reshape(1, D_MODEL)]
    out_specs = [unit(D_MODEL), unit(D_MODEL)]
    out_shape = [jax.ShapeDtypeStruct((M_ROWS, D_MODEL), F32), jax.ShapeDtypeStruct((M_ROWS, D_MODEL), BF16)]
    if route:
        in_specs.append(pl.BlockSpec((N_EXPERTS, D_MODEL), lambda i: (0, 0)))
        args.append(w_router_t)
        out_specs += [unit(PACKED_WIDTH), unit(TOP_K), unit(TOP_K)]
        out_shape += [jax.ShapeDtypeStruct((M_ROWS, PACKED_WIDTH), jnp.uint32),
                      jax.ShapeDtypeStruct((M_ROWS, TOP_K), jnp.int32),
                      jax.ShapeDtypeStruct((M_ROWS, TOP_K), F32)]
    return pl.pallas_call(
        functools.partial(_epilogue_body, pg=pg, psh=psh, psc=psc, route=route, project=project),
        grid=(N_UNITS,),
        in_specs=in_specs,
        out_specs=out_specs,
        out_shape=out_shape,
        scratch_shapes=scratch,
        compiler_params=_params("arbitrary"),
        name=("outproj_" if project else "") + ("epilogue_route" if route else "epilogue"),
    )(*args)


FF_LANE_BLOCKS = D_FF // 128


def _ffn_prep_body(g_ref, u_ref, d_ref, go_ref, uo_ref, do_ref):
    c = pl.program_id(0)

    @pl.when(c < FF_LANE_BLOCKS)
    def _():
        go_ref[...] = g_ref[...].astype(BF16)
        uo_ref[...] = u_ref[...].astype(BF16)
        do_ref[...] = d_ref[...].astype(BF16)

    @pl.when(c >= FF_LANE_BLOCKS)
    def _():
        go_ref[...] = jnp.zeros_like(go_ref)
        uo_ref[...] = jnp.zeros_like(uo_ref)
        do_ref[...] = jnp.zeros_like(do_ref)


def _ffn_prep(w_gu, w_down, ld):
    src = lambda c: jnp.minimum(c, FF_LANE_BLOCKS - 1)
    return pl.pallas_call(
        _ffn_prep_body,
        grid=(D_FF_PAD // 128,),
        in_specs=[pl.BlockSpec((None, D_MODEL, 128), lambda c: (ld, 0, src(c))),
                  pl.BlockSpec((None, D_MODEL, 128), lambda c: (ld, 0, FF_LANE_BLOCKS + src(c))),
                  pl.BlockSpec((None, 128, D_MODEL), lambda c: (ld, src(c), 0))],
        out_specs=[pl.BlockSpec((D_MODEL, 128), lambda c: (0, c)),
                   pl.BlockSpec((D_MODEL, 128), lambda c: (0, c)),
                   pl.BlockSpec((128, D_MODEL), lambda c: (c, 0))],
        out_shape=[jax.ShapeDtypeStruct((D_MODEL, D_FF_PAD), BF16), jax.ShapeDtypeStruct((D_MODEL, D_FF_PAD), BF16),
                   jax.ShapeDtypeStruct((D_FF_PAD, D_MODEL), BF16)],
        compiler_params=_params("arbitrary"),
        name="ffn_prep",
    )(w_gu, w_gu, w_down)


def _swiglu_rows(x, wg, wu, wd):
    g = _dot(x, wg)
    u = _dot(x, wu)
    return _dot((_silu(g) * u).astype(BF16), wd)


def _ffn_body(h_ref, wg_ref, wu_ref, wd_ref, o_ref):
    i = pl.program_id(0)
    j = pl.program_id(1)
    upt = FFN_ROW_TILE // ROWS_PER_UNIT
    full = (i + 1) * upt <= N_UNITS

    def run(n_units):
        @pl.when(j == 0)
        def _():
            o_ref[:n_units * ROWS_PER_UNIT, :] = jnp.zeros((n_units * ROWS_PER_UNIT, D_MODEL), F32)

        for r in range(n_units):
            rows = slice(r * ROWS_PER_UNIT, (r + 1) * ROWS_PER_UNIT)
            o_ref[rows, :] += _swiglu_rows(h_ref[rows, :], wg_ref[...], wu_ref[...], wd_ref[...])

    pl.when(full)(functools.partial(run, upt))
    pl.when(jnp.logical_not(full))(functools.partial(run, N_UNITS % upt))


def _ffn(h, wg, wu, wd):
    nj = D_FF_PAD // FFN_TILE
    return pl.pallas_call(
        _ffn_body,
        grid=(pl.cdiv(M_ROWS, FFN_ROW_TILE), nj),
        in_specs=[pl.BlockSpec((FFN_ROW_TILE, D_MODEL), lambda i, j: (i, 0)),
                  pl.BlockSpec((D_MODEL, FFN_TILE), lambda i, j: (0, j)),
                  pl.BlockSpec((D_MODEL, FFN_TILE), lambda i, j: (0, j)),
                  pl.BlockSpec((FFN_TILE, D_MODEL), lambda i, j: (j, 0))],
        out_specs=pl.BlockSpec((FFN_ROW_TILE, D_MODEL), lambda i, j: (i, 0)),
        out_shape=jax.ShapeDtypeStruct((M_ROWS, D_MODEL), F32),
        compiler_params=_params("arbitrary", "arbitrary"),
        name="ffn_dense",
    )(h, wg, wu, wd)


def _row_copy(src_hbm, row, dst, dst_row, sem):
    return pltpu.make_async_copy(src_hbm.at[pl.ds(row, 1), :], dst.at[pl.ds(dst_row, 1), :], sem)


ROW_DMA_UNROLL = 8


def _unit_copy(src, src_unit, dst, dst_unit, sem):
    rows = lambda u: pl.ds(pl.multiple_of(u * ROWS_PER_UNIT, ROWS_PER_UNIT), ROWS_PER_UNIT)
    return pltpu.make_async_copy(src.at[rows(src_unit), :], dst.at[rows(dst_unit), :], sem)


def _moe_dispatch_body(dest_ref, pad_lo_ref, pad_hi_ref, nu_ref, hp_ref, xs_hbm, zbuf, sem, zsem):
    i = pl.program_id(0)

    @pl.when(i == 0)
    def _():
        zbuf[...] = jnp.zeros_like(zbuf)
        zero_row = lambda r: _row_copy(zbuf, 0, xs_hbm, r, zsem)
        zero_unit = lambda u: _unit_copy(zbuf, 0, xs_hbm, u, zsem)

        def over_gaps(act):
            def row_body(r, c):
                act(zero_row(r))
                return c

            def unit_body(u, c):
                act(zero_unit(u))
                return c

            for e in range(N_EXPERTS):
                lax.fori_loop(pad_lo_ref[e], pad_hi_ref[e], row_body, 0)
            lax.fori_loop(nu_ref[0], MOE_UNITS, unit_body, 0)

        over_gaps(lambda cp: cp.start())
        over_gaps(lambda cp: cp.wait())

    base = i * (ROWS_PER_UNIT * TOP_K)

    def start(r, c):
        for k in range(TOP_K):
            _row_copy(hp_ref, r, xs_hbm, dest_ref[base + TOP_K * r + k], sem).start()
        return c

    def wait(r, c):
        for k in range(TOP_K):
            _row_copy(hp_ref, r, xs_hbm, 0, sem).wait()
        return c

    lax.fori_loop(0, ROWS_PER_UNIT, start, 0, unroll=ROW_DMA_UNROLL // TOP_K)
    lax.fori_loop(0, ROWS_PER_UNIT, wait, 0, unroll=ROW_DMA_UNROLL // TOP_K)


def _moe_dispatch(dest, pad_lo, pad_hi, total_units, h_packed):
    grid_spec = pltpu.PrefetchScalarGridSpec(
        num_scalar_prefetch=4,
        grid=(N_UNITS,),
        in_specs=[pl.BlockSpec((ROWS_PER_UNIT, PACKED_WIDTH), lambda i, *_: (i, 0))],
        out_specs=pl.BlockSpec(memory_space=pl.ANY),
        scratch_shapes=[pltpu.VMEM((ROWS_PER_UNIT, PACKED_WIDTH), jnp.uint32),
                        pltpu.SemaphoreType.DMA(()), pltpu.SemaphoreType.DMA(())],
    )
    return pl.pallas_call(
        _moe_dispatch_body,
        grid_spec=grid_spec,
        out_shape=jax.ShapeDtypeStruct((MOE_ROWS, PACKED_WIDTH), jnp.uint32),
        compiler_params=_params("arbitrary"),
        name="moe_dispatch",
    )(dest, pad_lo, pad_hi, total_units, h_packed)


def _moe_gmm_body(te_ref, tu_ref, tn_ref, nu_ref, xs_hbm, wg_ref, wu_ref, wd_ref, ys_hbm,
                  xsc, acc, wgu_b, wd_b, stage, sem_in, sem_out):
    t = pl.program_id(0)
    j = pl.program_id(1)
    nj = pl.num_programs(1)
    n = tn_ref[t]
    u0 = tu_ref[t]
    tf = MOE_FF_TILE

    @pl.when((j == 0) & (n > 0))
    def _():
        unit_in = lambda u, s: pltpu.make_async_copy(
            xs_hbm.at[pl.ds(pl.multiple_of((u0 + u) * ROWS_PER_UNIT, ROWS_PER_UNIT), ROWS_PER_UNIT), :],
            stage.at[s], sem_in.at[s])
        unit_in(0, 0).start()

        def load(u, c):
            s = u & 1

            @pl.when(u + 1 < n)
            def _():
                unit_in(u + 1, 1 - s).start()

            rows = pl.ds(pl.multiple_of(u * ROWS_PER_UNIT, ROWS_PER_UNIT), ROWS_PER_UNIT)
            acc[rows, :] = jnp.zeros((ROWS_PER_UNIT, D_MODEL), F32)
            unit_in(u, s).wait()
            lo, hi = _unpack_bf16_pairs(stage[s])
            xsc[rows, :PACKED_WIDTH] = lo
            xsc[rows, PACKED_WIDTH:] = hi
            return c

        lax.fori_loop(0, n, load, 0)

    @pl.when((j == 0) & (t == 0))
    def _():
        def start(u, c):
            _unit_copy(acc, 0, ys_hbm, u, sem_out).start()
            return c

        def wait(u, c):
            _unit_copy(acc, 0, ys_hbm, u, sem_out).wait()
            return c

        lax.fori_loop(nu_ref[0], MOE_UNITS, start, 0)
        lax.fori_loop(nu_ref[0], MOE_UNITS, wait, 0)

    @pl.when(n > 0)
    def _():
        wgu_b[:, :tf] = wg_ref[...].astype(BF16)
        wgu_b[:, tf:] = wu_ref[...].astype(BF16)
        wd_b[...] = wd_ref[...].astype(BF16)

        def unit_rows(r0):
            rs = pl.ds(r0, ROWS_PER_UNIT)
            gu = _dot(xsc[rs, :], wgu_b[...])
            a = (_silu(gu[:, :tf]) * gu[:, tf:]).astype(BF16)
            acc[rs, :] += _dot(a, wd_b[...])

        def units_from(u_first, k):
            for h in range(k):
                unit_rows(pl.multiple_of((u_first + h) * ROWS_PER_UNIT, ROWS_PER_UNIT))

        for b in reversed(range(MOE_TILE_UNITS.bit_length())):
            size = 1 << b
            done = lax.shift_left(lax.shift_right_logical(n, b + 1), b + 1)
            pl.when((n & size) != 0)(functools.partial(units_from, done, size))

    @pl.when((j == nj - 1) & (n > 0))
    def _():
        def start(u, c):
            _unit_copy(acc, u, ys_hbm, u0 + u, sem_out).start()
            return c

        def wait(u, c):
            _unit_copy(acc, u, ys_hbm, u0 + u, sem_out).wait()
            return c

        lax.fori_loop(0, n, start, 0)
        lax.fori_loop(0, n, wait, 0)


def _moe_gmm(tile_expert, tile_unit0, tile_nunits, total_units, xs, w_gu, w_down, lm):
    nj = D_FF_EXPERT // MOE_FF_TILE
    jj = lambda t, j, tn: jnp.where(tn[t] > 0, j, nj - 1)
    tile_rows = MOE_TILE_UNITS * ROWS_PER_UNIT
    grid_spec = pltpu.PrefetchScalarGridSpec(
        num_scalar_prefetch=4,
        grid=(MOE_TILES, nj),
        in_specs=[pl.BlockSpec(memory_space=pl.ANY),
                  pl.BlockSpec((None, None, D_MODEL, MOE_FF_TILE),
                               lambda t, j, te, tu, tn, nu: (lm, te[t], 0, jj(t, j, tn))),
                  pl.BlockSpec((None, None, D_MODEL, MOE_FF_TILE),
                               lambda t, j, te, tu, tn, nu: (lm, te[t], 0, nj + jj(t, j, tn))),
                  pl.BlockSpec((None, None, MOE_FF_TILE, D_MODEL),
                               lambda t, j, te, tu, tn, nu: (lm, te[t], jj(t, j, tn), 0))],
        out_specs=pl.BlockSpec(memory_space=pl.ANY),
        scratch_shapes=[pltpu.VMEM((tile_rows, D_MODEL), BF16), pltpu.VMEM((tile_rows, D_MODEL), F32),
                        pltpu.VMEM((D_MODEL, 2 * MOE_FF_TILE), BF16), pltpu.VMEM((MOE_FF_TILE, D_MODEL), BF16),
                        pltpu.VMEM((2, ROWS_PER_UNIT, PACKED_WIDTH), jnp.uint32),
                        pltpu.SemaphoreType.DMA((2,)), pltpu.SemaphoreType.DMA(())],
    )
    return pl.pallas_call(
        _moe_gmm_body,
        grid_spec=grid_spec,
        out_shape=jax.ShapeDtypeStruct((MOE_ROWS, D_MODEL), F32),
        compiler_params=_params("arbitrary", "arbitrary"),
        name="moe_gmm",
    )(tile_expert, tile_unit0, tile_nunits, total_units, xs, w_gu, w_gu, w_down)


def _moe_combine_body(dest_ref, ys_hbm, rw_ref, x_ref, mpg_ref, mpn_ref, gs_ref, shs_ref, scs_ref, lg_ref, lb_ref,
                      xo_ref, ho_ref, buf, sem, *, pg, psh, psc):
    i = pl.program_id(0)
    slot = i & 1

    def issue(unit, s):
        base = unit * (ROWS_PER_UNIT * TOP_K)

        def start(r, c):
            for k in range(TOP_K):
                _row_copy(ys_hbm, dest_ref[base + TOP_K * r + k], buf.at[s, k], r, sem.at[s]).start()
            return c

        lax.fori_loop(0, ROWS_PER_UNIT, start, 0, unroll=ROW_DMA_UNROLL // TOP_K)

    @pl.when(i == 0)
    def _():
        issue(0, 0)

    @pl.when(i + 1 < pl.num_programs(0))
    def _():
        issue(i + 1, 1 - slot)

    def wait(r, c):
        for k in range(TOP_K):
            _row_copy(ys_hbm, 0, buf.at[slot, k], r, sem.at[slot]).wait()
        return c

    lax.fori_loop(0, ROWS_PER_UNIT, wait, 0, unroll=ROW_DMA_UNROLL // TOP_K)
    y_of = lambda prompt, rs: rw_ref[rs, 0:1] * buf[slot, 0, rs, :] + rw_ref[rs, 1:2] * buf[slot, 1, rs, :]
    _finish_dispatch(i, x_ref, y_of, pg, psh, psc, mpg_ref, mpn_ref, gs_ref, shs_ref, scs_ref,
                     lg_ref, lb_ref, xo_ref, ho_ref, None)


def _moe_combine(dest, ys, route_w, x, modp, mods, ln_g, ln_b, layer, next_layer, pg, psh, psc):
    unit = lambda w: pl.BlockSpec((ROWS_PER_UNIT, w), lambda i, d: (i, 0))
    grid_spec = pltpu.PrefetchScalarGridSpec(
        num_scalar_prefetch=1,
        grid=(N_UNITS,),
        in_specs=[pl.BlockSpec(memory_space=pl.ANY), unit(TOP_K), unit(D_MODEL)]
        + _epilogue_specs(layer, next_layer, pg, psh, psc),
        out_specs=[unit(D_MODEL), unit(D_MODEL)],
        scratch_shapes=[pltpu.VMEM((2, TOP_K, ROWS_PER_UNIT, D_MODEL), F32), pltpu.SemaphoreType.DMA((2,))],
    )
    return pl.pallas_call(
        functools.partial(_moe_combine_body, pg=pg, psh=psh, psc=psc),
        grid_spec=grid_spec,
        out_shape=[jax.ShapeDtypeStruct((M_ROWS, D_MODEL), F32), jax.ShapeDtypeStruct((M_ROWS, D_MODEL), BF16)],
        compiler_params=_params("arbitrary"),
        name="moe_combine",
    )(dest, ys, route_w, x, modp, modp, mods, mods, mods, ln_g.reshape(1, D_MODEL), ln_b.reshape(1, D_MODEL))


def _moe_plan(route_i):
    i32 = jnp.int32
    e_flat = route_i.reshape(-1)
    onehot = (e_flat[:, None] == jnp.arange(N_EXPERTS, dtype=i32)[None, :]).astype(i32)
    csum = jnp.cumsum(onehot, axis=0)
    rank = jnp.sum(csum * onehot, axis=1) - 1
    counts = csum[-1]
    units_e = (counts + ROWS_PER_UNIT - 1) // ROWS_PER_UNIT
    unit_end = jnp.cumsum(units_e)
    unit_start = unit_end - units_e
    dest = (unit_start[e_flat] * ROWS_PER_UNIT + rank).astype(i32)
    pad_lo = (unit_start * ROWS_PER_UNIT + counts).astype(i32)
    pad_hi = (unit_end * ROWS_PER_UNIT).astype(i32)
    total_units = unit_end[-1:].astype(i32)
    tiles_e = (units_e + MOE_TILE_UNITS - 1) // MOE_TILE_UNITS
    tile_end = jnp.cumsum(tiles_e)
    tile_start = tile_end - tiles_e
    n_tiles = tile_end[-1]
    t_ids = jnp.arange(MOE_TILES, dtype=i32)
    t_eff = jnp.minimum(t_ids, n_tiles - 1)
    tile_expert = jnp.sum((t_eff[:, None] >= tile_end[None, :]).astype(i32), axis=1)
    k = t_eff - tile_start[tile_expert]
    tile_unit0 = unit_start[tile_expert] + k * MOE_TILE_UNITS
    units_left = units_e[tile_expert] - k * MOE_TILE_UNITS
    tile_nunits = jnp.where(t_ids < n_tiles, jnp.minimum(units_left, MOE_TILE_UNITS), 0)
    return (dest, pad_lo, pad_hi, total_units, tile_expert.astype(i32), tile_unit0.astype(i32),
            tile_nunits.astype(i32))


def kernel(x_prompt, x_sample, cache_win_k, cache_win_v, c_prompt, c_sample, w_ada, b_ada, w_in, b_in, v_norm_g, v_norm_b, w_spatial, b_spatial, attn_sinks, w_out, b_out, ln1_g, ln1_b, ln2_g, ln2_b, w_ffn_gu, w_ffn_down, w_router, w_exp_gu, w_exp_down):
    c_rows = jnp.concatenate([c_prompt, c_sample, jnp.zeros((ADA_ROWS - BATCH - DEC_BATCH, D_MODEL), F32)], axis=0)
    mod = _adaln(c_rows, w_ada, b_ada)
    modp = mod[:, :BATCH].reshape(DEPTH, BATCH, 6, D_MODEL)
    mods = jnp.repeat(mod[:, BATCH:BATCH + DEC_BATCH], DEC_SEQ, axis=1)

    x = jnp.concatenate([x_prompt.reshape(M_PROMPT, D_MODEL), x_sample.reshape(M_SAMPLE, D_MODEL)], axis=0)
    h = _modulate0(x, modp, mods)

    rows = GQA * DEC_SEQ
    kp_list, vp_list, ks_list, vs_list, cv_list = [], [], [], [], []
    for l in range(DEPTH):
        nl = min(l + 1, DEPTH - 1)
        u, v, q, kv = _proj(h, w_in, l, b_in[l], v_norm_g[l], v_norm_b[l])
        b_s_t = b_spatial[l].T
        mix_p = _mix_prompt(attn_sinks[l].reshape(-1), u, v, q, kv, w_spatial[l], b_s_t)

        u_s = u[M_PROMPT:].reshape(DEC_BATCH, DEC_SEQ, A_WIDTH)
        v_s = v[M_PROMPT:].reshape(DEC_BATCH, DEC_SEQ, A_WIDTH)
        kv_s = kv[M_PROMPT:].reshape(DEC_BATCH, DEC_SEQ, 2 * KV_WIDTH)
        q_s = q[M_PROMPT:].reshape(DEC_BATCH, DEC_SEQ, N_KV_HEADS, GQA, HEAD_DIM)
        q_s = q_s.transpose(0, 2, 3, 1, 4).reshape(DEC_BATCH, N_KV_HEADS, rows, HEAD_DIM)
        ck = cache_win_k[l].reshape(DEC_BATCH, WINDOW, KV_WIDTH)
        cv = cache_win_v[l].reshape(DEC_BATCH, WINDOW, KV_WIDTH)
        sink_rows = jnp.repeat(attn_sinks[l], DEC_SEQ, axis=1).reshape(N_KV_HEADS, rows, 1)
        a_s, o_s = _mix_sample(u_s, v_s, q_s, kv_s, ck, cv, w_spatial[l], b_s_t, sink_rows)
        o_s = o_s.reshape(DEC_BATCH, N_KV_HEADS, GQA, DEC_SEQ, HEAD_DIM).transpose(0, 3, 1, 2, 4)
        mix_s = jnp.concatenate([a_s.reshape(M_SAMPLE, A_WIDTH), o_s.reshape(M_SAMPLE, B_WIDTH)], axis=1)

        y = (mix_p, mix_s, w_out, b_out[l])
        if l % 2 == 0:
            x, h = _epilogue(x, y, modp, mods, ln1_g[l], ln1_b[l], l, l, P_GATE1, P_SHIFT2, P_SCALE2)
            ld = l // 2
            wg, wu, wd = _ffn_prep(w_ffn_gu, w_ffn_down, ld)
            f = _ffn(h, wg, wu, wd)
            x, h = _epilogue(x, f, modp, mods, ln2_g[l], ln2_b[l], l, nl, P_GATE2, P_SHIFT1, P_SCALE1)
        else:
            lm = l // 2
            x, h, h_packed, route_i, route_w = _epilogue(x, y, modp, mods, ln1_g[l], ln1_b[l], l, l,
                                                      P_GATE1, P_SHIFT2, P_SCALE2, w_router_t=w_router[lm].T)
            dest, pad_lo, pad_hi, total_units, tile_expert, tile_unit0, tile_nunits = _moe_plan(route_i)
            xs = _moe_dispatch(dest, pad_lo, pad_hi, total_units, h_packed)
            ys = _moe_gmm(tile_expert, tile_unit0, tile_nunits, total_units, xs, w_exp_gu, w_exp_down, lm)
            x, h = _moe_combine(dest, ys, route_w, x, modp, mods, ln2_g[l], ln2_b[l], l, nl,
                                P_GATE2, P_SHIFT1, P_SCALE1)

        kv_p = kv[:M_PROMPT].reshape(BATCH, SEQ, 2 * KV_WIDTH)[:, SEQ - WINDOW:]
        kp_list.append(kv_p[..., :KV_WIDTH].reshape(BATCH, WINDOW, N_KV_HEADS, HEAD_DIM))
        vp_list.append(kv_p[..., KV_WIDTH:].reshape(BATCH, WINDOW, N_KV_HEADS, HEAD_DIM))
        k_new = kv_s[..., :KV_WIDTH].reshape(DEC_BATCH, DEC_SEQ, N_KV_HEADS, HEAD_DIM)
        v_new = kv_s[..., KV_WIDTH:].reshape(DEC_BATCH, DEC_SEQ, N_KV_HEADS, HEAD_DIM)
        ks_list.append(jnp.concatenate([cache_win_k[l][:, DEC_SEQ:], k_new], axis=1))
        vs_list.append(jnp.concatenate([cache_win_v[l][:, DEC_SEQ:], v_new], axis=1))
        cv_list.append(v_s.astype(F32))

    y_prompt = x[:M_PROMPT].reshape(BATCH, SEQ, D_MODEL)
    y_sample = x[M_PROMPT:].reshape(DEC_BATCH, DEC_SEQ, D_MODEL)
    return (y_prompt, y_sample, jnp.stack(kp_list), jnp.stack(vp_list), jnp.stack(ks_list), jnp.stack(vs_list),
            jnp.stack(cv_list))
```

```python
import functools

import jax
import jax.numpy as jnp
from jax import lax
from jax.experimental import pallas as pl
from jax.experimental.pallas import tpu as pltpu

F32 = jnp.float32
BF16 = jnp.bfloat16

D_MODEL = 2048
BATCH = 4
SEQ = 2048
DEPTH = 4
DEC_BATCH = 32
DEC_SEQ = 8
CHUNK = 128
A_WIDTH = D_MODEL // 2
A_GROUPS = 4
A_GROUP_DIM = A_WIDTH // A_GROUPS
HEAD_DIM = 64
B_WIDTH = D_MODEL - A_WIDTH
N_HEADS = B_WIDTH // HEAD_DIM
N_KV_HEADS = 2
GQA = N_HEADS // N_KV_HEADS
KV_WIDTH = N_KV_HEADS * HEAD_DIM
WINDOW = 128
IN_COLS = 2 * A_WIDTH + B_WIDTH + 2 * KV_WIDTH
D_FF = 5504
N_EXPERTS = 8
TOP_K = 2
D_FF_EXPERT = 7168
ALPHA = (2.0 * DEPTH) ** 0.25
LN_EPS = 1e-5
NEG_INF = -1e30

M_PROMPT = BATCH * SEQ
M_SAMPLE = DEC_BATCH * DEC_SEQ
M_ROWS = M_PROMPT + M_SAMPLE
ROWS_PER_UNIT = 256
N_PROMPT_UNITS = M_PROMPT // ROWS_PER_UNIT
N_UNITS = M_ROWS // ROWS_PER_UNIT
UNITS_PER_SEQ = SEQ // ROWS_PER_UNIT
ADA_ROWS = 40
P_SHIFT1, P_SCALE1, P_GATE1, P_SHIFT2, P_SCALE2, P_GATE2 = range(6)

D_FF_PAD = 5632
FFN_TILE = 512
FFN_ROW_TILE = 4 * ROWS_PER_UNIT
MOE_FF_TILE = 256
N_PAIRS = M_ROWS * TOP_K
MOE_UNITS = N_PAIRS // ROWS_PER_UNIT + N_EXPERTS
MOE_ROWS = MOE_UNITS * ROWS_PER_UNIT
MOE_TILE_UNITS = 9
MOE_TILES =MOE_UNITS // MOE_TILE_UNITS + N_EXPERTS

VMEM_LIMIT = 56 * 1024 * 1024


def _params(*sem):
    return pltpu.CompilerParams(dimension_semantics=sem, vmem_limit_bytes=VMEM_LIMIT)


def _dot(a, b):
    return jnp.dot(a, b, preferred_element_type=F32)


def _layer_norm(z, g, b):
    mu = jnp.mean(z, axis=-1, keepdims=True)
    zc = z - mu
    var = jnp.mean(zc * zc, axis=-1, keepdims=True)
    return zc * lax.rsqrt(var + LN_EPS) * g + b


def _gelu_tanh(x):
    return x * (0.5 * (1.0 + jnp.tanh(0.7978845608028654 * (x + 0.044715 * (x * x * x)))))


def _silu(x):
    return x * jax.nn.sigmoid(x)


def _prompt_batch_of_unit(i):
    return jnp.minimum(i // UNITS_PER_SEQ, BATCH - 1)


def _adaln_body(c_ref, w_ref, b_ref, o_ref):
    s = _silu(c_ref[...]).astype(BF16)
    o_ref[...] = _dot(s, w_ref[...].astype(BF16)) + b_ref[...]


def _adaln(c_rows, w_ada, b_ada):
    tn = 1024
    return pl.pallas_call(
        _adaln_body,
        grid=(DEPTH, 6 * D_MODEL // tn),
        in_specs=[pl.BlockSpec((ADA_ROWS, D_MODEL), lambda l, n: (0, 0)),
                  pl.BlockSpec((None, D_MODEL, tn), lambda l, n: (l, 0, n)),
                  pl.BlockSpec((None, 1, tn), lambda l, n: (l, 0, n))],
        out_specs=pl.BlockSpec((None, ADA_ROWS, tn), lambda l, n: (l, 0, n)),
        out_shape=jax.ShapeDtypeStruct((DEPTH, ADA_ROWS, 6 * D_MODEL), F32),
        compiler_params=_params("arbitrary", "arbitrary"),
        name="adaln",
    )(c_rows, w_ada, b_ada.reshape(DEPTH, 1, 6 * D_MODEL))


def _mod0_body(xp_ref, xs_ref, mp_ref, ss_ref, cs_ref, h_ref, x_ref):
    i = pl.program_id(0)

    @pl.when(i < N_PROMPT_UNITS)
    def _():
        x = xp_ref[...]
        x_ref[...] = x
        h_ref[...] = (x * (1 + mp_ref[P_SCALE1:P_SCALE1 + 1, :]) + mp_ref[P_SHIFT1:P_SHIFT1 + 1, :]).astype(BF16)

    @pl.when(i >= N_PROMPT_UNITS)
    def _():
        x = xs_ref[...]
        x_ref[...] = x
        h_ref[...] = (x * (1 + cs_ref[...]) + ss_ref[...]).astype(BF16)


def _unit_spec(width):
    return pl.BlockSpec((ROWS_PER_UNIT, width), lambda i: (i, 0))


def _modp_spec(layer):
    return pl.BlockSpec((None, None, 6, D_MODEL), lambda i: (layer, _prompt_batch_of_unit(i), 0, 0))


def _mods_spec(layer, p):
    return pl.BlockSpec((None, ROWS_PER_UNIT, D_MODEL), lambda i, *_: (layer, 0, p))


def _prompt_unit_spec(width):
    return pl.BlockSpec((ROWS_PER_UNIT, width), lambda i, *_: (jnp.minimum(i, N_PROMPT_UNITS - 1), 0))


def _sample_rows_spec(width):
    return pl.BlockSpec((M_SAMPLE, width), lambda i, *_: (0, 0))


def _modulate0(x_prompt, x_sample, modp, mods):
    return pl.pallas_call(
        _mod0_body,
        grid=(N_UNITS,),
        in_specs=[_prompt_unit_spec(D_MODEL), _sample_rows_spec(D_MODEL), _modp_spec(0),
                  _mods_spec(0, P_SHIFT1), _mods_spec(0, P_SCALE1)],
        out_specs=[_unit_spec(D_MODEL), _unit_spec(D_MODEL)],
        out_shape=[jax.ShapeDtypeStruct((M_ROWS, D_MODEL), BF16), jax.ShapeDtypeStruct((M_ROWS, D_MODEL), F32)],
        compiler_params=_params("arbitrary"),
        name="modulate0",
    )(x_prompt, x_sample, modp, mods, mods)


def _cast_weight_once(w32_ref, w_ref, chunk):
    @pl.when(pl.program_id(0) == 0)
    def _():
        for c in range(0, w32_ref.shape[1], chunk):
            w_ref[:, c:c + chunk] = w32_ref[:, c:c + chunk].astype(BF16)


def _resident(layer, shape):
    return pl.BlockSpec((None,) + shape, lambda i: (layer,) + (0,) * len(shape), pipeline_mode=pl.Buffered(1))


def _proj_body(h_ref, w32_ref, b_ref, gv_ref, bv_ref, u_ref, v_ref, q_ref, kv_ref, w_ref):
    _cast_weight_once(w32_ref, w_ref, A_GROUP_DIM)
    h = h_ref[...]
    gd = A_GROUP_DIM
    for c in range(A_GROUPS):
        sl = slice(c * gd, (c + 1) * gd)
        z = _dot(h, w_ref[:, sl]) + b_ref[:, sl]
        u_ref[:, sl] = _gelu_tanh(z).astype(BF16)
    for g in range(A_GROUPS):
        sl = slice(A_WIDTH + g * gd, A_WIDTH + (g + 1) * gd)
        z = _gelu_tanh(_dot(h, w_ref[:, sl]) + b_ref[:, sl])
        vn = _layer_norm(z, gv_ref[g:g + 1, :], bv_ref[g:g + 1, :])
        v_ref[:, g * gd:(g + 1) * gd] = vn.astype(BF16)
    o1 = 2 * A_WIDTH
    o2 = o1 + B_WIDTH
    z = _dot(h, w_ref[:, o1:o2]) + b_ref[:, o1:o2]
    q_ref[...] = (z * (HEAD_DIM ** -0.5)).astype(BF16)
    kv_ref[...] = _dot(h, w_ref[:, o2:]) + b_ref[:, o2:]


def _proj(h, w_all, layer, b, gv, bv):
    const = lambda shape: pl.BlockSpec(shape, lambda i: (0,) * len(shape))
    return pl.pallas_call(
        _proj_body,
        grid=(N_UNITS,),
        in_specs=[_unit_spec(D_MODEL), _resident(layer, (D_MODEL, IN_COLS)), const((1, IN_COLS)),
                  const((A_GROUPS, A_GROUP_DIM)), const((A_GROUPS, A_GROUP_DIM))],
        out_specs=[_unit_spec(A_WIDTH), _unit_spec(A_WIDTH), _unit_spec(B_WIDTH), _unit_spec(2 * KV_WIDTH)],
        out_shape=[jax.ShapeDtypeStruct((M_ROWS, A_WIDTH), BF16), jax.ShapeDtypeStruct((M_ROWS, A_WIDTH), BF16),
                   jax.ShapeDtypeStruct((M_ROWS, B_WIDTH), BF16), jax.ShapeDtypeStruct((M_ROWS, 2 * KV_WIDTH), F32)],
        scratch_shapes=[pltpu.VMEM((D_MODEL, IN_COLS), BF16)],
        compiler_params=_params("arbitrary"),
        name="proj",
    )(h, w_all, b.reshape(1, IN_COLS), gv, bv)


def _mix_prompt_body(sink_ref, u_ref, v_ref, q_ref, kvc_ref, kvp_ref, ws_ref, bs_ref, o_ref):
    blk = pl.program_id(1)
    gd = A_GROUP_DIM
    tr = lax.broadcasted_iota(jnp.int32, (CHUNK, CHUNK), 0)
    tc = lax.broadcasted_iota(jnp.int32, (CHUNK, CHUNK), 1)
    for g in range(A_GROUPS):
        sl = slice(g * gd, (g + 1) * gd)
        w = jnp.where(tr >= tc, ws_ref[g], 0.0).astype(BF16)
        s = _dot(w, v_ref[:, sl]) + bs_ref[:, g:g + 1]
        o_ref[:, sl] = (u_ref[:, sl].astype(F32) * s).astype(BF16)

    kvc = kvc_ref[...]
    kvp = kvp_ref[...]
    row = lax.broadcasted_iota(jnp.int32, (WINDOW, 2 * WINDOW), 0)
    col = lax.broadcasted_iota(jnp.int32, (WINDOW, 2 * WINDOW), 1)
    dlt = row + WINDOW - col
    mask = (dlt >= 0) & (dlt <= WINDOW) & ((col >= WINDOW) | (blk > 0))
    hd = HEAD_DIM
    for j in range(N_KV_HEADS):
        ks = slice(j * hd, (j + 1) * hd)
        vs = slice(KV_WIDTH + j * hd, KV_WIDTH + (j + 1) * hd)
        kk = jnp.concatenate([kvp[:, ks], kvc[:, ks]], axis=0).astype(BF16)
        vv = jnp.concatenate([kvp[:, vs], kvc[:, vs]], axis=0).astype(BF16)
        for hh in range(GQA):
            h = j * GQA + hh
            qh = q_ref[:, h * hd:(h + 1) * hd]
            s = lax.dot_general(qh, kk, (((1,), (1,)), ((), ())), preferred_element_type=F32)
            s = jnp.where(mask, s, NEG_INF)
            sk = sink_ref[h]
            m = jnp.maximum(jnp.max(s, axis=-1, keepdims=True), sk)
            p = jnp.exp(s - m)
            den = jnp.sum(p, axis=-1, keepdims=True) + jnp.exp(sk - m)
            o = _dot(p.astype(BF16), vv) / den
            o_ref[:, A_WIDTH + h * hd:A_WIDTH + (h + 1) * hd] = o.astype(BF16)


def _mix_prompt(sinks, u, v, q, kv, w_s, b_s_t):
    nb = SEQ // WINDOW
    row_blk = lambda w: pl.BlockSpec((WINDOW, w), lambda b, i, s: (b * nb + i, 0))
    grid_spec = pltpu.PrefetchScalarGridSpec(
        num_scalar_prefetch=1,
        grid=(BATCH, nb),
        in_specs=[row_blk(A_WIDTH), row_blk(A_WIDTH), row_blk(B_WIDTH), row_blk(2 * KV_WIDTH),
                  pl.BlockSpec((WINDOW, 2 * KV_WIDTH), lambda b, i, s: (b * nb + jnp.maximum(i - 1, 0), 0)),
                  pl.BlockSpec((A_GROUPS, CHUNK, CHUNK), lambda b, i, s: (0, 0, 0)),
                  pl.BlockSpec((CHUNK, A_GROUPS), lambda b, i, s: (0, 0))],
        out_specs=pl.BlockSpec((WINDOW, D_MODEL), lambda b, i, s: (b * nb + i, 0)),
    )
    return pl.pallas_call(
        _mix_prompt_body,
        grid_spec=grid_spec,
        out_shape=jax.ShapeDtypeStruct((M_PROMPT, D_MODEL), BF16),
        compiler_params=_params("arbitrary", "arbitrary"),
        name="mix_prompt",
    )(sinks, u, v, q, kv, kv, w_s, b_s_t)


SAMPLE_KEYS = 2 * WINDOW


SAMPLE_BATCH_BLOCK = 8


def _mix_sample_body(u_ref, v_ref, q_ref, kn_ref, ck_ref, cv_ref, ws_ref, bs_ref, sk_ref, a_ref, o_ref):
    for b in range(SAMPLE_BATCH_BLOCK):
        _mix_sample_one(u_ref.at[b], v_ref.at[b], q_ref.at[b], kn_ref.at[b], ck_ref.at[b], cv_ref.at[b],
                        ws_ref, bs_ref, sk_ref, a_ref.at[b], o_ref.at[b])


def _mix_sample_one(u_ref, v_ref, q_ref, kn_ref, ck_ref, cv_ref, ws_ref, bs_ref, sk_ref, a_ref, o_ref):
    gd = A_GROUP_DIM
    ds = DEC_SEQ
    v = v_ref[...].astype(F32)
    u = u_ref[...].astype(F32)
    tr = lax.broadcasted_iota(jnp.int32, (ds, ds), 0)
    tc = lax.broadcasted_iota(jnp.int32, (ds, ds), 1)
    for g in range(A_GROUPS):
        sl = slice(g * gd, (g + 1) * gd)
        w = jnp.where(tr >= tc, ws_ref[g][:ds, :ds], 0.0)
        s = jnp.zeros((ds, gd), F32) + bs_ref[:ds, g:g + 1]
        for t in range(ds):
            s = s + w[:, t:t + 1] * v[t:t + 1, sl]
        a_ref[:, sl] = (u[:, sl] * s).astype(BF16)

    kn = kn_ref[...]
    ck = ck_ref[...]
    cv = cv_ref[...]
    hd = HEAD_DIM
    rows = GQA * ds
    qt = lax.broadcasted_iota(jnp.int32, (rows, SAMPLE_KEYS), 0) & (ds - 1)
    col = lax.broadcasted_iota(jnp.int32, (rows, SAMPLE_KEYS), 1)
    mask = ((col < WINDOW) & (col >= qt)) | ((col >= WINDOW) & (col - WINDOW <= qt))
    pad = jnp.zeros((SAMPLE_KEYS - WINDOW - ds, hd), F32)
    for j in range(N_KV_HEADS):
        ks = slice(j * hd, (j + 1) * hd)
        vs = slice(KV_WIDTH + j * hd, KV_WIDTH + (j + 1) * hd)
        kk = jnp.concatenate([ck[:, ks], kn[:, ks], pad], axis=0).astype(BF16)
        vv = jnp.concatenate([cv[:, ks], kn[:, vs], pad], axis=0).astype(BF16)
        s = lax.dot_general(q_ref[j], kk, (((1,), (1,)), ((), ())), preferred_element_type=F32)
        s = jnp.where(mask, s, NEG_INF)
        sk = sk_ref[j]
        m = jnp.maximum(jnp.max(s, axis=-1, keepdims=True), sk)
        p = jnp.exp(s - m)
        den = jnp.sum(p, axis=-1, keepdims=True) + jnp.exp(sk - m)
        o_ref[j] = (_dot(p.astype(BF16), vv) / den).astype(BF16)


def _mix_sample(u_s, v_s, q_s, kv_s, cache_k, cache_v, w_s, b_s_t, sink_rows):
    rows = GQA * DEC_SEQ
    per_b = lambda *shape: pl.BlockSpec((SAMPLE_BATCH_BLOCK,) + shape, lambda b: (b,) + (0,) * len(shape))
    const = lambda *shape: pl.BlockSpec(shape, lambda b: (0,) * len(shape))
    return pl.pallas_call(
        _mix_sample_body,
        grid=(DEC_BATCH // SAMPLE_BATCH_BLOCK,),
        in_specs=[per_b(DEC_SEQ, A_WIDTH), per_b(DEC_SEQ, A_WIDTH), per_b(N_KV_HEADS, rows, HEAD_DIM),
                  per_b(DEC_SEQ, 2 * KV_WIDTH), per_b(WINDOW, KV_WIDTH), per_b(WINDOW, KV_WIDTH),
                  const(A_GROUPS, CHUNK, CHUNK), const(CHUNK, A_GROUPS), const(N_KV_HEADS, rows, 1)],
        out_specs=[per_b(DEC_SEQ, A_WIDTH), per_b(N_KV_HEADS, rows, HEAD_DIM)],
        out_shape=[jax.ShapeDtypeStruct((DEC_BATCH, DEC_SEQ, A_WIDTH), BF16),
                   jax.ShapeDtypeStruct((DEC_BATCH, N_KV_HEADS, rows, HEAD_DIM), BF16)],
        compiler_params=_params("arbitrary"),
        name="mix_sample",
    )(u_s, v_s, q_s, kv_s, cache_k, cache_v, w_s, b_s_t, sink_rows)


PACKED_WIDTH = D_MODEL // 2


def _pack_bf16_pairs(h):
    bits = lambda v: lax.bitcast_convert_type(v.astype(BF16).astype(F32), jnp.uint32)
    lo = lax.shift_right_logical(bits(h[:, :PACKED_WIDTH]), jnp.uint32(16))
    hi = bits(h[:, PACKED_WIDTH:]) & jnp.uint32(0xFFFF0000)
    return lo | hi


def _unpack_bf16_pairs(w):
    lo = lax.bitcast_convert_type(lax.shift_left(w, jnp.uint32(16)), F32).astype(BF16)
    hi = lax.bitcast_convert_type(w & jnp.uint32(0xFFFF0000), F32).astype(BF16)
    return lo, hi


def _route_top2(h, rs, wrt_ref, ri_ref, rw_ref):
    logits = [jnp.sum(h * wrt_ref[e:e + 1, :], axis=-1, keepdims=True) for e in range(N_EXPERTS)]
    m1 = logits[0]
    i1 = jnp.zeros_like(m1, dtype=jnp.int32)
    for e in range(1, N_EXPERTS):
        gt = logits[e] > m1
        m1 = jnp.where(gt, logits[e], m1)
        i1 = jnp.where(gt, e, i1)
    m2 = jnp.full_like(m1, -jnp.inf)
    i2 = jnp.zeros_like(i1)
    for e in range(N_EXPERTS):
        ok = (i1 != e) & (logits[e] > m2)
        m2 = jnp.where(ok, logits[e], m2)
        i2 = jnp.where(ok, e, i2)
    w1 = 1.0 / (1.0 + jnp.exp(m2 - m1))
    ri_ref[rs, 0:1] = i1
    ri_ref[rs, 1:2] = i2
    rw_ref[rs, 0:1] = w1
    rw_ref[rs, 1:2] = 1.0 - w1


FINISH_SPLIT = 2


def _finish_rows(rs, x, y, gate, shift, scale, lg_ref, lb_ref, xo_ref, ho_ref, route_refs):
    xn = _layer_norm(ALPHA * x + (1 + gate) * y, lg_ref[...], lb_ref[...])
    xo_ref[rs, :] = xn
    if ho_ref is None:
        return
    h = xn * (1 + scale) + shift
    ho_ref[rs, :] = h.astype(BF16)
    if route_refs is not None:
        wrt_ref, hp_ref, ri_ref, rw_ref = route_refs
        hp_ref[rs, :] = _pack_bf16_pairs(h)
        _route_top2(h, rs, wrt_ref, ri_ref, rw_ref)


def _finish_dispatch(i, x_ref, y_of, pg, psh, psc, mpg_ref, mpn_ref, gs_ref, shs_ref, scs_ref,
                     lg_ref, lb_ref, xo_ref, ho_ref, route_refs):
    n = ROWS_PER_UNIT // FINISH_SPLIT
    slices = [slice(k * n, (k + 1) * n) for k in range(FINISH_SPLIT)]
    xo_prompt, xo_sample = xo_ref if isinstance(xo_ref, tuple) else (xo_ref, xo_ref)

    @pl.when(i < N_PROMPT_UNITS)
    def _():
        for rs in slices:
            _finish_rows(rs, x_ref[rs, :], y_of(True, rs), mpg_ref[pg:pg + 1, :], mpn_ref[psh:psh + 1, :],
                         mpn_ref[psc:psc + 1, :], lg_ref, lb_ref, xo_prompt, ho_ref, route_refs)

    @pl.when(i >= N_PROMPT_UNITS)
    def _():
        for rs in slices:
            _finish_rows(rs, x_ref[rs, :], y_of(False, rs), gs_ref[rs, :], shs_ref[rs, :], scs_ref[rs, :],
                         lg_ref, lb_ref, xo_sample, ho_ref, route_refs)


def _epilogue_body(*refs, pg, psh, psc, route, project):
    refs = list(refs)
    if project:
        mp_ref, ms_ref, w32_ref, b_ref = refs[:4]
        w_ref = refs.pop()
        refs = refs[4:]
        _cast_weight_once(w32_ref, w_ref, ROWS_PER_UNIT)
        y_of = lambda prompt, rs: _dot((mp_ref if prompt else ms_ref)[rs, :], w_ref[...]) + b_ref[...]
    else:
        y_ref = refs.pop(0)
        y_of = lambda prompt, rs: y_ref[rs, :]
    x_ref, mpg_ref, mpn_ref, gs_ref, shs_ref, scs_ref, lg_ref, lb_ref = refs[:8]
    if route:
        wrt_ref, xo_ref, ho_ref, hp_ref, ri_ref, rw_ref = refs[8:]
        route_refs = (wrt_ref, hp_ref, ri_ref, rw_ref)
    else:
        xo_ref, ho_ref = refs[8:]
        route_refs = None
    _finish_dispatch(pl.program_id(0), x_ref, y_of, pg, psh, psc, mpg_ref, mpn_ref,
                     gs_ref, shs_ref, scs_ref, lg_ref, lb_ref, xo_ref, ho_ref, route_refs)


def _epilogue_specs(layer, next_layer, pg, psh, psc):
    row = pl.BlockSpec((1, D_MODEL), lambda i, *_: (0, 0))
    modp = lambda l: pl.BlockSpec((None, None, 6, D_MODEL), lambda i, *_: (l, _prompt_batch_of_unit(i), 0, 0))
    return [modp(layer), modp(next_layer), _mods_spec(layer, pg), _mods_spec(next_layer, psh),
            _mods_spec(next_layer, psc), row, row]


def _epilogue(x, y, modp, mods, ln_g, ln_b, layer, next_layer, pg, psh, psc, w_router_t=None):
    route = w_router_t is not None
    project = isinstance(y, tuple)
    unit = lambda w: pl.BlockSpec((ROWS_PER_UNIT, w), lambda i: (i, 0))
    if project:
        mix_p, mix_s, w_out, b_out = y
        in_specs = [pl.BlockSpec((ROWS_PER_UNIT, D_MODEL), lambda i: (jnp.minimum(i, N_PROMPT_UNITS - 1), 0)),
                    pl.BlockSpec((M_SAMPLE, D_MODEL), lambda i: (0, 0)),
                    _resident(layer, (D_MODEL, D_MODEL)),
                    pl.BlockSpec((1, D_MODEL), lambda i: (0, 0))]
        args = [mix_p, mix_s, w_out, b_out.reshape(1, D_MODEL)]
        scratch = [pltpu.VMEM((D_MODEL, D_MODEL), BF16)]
    else:
        in_specs = [unit(D_MODEL)]
        args = [y]
        scratch = []
    in_specs += [unit(D_MODEL)] + _epilogue_specs(layer, next_layer, pg, psh, psc)
    args += [x, modp, modp, mods, mods, mods, ln_g.reshape(1, D_MODEL), ln_b.reshape(1, D_MODEL)]
    out_specs = [unit(D_MODEL), unit(D_MODEL)]
    out_shape = [jax.ShapeDtypeStruct((M_ROWS, D_MODEL), F32), jax.ShapeDtypeStruct((M_ROWS, D_MODEL), BF16)]
    if route:
        in_specs.append(pl.BlockSpec((N_EXPERTS, D_MODEL), lambda i: (0, 0)))
        args.append(w_router_t)
        out_specs += [unit(PACKED_WIDTH), unit(TOP_K), unit(TOP_K)]
        out_shape += [jax.ShapeDtypeStruct((M_ROWS, PACKED_WIDTH), jnp.uint32),
                      jax.ShapeDtypeStruct((M_ROWS, TOP_K), jnp.int32),
                      jax.ShapeDtypeStruct((M_ROWS, TOP_K), F32)]
    return pl.pallas_call(
        functools.partial(_epilogue_body, pg=pg, psh=psh, psc=psc, route=route, project=project),
        grid=(N_UNITS,),
        in_specs=in_specs,
        out_specs=out_specs,
        out_shape=out_shape,
        scratch_shapes=scratch,
        compiler_params=_params("arbitrary"),
        name=("outproj_" if project else "") + ("epilogue_route" if route else "epilogue"),
    )(*args)


FF_LANE_BLOCKS = D_FF // 128


def _ffn_prep_body(g_ref, u_ref, d_ref, go_ref, uo_ref, do_ref):
    c = pl.program_id(0)

    @pl.when(c < FF_LANE_BLOCKS)
    def _():
        go_ref[...] = g_ref[...].astype(BF16)
        uo_ref[...] = u_ref[...].astype(BF16)
        do_ref[...] = d_ref[...].astype(BF16)

    @pl.when(c >= FF_LANE_BLOCKS)
    def _():
        go_ref[...] = jnp.zeros_like(go_ref)
        uo_ref[...] = jnp.zeros_like(uo_ref)
        do_ref[...] = jnp.zeros_like(do_ref)


def _ffn_prep(w_gu, w_down, ld):
    src = lambda c: jnp.minimum(c, FF_LANE_BLOCKS - 1)
    return pl.pallas_call(
        _ffn_prep_body,
        grid=(D_FF_PAD // 128,),
        in_specs=[pl.BlockSpec((None, D_MODEL, 128), lambda c: (ld, 0, src(c))),
                  pl.BlockSpec((None, D_MODEL, 128), lambda c: (ld, 0, FF_LANE_BLOCKS + src(c))),
                  pl.BlockSpec((None, 128, D_MODEL), lambda c: (ld, src(c), 0))],
        out_specs=[pl.BlockSpec((D_MODEL, 128), lambda c: (0, c)),
                   pl.BlockSpec((D_MODEL, 128), lambda c: (0, c)),
                   pl.BlockSpec((128, D_MODEL), lambda c: (c, 0))],
        out_shape=[jax.ShapeDtypeStruct((D_MODEL, D_FF_PAD), BF16), jax.ShapeDtypeStruct((D_MODEL, D_FF_PAD), BF16),
                   jax.ShapeDtypeStruct((D_FF_PAD, D_MODEL), BF16)],
        compiler_params=_params("arbitrary"),
        name="ffn_prep",
    )(w_gu, w_gu, w_down)


def _swiglu_rows(x, wg, wu, wd):
    g = _dot(x, wg)
    u = _dot(x, wu)
    return _dot((_silu(g) * u).astype(BF16), wd)


def _ffn_body(h_ref, wg_ref, wu_ref, wd_ref, o_ref):
    i = pl.program_id(0)
    j = pl.program_id(1)
    upt = FFN_ROW_TILE // ROWS_PER_UNIT
    full = (i + 1) * upt <= N_UNITS

    def run(n_units):
        @pl.when(j == 0)
        def _():
            o_ref[:n_units * ROWS_PER_UNIT, :] = jnp.zeros((n_units * ROWS_PER_UNIT, D_MODEL), F32)

        for r in range(n_units):
            rows = slice(r * ROWS_PER_UNIT, (r + 1) * ROWS_PER_UNIT)
            o_ref[rows, :] += _swiglu_rows(h_ref[rows, :], wg_ref[...], wu_ref[...], wd_ref[...])

    pl.when(full)(functools.partial(run, upt))
    pl.when(jnp.logical_not(full))(functools.partial(run, N_UNITS % upt))


def _ffn(h, wg, wu, wd):
    nj = D_FF_PAD // FFN_TILE
    return pl.pallas_call(
        _ffn_body,
        grid=(pl.cdiv(M_ROWS, FFN_ROW_TILE), nj),
        in_specs=[pl.BlockSpec((FFN_ROW_TILE, D_MODEL), lambda i, j: (i, 0)),
                  pl.BlockSpec((D_MODEL, FFN_TILE), lambda i, j: (0, j)),
                  pl.BlockSpec((D_MODEL, FFN_TILE), lambda i, j: (0, j)),
                  pl.BlockSpec((FFN_TILE, D_MODEL), lambda i, j: (j, 0))],
        out_specs=pl.BlockSpec((FFN_ROW_TILE, D_MODEL), lambda i, j: (i, 0)),
        out_shape=jax.ShapeDtypeStruct((M_ROWS, D_MODEL), F32),
        compiler_params=_params("arbitrary", "arbitrary"),
        name="ffn_dense",
    )(h, wg, wu, wd)


def _row_copy(src_hbm, row, dst, dst_row, sem):
    return pltpu.make_async_copy(src_hbm.at[pl.ds(row, 1), :], dst.at[pl.ds(dst_row, 1), :], sem)


ROW_DMA_UNROLL = 8


def _unit_copy(src, src_unit, dst, dst_unit, sem):
    rows = lambda u: pl.ds(pl.multiple_of(u * ROWS_PER_UNIT, ROWS_PER_UNIT), ROWS_PER_UNIT)
    return pltpu.make_async_copy(src.at[rows(src_unit), :], dst.at[rows(dst_unit), :], sem)


def _moe_dispatch_body(dest_ref, pad_lo_ref, pad_hi_ref, nu_ref, hp_ref, xs_hbm, zbuf, sem, zsem):
    i = pl.program_id(0)

    @pl.when(i == 0)
    def _():
        zbuf[...] = jnp.zeros_like(zbuf)
        zero_row = lambda r: _row_copy(zbuf, 0, xs_hbm, r, zsem)
        zero_unit = lambda u: _unit_copy(zbuf, 0, xs_hbm, u, zsem)

        def over_gaps(act):
            def row_body(r, c):
                act(zero_row(r))
                return c

            def unit_body(u, c):
                act(zero_unit(u))
                return c

            for e in range(N_EXPERTS):
                lax.fori_loop(pad_lo_ref[e], pad_hi_ref[e], row_body, 0)
            lax.fori_loop(nu_ref[0], MOE_UNITS, unit_body, 0)

        over_gaps(lambda cp: cp.start())
        over_gaps(lambda cp: cp.wait())

    base = i * (ROWS_PER_UNIT * TOP_K)

    def start(r, c):
        for k in range(TOP_K):
            _row_copy(hp_ref, r, xs_hbm, dest_ref[base + TOP_K * r + k], sem).start()
        return c

    def wait(r, c):
        for k in range(TOP_K):
            _row_copy(hp_ref, r, xs_hbm, 0, sem).wait()
        return c

    lax.fori_loop(0, ROWS_PER_UNIT, start, 0, unroll=ROW_DMA_UNROLL // TOP_K)
    lax.fori_loop(0, ROWS_PER_UNIT, wait, 0, unroll=ROW_DMA_UNROLL // TOP_K)


def _moe_dispatch(dest, pad_lo, pad_hi, total_units, h_packed):
    grid_spec = pltpu.PrefetchScalarGridSpec(
        num_scalar_prefetch=4,
        grid=(N_UNITS,),
        in_specs=[pl.BlockSpec((ROWS_PER_UNIT, PACKED_WIDTH), lambda i, *_: (i, 0))],
        out_specs=pl.BlockSpec(memory_space=pl.ANY),
        scratch_shapes=[pltpu.VMEM((ROWS_PER_UNIT, PACKED_WIDTH), jnp.uint32),
                        pltpu.SemaphoreType.DMA(()), pltpu.SemaphoreType.DMA(())],
    )
    return pl.pallas_call(
        _moe_dispatch_body,
        grid_spec=grid_spec,
        out_shape=jax.ShapeDtypeStruct((MOE_ROWS, PACKED_WIDTH), jnp.uint32),
        compiler_params=_params("arbitrary"),
        name="moe_dispatch",
    )(dest, pad_lo, pad_hi, total_units, h_packed)


def _moe_gmm_body(te_ref, tu_ref, tn_ref, nu_ref, xs_hbm, wg_ref, wu_ref, wd_ref, ys_hbm,
                  xsc, acc, wgu_b, wd_b, stage, sem_in, sem_out):
    t = pl.program_id(0)
    j = pl.program_id(1)
    nj = pl.num_programs(1)
    n = tn_ref[t]
    u0 = tu_ref[t]
    tf = MOE_FF_TILE

    @pl.when((j == 0) & (n > 0))
    def _():
        unit_in = lambda u, s: pltpu.make_async_copy(
            xs_hbm.at[pl.ds(pl.multiple_of((u0 + u) * ROWS_PER_UNIT, ROWS_PER_UNIT), ROWS_PER_UNIT), :],
            stage.at[s], sem_in.at[s])
        unit_in(0, 0).start()

        def load(u, c):
            s = u & 1

            @pl.when(u + 1 < n)
            def _():
                unit_in(u + 1, 1 - s).start()

            rows = pl.ds(pl.multiple_of(u * ROWS_PER_UNIT, ROWS_PER_UNIT), ROWS_PER_UNIT)
            unit_in(u, s).wait()
            lo, hi = _unpack_bf16_pairs(stage[s])
            xsc[rows, :PACKED_WIDTH] = lo
            xsc[rows, PACKED_WIDTH:] = hi
            return c

        lax.fori_loop(0, n, load, 0)

    t_prev = jnp.maximum(t - 1, 0)

    @pl.when((j == 0) & (t > 0) & (tn_ref[t_prev] > 0))
    def _():
        def wait(u, c):
            _unit_copy(acc, u, ys_hbm, u, sem_out).wait()
            return c

        lax.fori_loop(0, tn_ref[t_prev], wait, 0)

    @pl.when((j == 0) & (n > 0))
    def _():
        def zero(u, c):
            acc[pl.ds(pl.multiple_of(u * ROWS_PER_UNIT, ROWS_PER_UNIT), ROWS_PER_UNIT), :] = jnp.zeros(
                (ROWS_PER_UNIT, D_MODEL), F32)
            return c

        lax.fori_loop(0, n, zero, 0)

    @pl.when((j == 0) & (t == 0))
    def _():
        def start(u, c):
            _unit_copy(acc, 0, ys_hbm, u, sem_out).start()
            return c

        def wait(u, c):
            _unit_copy(acc, 0, ys_hbm, u, sem_out).wait()
            return c

        lax.fori_loop(nu_ref[0], MOE_UNITS, start, 0)
        lax.fori_loop(nu_ref[0], MOE_UNITS, wait, 0)

    @pl.when(n > 0)
    def _():
        wgu_b[:, :tf] = wg_ref[...].astype(BF16)
        wgu_b[:, tf:] = wu_ref[...].astype(BF16)
        wd_b[...] = wd_ref[...].astype(BF16)

        def unit_rows(r0):
            rs = pl.ds(r0, ROWS_PER_UNIT)
            gu = _dot(xsc[rs, :], wgu_b[...])
            a = (_silu(gu[:, :tf]) * gu[:, tf:]).astype(BF16)
            acc[rs, :] += _dot(a, wd_b[...])

        def units_from(u_first, k):
            for h in range(k):
                unit_rows(pl.multiple_of((u_first + h) * ROWS_PER_UNIT, ROWS_PER_UNIT))

        for b in reversed(range(MOE_TILE_UNITS.bit_length())):
            size = 1 << b
            done = lax.shift_left(lax.shift_right_logical(n, b + 1), b + 1)
            pl.when((n & size) != 0)(functools.partial(units_from, done, size))

    @pl.when((j == nj - 1) & (n > 0))
    def _():
        def start(u, c):
            _unit_copy(acc, u, ys_hbm, u0 + u, sem_out).start()
            return c

        def wait(u, c):
            _unit_copy(acc, u, ys_hbm, u0 + u, sem_out).wait()
            return c

        lax.fori_loop(0, n, start, 0)

        @pl.when(t == pl.num_programs(0) - 1)
        def _():
            lax.fori_loop(0, n, wait, 0)


def _moe_gmm(tile_expert, tile_unit0, tile_nunits, total_units, xs, w_gu, w_down, lm):
    nj = D_FF_EXPERT // MOE_FF_TILE
    jj = lambda t, j, tn: jnp.where(tn[t] > 0, j, nj - 1)
    tile_rows = MOE_TILE_UNITS * ROWS_PER_UNIT
    grid_spec = pltpu.PrefetchScalarGridSpec(
        num_scalar_prefetch=4,
        grid=(MOE_TILES, nj),
        in_specs=[pl.BlockSpec(memory_space=pl.ANY),
                  pl.BlockSpec((None, None, D_MODEL, MOE_FF_TILE),
                               lambda t, j, te, tu, tn, nu: (lm, te[t], 0, jj(t, j, tn))),
                  pl.BlockSpec((None, None, D_MODEL, MOE_FF_TILE),
                               lambda t, j, te, tu, tn, nu: (lm, te[t], 0, nj + jj(t, j, tn))),
                  pl.BlockSpec((None, None, MOE_FF_TILE, D_MODEL),
                               lambda t, j, te, tu, tn, nu: (lm, te[t], jj(t, j, tn), 0))],
        out_specs=pl.BlockSpec(memory_space=pl.ANY),
        scratch_shapes=[pltpu.VMEM((tile_rows, D_MODEL), BF16), pltpu.VMEM((tile_rows, D_MODEL), F32),
                        pltpu.VMEM((D_MODEL, 2 * MOE_FF_TILE), BF16), pltpu.VMEM((MOE_FF_TILE, D_MODEL), BF16),
                        pltpu.VMEM((2, ROWS_PER_UNIT, PACKED_WIDTH), jnp.uint32),
                        pltpu.SemaphoreType.DMA((2,)), pltpu.SemaphoreType.DMA(())],
    )
    return pl.pallas_call(
        _moe_gmm_body,
        grid_spec=grid_spec,
        out_shape=jax.ShapeDtypeStruct((MOE_ROWS, D_MODEL), F32),
        compiler_params=_params("arbitrary", "arbitrary"),
        name="moe_gmm",
    )(tile_expert, tile_unit0, tile_nunits, total_units, xs, w_gu, w_gu, w_down)


def _moe_combine_body(dest_ref, ys_hbm, rw_ref, x_ref, mpg_ref, mpn_ref, gs_ref, shs_ref, scs_ref, lg_ref, lb_ref,
                      *rest, pg, psh, psc, final):
    if final:
        xo_prompt, xo_sample, buf, sem = rest
        xo_ref, ho_ref = (xo_prompt, xo_sample), None
    else:
        xo_ref, ho_ref, buf, sem = rest
    i = pl.program_id(0)
    slot = i & 1

    def issue(unit, s):
        base = unit * (ROWS_PER_UNIT * TOP_K)

        def start(r, c):
            for k in range(TOP_K):
                _row_copy(ys_hbm, dest_ref[base + TOP_K * r + k], buf.at[s, k], r, sem.at[s]).start()
            return c

        lax.fori_loop(0, ROWS_PER_UNIT, start, 0, unroll=ROW_DMA_UNROLL // TOP_K)

    @pl.when(i == 0)
    def _():
        issue(0, 0)

    @pl.when(i + 1 < pl.num_programs(0))
    def _():
        issue(i + 1, 1 - slot)

    def wait(r, c):
        for k in range(TOP_K):
            _row_copy(ys_hbm, 0, buf.at[slot, k], r, sem.at[slot]).wait()
        return c

    lax.fori_loop(0, ROWS_PER_UNIT, wait, 0, unroll=ROW_DMA_UNROLL // TOP_K)
    y_of = lambda prompt, rs: rw_ref[rs, 0:1] * buf[slot, 0, rs, :] + rw_ref[rs, 1:2] * buf[slot, 1, rs, :]
    _finish_dispatch(i, x_ref, y_of, pg, psh, psc, mpg_ref, mpn_ref, gs_ref, shs_ref, scs_ref,
                     lg_ref, lb_ref, xo_ref, ho_ref, None)


def _moe_combine(dest, ys, route_w, x, modp, mods, ln_g, ln_b, layer, next_layer, pg, psh, psc, final):
    unit = lambda w: pl.BlockSpec((ROWS_PER_UNIT, w), lambda i, d: (i, 0))
    if final:
        out_specs = [_prompt_unit_spec(D_MODEL), _sample_rows_spec(D_MODEL)]
        out_shape = [jax.ShapeDtypeStruct((M_PROMPT, D_MODEL), F32), jax.ShapeDtypeStruct((M_SAMPLE, D_MODEL), F32)]
    else:
        out_specs = [unit(D_MODEL), unit(D_MODEL)]
        out_shape = [jax.ShapeDtypeStruct((M_ROWS, D_MODEL), F32), jax.ShapeDtypeStruct((M_ROWS, D_MODEL), BF16)]
    grid_spec = pltpu.PrefetchScalarGridSpec(
        num_scalar_prefetch=1,
        grid=(N_UNITS,),
        in_specs=[pl.BlockSpec(memory_space=pl.ANY), unit(TOP_K), unit(D_MODEL)]
        + _epilogue_specs(layer, next_layer, pg, psh, psc),
        out_specs=out_specs,
        scratch_shapes=[pltpu.VMEM((2, TOP_K, ROWS_PER_UNIT, D_MODEL), F32), pltpu.SemaphoreType.DMA((2,))],
    )
    return pl.pallas_call(
        functools.partial(_moe_combine_body, pg=pg, psh=psh, psc=psc, final=final),
        grid_spec=grid_spec,
        out_shape=out_shape,
        compiler_params=_params("arbitrary"),
        name="moe_combine",
    )(dest, ys, route_w, x, modp, modp, mods, mods, mods, ln_g.reshape(1, D_MODEL), ln_b.reshape(1, D_MODEL))


def _moe_plan(route_i):
    i32 = jnp.int32
    e_flat = route_i.reshape(-1)
    onehot = (e_flat[:, None] == jnp.arange(N_EXPERTS, dtype=i32)[None, :]).astype(i32)
    csum = jnp.cumsum(onehot, axis=0)
    rank = jnp.sum(csum * onehot, axis=1) - 1
    counts = csum[-1]
    units_e = (counts + ROWS_PER_UNIT - 1) // ROWS_PER_UNIT
    unit_end = jnp.cumsum(units_e)
    unit_start = unit_end - units_e
    dest = (unit_start[e_flat] * ROWS_PER_UNIT + rank).astype(i32)
    pad_lo = (unit_start * ROWS_PER_UNIT + counts).astype(i32)
    pad_hi = (unit_end * ROWS_PER_UNIT).astype(i32)
    total_units = unit_end[-1:].astype(i32)
    tiles_e = (units_e + MOE_TILE_UNITS - 1) // MOE_TILE_UNITS
    tile_end = jnp.cumsum(tiles_e)
    tile_start = tile_end - tiles_e
    n_tiles = tile_end[-1]
    t_ids = jnp.arange(MOE_TILES, dtype=i32)
    t_eff = jnp.minimum(t_ids, n_tiles - 1)
    tile_expert = jnp.sum((t_eff[:, None] >= tile_end[None, :]).astype(i32), axis=1)
    k = t_eff - tile_start[tile_expert]
    tile_unit0 = unit_start[tile_expert] + k * MOE_TILE_UNITS
    units_left = units_e[tile_expert] - k * MOE_TILE_UNITS
    tile_nunits = jnp.where(t_ids < n_tiles, jnp.minimum(units_left, MOE_TILE_UNITS), 0)
    return (dest, pad_lo, pad_hi, total_units, tile_expert.astype(i32), tile_unit0.astype(i32),
            tile_nunits.astype(i32))


def kernel(x_prompt, x_sample, cache_win_k, cache_win_v, c_prompt, c_sample, w_ada, b_ada, w_in, b_in, v_norm_g, v_norm_b, w_spatial, b_spatial, attn_sinks, w_out, b_out, ln1_g, ln1_b, ln2_g, ln2_b, w_ffn_gu, w_ffn_down, w_router, w_exp_gu, w_exp_down):
    c_rows = jnp.concatenate([c_prompt, c_sample, jnp.zeros((ADA_ROWS - BATCH - DEC_BATCH, D_MODEL), F32)], axis=0)
    mod = _adaln(c_rows, w_ada, b_ada)
    modp = mod[:, :BATCH].reshape(DEPTH, BATCH, 6, D_MODEL)
    mods = jnp.repeat(mod[:, BATCH:BATCH + DEC_BATCH], DEC_SEQ, axis=1)

    assert DEPTH % 2 == 0, "the last layer is a mixture-of-experts layer; its combine kernel emits the outputs"
    h, x = _modulate0(x_prompt.reshape(M_PROMPT, D_MODEL), x_sample.reshape(M_SAMPLE, D_MODEL), modp, mods)

    rows = GQA * DEC_SEQ
    kp_list, vp_list, ks_list, vs_list, cv_list = [], [], [], [], []
    for l in range(DEPTH):
        nl = min(l + 1, DEPTH - 1)
        u, v, q, kv = _proj(h, w_in, l, b_in[l], v_norm_g[l], v_norm_b[l])
        b_s_t = b_spatial[l].T
        mix_p = _mix_prompt(attn_sinks[l].reshape(-1), u, v, q, kv, w_spatial[l], b_s_t)

        u_s = u[M_PROMPT:].reshape(DEC_BATCH, DEC_SEQ, A_WIDTH)
        v_s = v[M_PROMPT:].reshape(DEC_BATCH, DEC_SEQ, A_WIDTH)
        kv_s = kv[M_PROMPT:].reshape(DEC_BATCH, DEC_SEQ, 2 * KV_WIDTH)
        q_s = q[M_PROMPT:].reshape(DEC_BATCH, DEC_SEQ, N_KV_HEADS, GQA, HEAD_DIM)
        q_s = q_s.transpose(0, 2, 3, 1, 4).reshape(DEC_BATCH, N_KV_HEADS, rows, HEAD_DIM)
        ck = cache_win_k[l].reshape(DEC_BATCH, WINDOW, KV_WIDTH)
        cv = cache_win_v[l].reshape(DEC_BATCH, WINDOW, KV_WIDTH)
        sink_rows = jnp.repeat(attn_sinks[l], DEC_SEQ, axis=1).reshape(N_KV_HEADS, rows, 1)
        a_s, o_s = _mix_sample(u_s, v_s, q_s, kv_s, ck, cv, w_spatial[l], b_s_t, sink_rows)
        o_s = o_s.reshape(DEC_BATCH, N_KV_HEADS, GQA, DEC_SEQ, HEAD_DIM).transpose(0, 3, 1, 2, 4)
        mix_s = jnp.concatenate([a_s.reshape(M_SAMPLE, A_WIDTH), o_s.reshape(M_SAMPLE, B_WIDTH)], axis=1)

        y = (mix_p, mix_s, w_out, b_out[l])
        if l % 2 == 0:
            x, h = _epilogue(x, y, modp, mods, ln1_g[l], ln1_b[l], l, l, P_GATE1, P_SHIFT2, P_SCALE2)
            ld = l // 2
            wg, wu, wd = _ffn_prep(w_ffn_gu, w_ffn_down, ld)
            f = _ffn(h, wg, wu, wd)
            x, h = _epilogue(x, f, modp, mods, ln2_g[l], ln2_b[l], l, nl, P_GATE2, P_SHIFT1, P_SCALE1)
        else:
            lm = l // 2
            x, h, h_packed, route_i, route_w = _epilogue(x, y, modp, mods, ln1_g[l], ln1_b[l], l, l,
                                                      P_GATE1, P_SHIFT2, P_SCALE2, w_router_t=w_router[lm].T)
            dest, pad_lo, pad_hi, total_units, tile_expert, tile_unit0, tile_nunits = _moe_plan(route_i)
            xs = _moe_dispatch(dest, pad_lo, pad_hi, total_units, h_packed)
            ys = _moe_gmm(tile_expert, tile_unit0, tile_nunits, total_units, xs, w_exp_gu, w_exp_down, lm)
            x, h = _moe_combine(dest, ys, route_w, x, modp, mods, ln2_g[l], ln2_b[l], l, nl,
                                P_GATE2, P_SHIFT1, P_SCALE1, final=(l == DEPTH - 1))

        kv_p = kv[:M_PROMPT].reshape(BATCH, SEQ, 2 * KV_WIDTH)[:, SEQ - WINDOW:]
        kp_list.append(kv_p[..., :KV_WIDTH].reshape(BATCH, WINDOW, N_KV_HEADS, HEAD_DIM))
        vp_list.append(kv_p[..., KV_WIDTH:].reshape(BATCH, WINDOW, N_KV_HEADS, HEAD_DIM))
        k_new = kv_s[..., :KV_WIDTH].reshape(DEC_BATCH, DEC_SEQ, N_KV_HEADS, HEAD_DIM)
        v_new = kv_s[..., KV_WIDTH:].reshape(DEC_BATCH, DEC_SEQ, N_KV_HEADS, HEAD_DIM)
        ks_list.append(jnp.concatenate([cache_win_k[l][:, DEC_SEQ:], k_new], axis=1))
        vs_list.append(jnp.concatenate([cache_win_v[l][:, DEC_SEQ:], v_new], axis=1))
        cv_list.append(v_s.astype(F32))

    y_prompt = x.reshape(BATCH, SEQ, D_MODEL)
    y_sample = h.reshape(DEC_BATCH, DEC_SEQ, D_MODEL)
    return (y_prompt, y_sample, jnp.stack(kp_list), jnp.stack(vp_list), jnp.stack(ks_list), jnp.stack(vs_list),
            jnp.stack(cv_list))
```

```python
import functools

import jax
import jax.numpy as jnp
from jax import lax
from jax.experimental import pallas as pl
from jax.experimental.pallas import tpu as pltpu

F32 = jnp.float32
BF16 = jnp.bfloat16

D_MODEL = 2048
BATCH = 4
SEQ = 2048
DEPTH = 4
DEC_BATCH = 32
DEC_SEQ = 8
CHUNK = 128
A_WIDTH = D_MODEL // 2
A_GROUPS = 4
A_GROUP_DIM = A_WIDTH // A_GROUPS
HEAD_DIM = 64
B_WIDTH = D_MODEL - A_WIDTH
N_HEADS = B_WIDTH // HEAD_DIM
N_KV_HEADS = 2
GQA = N_HEADS // N_KV_HEADS
KV_WIDTH = N_KV_HEADS * HEAD_DIM
WINDOW = 128
IN_COLS = 2 * A_WIDTH + B_WIDTH + 2 * KV_WIDTH
D_FF = 5504
N_EXPERTS = 8
TOP_K = 2
D_FF_EXPERT = 7168
ALPHA = (2.0 * DEPTH) ** 0.25
LN_EPS = 1e-5
NEG_INF = -1e30

M_PROMPT = BATCH * SEQ
M_SAMPLE = DEC_BATCH * DEC_SEQ
M_ROWS = M_PROMPT + M_SAMPLE
ROWS_PER_UNIT = 256
N_PROMPT_UNITS = M_PROMPT // ROWS_PER_UNIT
N_UNITS = M_ROWS // ROWS_PER_UNIT
UNITS_PER_SEQ = SEQ // ROWS_PER_UNIT
ADA_ROWS = 40
P_SHIFT1, P_SCALE1, P_GATE1, P_SHIFT2, P_SCALE2, P_GATE2 = range(6)

D_FF_PAD = 5632
FFN_TILE = 512
FFN_ROW_TILE = 4 * ROWS_PER_UNIT
MOE_FF_TILE = 256
N_PAIRS = M_ROWS * TOP_K
MOE_UNITS = N_PAIRS // ROWS_PER_UNIT + N_EXPERTS
MOE_ROWS = MOE_UNITS * ROWS_PER_UNIT
MOE_TILE_UNITS = 9
UNITS_PER_DOT = 2
MOE_TILES =MOE_UNITS // MOE_TILE_UNITS + N_EXPERTS

VMEM_LIMIT = 56 * 1024 * 1024


def _params(*sem):
    return pltpu.CompilerParams(dimension_semantics=sem, vmem_limit_bytes=VMEM_LIMIT)


def _dot(a, b):
    return jnp.dot(a, b, preferred_element_type=F32)


def _layer_norm(z, g, b):
    mu = jnp.mean(z, axis=-1, keepdims=True)
    zc = z - mu
    var = jnp.mean(zc * zc, axis=-1, keepdims=True)
    return zc * lax.rsqrt(var + LN_EPS) * g + b


def _gelu_tanh(x):
    return x * (0.5 * (1.0 + jnp.tanh(0.7978845608028654 * (x + 0.044715 * (x * x * x)))))


def _silu(x):
    return x * jax.nn.sigmoid(x)


def _prompt_batch_of_unit(i):
    return jnp.minimum(i // UNITS_PER_SEQ, BATCH - 1)


def _adaln_body(c_ref, w_ref, b_ref, o_ref):
    s = _silu(c_ref[...]).astype(BF16)
    o_ref[...] = _dot(s, w_ref[...].astype(BF16)) + b_ref[...]


def _adaln(c_rows, w_ada, b_ada):
    tn = 1024
    return pl.pallas_call(
        _adaln_body,
        grid=(DEPTH, 6 * D_MODEL // tn),
        in_specs=[pl.BlockSpec((ADA_ROWS, D_MODEL), lambda l, n: (0, 0)),
                  pl.BlockSpec((None, D_MODEL, tn), lambda l, n: (l, 0, n)),
                  pl.BlockSpec((None, 1, tn), lambda l, n: (l, 0, n))],
        out_specs=pl.BlockSpec((None, ADA_ROWS, tn), lambda l, n: (l, 0, n)),
        out_shape=jax.ShapeDtypeStruct((DEPTH, ADA_ROWS, 6 * D_MODEL), F32),
        compiler_params=_params("arbitrary", "arbitrary"),
        name="adaln",
    )(c_rows, w_ada, b_ada.reshape(DEPTH, 1, 6 * D_MODEL))


def _mod0_body(xp_ref, xs_ref, mp_ref, ss_ref, cs_ref, h_ref, x_ref):
    i = pl.program_id(0)

    @pl.when(i < N_PROMPT_UNITS)
    def _():
        x = xp_ref[...]
        x_ref[...] = x
        h_ref[...] = (x * (1 + mp_ref[P_SCALE1:P_SCALE1 + 1, :]) + mp_ref[P_SHIFT1:P_SHIFT1 + 1, :]).astype(BF16)

    @pl.when(i >= N_PROMPT_UNITS)
    def _():
        x = xs_ref[...]
        x_ref[...] = x
        h_ref[...] = (x * (1 + cs_ref[...]) + ss_ref[...]).astype(BF16)


def _unit_spec(width):
    return pl.BlockSpec((ROWS_PER_UNIT, width), lambda i: (i, 0))


def _modp_spec(layer):
    return pl.BlockSpec((None, None, 6, D_MODEL), lambda i: (layer, _prompt_batch_of_unit(i), 0, 0))


def _mods_spec(layer, p):
    return pl.BlockSpec((None, ROWS_PER_UNIT, D_MODEL), lambda i, *_: (layer, 0, p))


def _prompt_unit_spec(width):
    return pl.BlockSpec((ROWS_PER_UNIT, width), lambda i, *_: (jnp.minimum(i, N_PROMPT_UNITS - 1), 0))


def _sample_rows_spec(width):
    return pl.BlockSpec((M_SAMPLE, width), lambda i, *_: (0, 0))


def _modulate0(x_prompt, x_sample, modp, mods):
    return pl.pallas_call(
        _mod0_body,
        grid=(N_UNITS,),
        in_specs=[_prompt_unit_spec(D_MODEL), _sample_rows_spec(D_MODEL), _modp_spec(0),
                  _mods_spec(0, P_SHIFT1), _mods_spec(0, P_SCALE1)],
        out_specs=[_unit_spec(D_MODEL), _unit_spec(D_MODEL)],
        out_shape=[jax.ShapeDtypeStruct((M_ROWS, D_MODEL), BF16), jax.ShapeDtypeStruct((M_ROWS, D_MODEL), F32)],
        compiler_params=_params("arbitrary"),
        name="modulate0",
    )(x_prompt, x_sample, modp, mods, mods)


def _cast_weight_once(w32_ref, w_ref, chunk):
    @pl.when(pl.program_id(0) == 0)
    def _():
        for c in range(0, w32_ref.shape[1], chunk):
            w_ref[:, c:c + chunk] = w32_ref[:, c:c + chunk].astype(BF16)


def _resident(layer, shape):
    return pl.BlockSpec((None,) + shape, lambda i: (layer,) + (0,) * len(shape), pipeline_mode=pl.Buffered(1))


def _proj_body(h_ref, w32_ref, b_ref, gv_ref, bv_ref, u_ref, v_ref, q_ref, kv_ref, w_ref):
    _cast_weight_once(w32_ref, w_ref, A_GROUP_DIM)
    h = h_ref[...]
    gd = A_GROUP_DIM
    for c in range(A_GROUPS):
        sl = slice(c * gd, (c + 1) * gd)
        z = _dot(h, w_ref[:, sl]) + b_ref[:, sl]
        u_ref[:, sl] = _gelu_tanh(z).astype(BF16)
    for g in range(A_GROUPS):
        sl = slice(A_WIDTH + g * gd, A_WIDTH + (g + 1) * gd)
        z = _gelu_tanh(_dot(h, w_ref[:, sl]) + b_ref[:, sl])
        vn = _layer_norm(z, gv_ref[g:g + 1, :], bv_ref[g:g + 1, :])
        v_ref[:, g * gd:(g + 1) * gd] = vn.astype(BF16)
    o1 = 2 * A_WIDTH
    o2 = o1 + B_WIDTH
    z = _dot(h, w_ref[:, o1:o2]) + b_ref[:, o1:o2]
    q_ref[...] = (z * (HEAD_DIM ** -0.5)).astype(BF16)
    kv_ref[...] = _dot(h, w_ref[:, o2:]) + b_ref[:, o2:]


def _proj(h, w_all, layer, b, gv, bv):
    const = lambda shape: pl.BlockSpec(shape, lambda i: (0,) * len(shape))
    return pl.pallas_call(
        _proj_body,
        grid=(N_UNITS,),
        in_specs=[_unit_spec(D_MODEL), _resident(layer, (D_MODEL, IN_COLS)), const((1, IN_COLS)),
                  const((A_GROUPS, A_GROUP_DIM)), const((A_GROUPS, A_GROUP_DIM))],
        out_specs=[_unit_spec(A_WIDTH), _unit_spec(A_WIDTH), _unit_spec(B_WIDTH), _unit_spec(2 * KV_WIDTH)],
        out_shape=[jax.ShapeDtypeStruct((M_ROWS, A_WIDTH), BF16), jax.ShapeDtypeStruct((M_ROWS, A_WIDTH), BF16),
                   jax.ShapeDtypeStruct((M_ROWS, B_WIDTH), BF16), jax.ShapeDtypeStruct((M_ROWS, 2 * KV_WIDTH), F32)],
        scratch_shapes=[pltpu.VMEM((D_MODEL, IN_COLS), BF16)],
        compiler_params=_params("arbitrary"),
        name="proj",
    )(h, w_all, b.reshape(1, IN_COLS), gv, bv)


def _mix_prompt_body(sink_ref, u_ref, v_ref, q_ref, kvc_ref, kvp_ref, ws_ref, bs_ref, o_ref):
    blk = pl.program_id(1)
    gd = A_GROUP_DIM
    tr = lax.broadcasted_iota(jnp.int32, (CHUNK, CHUNK), 0)
    tc = lax.broadcasted_iota(jnp.int32, (CHUNK, CHUNK), 1)
    for g in range(A_GROUPS):
        sl = slice(g * gd, (g + 1) * gd)
        w = jnp.where(tr >= tc, ws_ref[g], 0.0).astype(BF16)
        s = _dot(w, v_ref[:, sl]) + bs_ref[:, g:g + 1]
        o_ref[:, sl] = (u_ref[:, sl].astype(F32) * s).astype(BF16)

    kvc = kvc_ref[...]
    kvp = kvp_ref[...]
    row = lax.broadcasted_iota(jnp.int32, (WINDOW, 2 * WINDOW), 0)
    col = lax.broadcasted_iota(jnp.int32, (WINDOW, 2 * WINDOW), 1)
    dlt = row + WINDOW - col
    mask = (dlt >= 0) & (dlt <= WINDOW) & ((col >= WINDOW) | (blk > 0))
    hd = HEAD_DIM
    for j in range(N_KV_HEADS):
        ks = slice(j * hd, (j + 1) * hd)
        vs = slice(KV_WIDTH + j * hd, KV_WIDTH + (j + 1) * hd)
        kk = jnp.concatenate([kvp[:, ks], kvc[:, ks]], axis=0).astype(BF16)
        vv = jnp.concatenate([kvp[:, vs], kvc[:, vs]], axis=0).astype(BF16)
        for hh in range(GQA):
            h = j * GQA + hh
            qh = q_ref[:, h * hd:(h + 1) * hd]
            s = lax.dot_general(qh, kk, (((1,), (1,)), ((), ())), preferred_element_type=F32)
            s = jnp.where(mask, s, NEG_INF)
            sk = sink_ref[h]
            m = jnp.maximum(jnp.max(s, axis=-1, keepdims=True), sk)
            p = jnp.exp(s - m)
            den = jnp.sum(p, axis=-1, keepdims=True) + jnp.exp(sk - m)
            o = _dot(p.astype(BF16), vv) / den
            o_ref[:, A_WIDTH + h * hd:A_WIDTH + (h + 1) * hd] = o.astype(BF16)


def _mix_prompt(sinks, u, v, q, kv, w_s, b_s_t):
    nb = SEQ // WINDOW
    row_blk = lambda w: pl.BlockSpec((WINDOW, w), lambda b, i, s: (b * nb + i, 0))
    grid_spec = pltpu.PrefetchScalarGridSpec(
        num_scalar_prefetch=1,
        grid=(BATCH, nb),
        in_specs=[row_blk(A_WIDTH), row_blk(A_WIDTH), row_blk(B_WIDTH), row_blk(2 * KV_WIDTH),
                  pl.BlockSpec((WINDOW, 2 * KV_WIDTH), lambda b, i, s: (b * nb + jnp.maximum(i - 1, 0), 0)),
                  pl.BlockSpec((A_GROUPS, CHUNK, CHUNK), lambda b, i, s: (0, 0, 0)),
                  pl.BlockSpec((CHUNK, A_GROUPS), lambda b, i, s: (0, 0))],
        out_specs=pl.BlockSpec((WINDOW, D_MODEL), lambda b, i, s: (b * nb + i, 0)),
    )
    return pl.pallas_call(
        _mix_prompt_body,
        grid_spec=grid_spec,
        out_shape=jax.ShapeDtypeStruct((M_PROMPT, D_MODEL), BF16),
        compiler_params=_params("arbitrary", "arbitrary"),
        name="mix_prompt",
    )(sinks, u, v, q, kv, kv, w_s, b_s_t)


SAMPLE_KEYS = 2 * WINDOW


SAMPLE_BATCH_BLOCK = 8


def _mix_sample_body(u_ref, v_ref, q_ref, kn_ref, ck_ref, cv_ref, ws_ref, bs_ref, sk_ref, a_ref, o_ref):
    for b in range(SAMPLE_BATCH_BLOCK):
        _mix_sample_one(u_ref.at[b], v_ref.at[b], q_ref.at[b], kn_ref.at[b], ck_ref.at[b], cv_ref.at[b],
                        ws_ref, bs_ref, sk_ref, a_ref.at[b], o_ref.at[b])


def _mix_sample_one(u_ref, v_ref, q_ref, kn_ref, ck_ref, cv_ref, ws_ref, bs_ref, sk_ref, a_ref, o_ref):
    gd = A_GROUP_DIM
    ds = DEC_SEQ
    v = v_ref[...].astype(F32)
    u = u_ref[...].astype(F32)
    tr = lax.broadcasted_iota(jnp.int32, (ds, ds), 0)
    tc = lax.broadcasted_iota(jnp.int32, (ds, ds), 1)
    for g in range(A_GROUPS):
        sl = slice(g * gd, (g + 1) * gd)
        w = jnp.where(tr >= tc, ws_ref[g][:ds, :ds], 0.0)
        s = jnp.zeros((ds, gd), F32) + bs_ref[:ds, g:g + 1]
        for t in range(ds):
            s = s + w[:, t:t + 1] * v[t:t + 1, sl]
        a_ref[:, sl] = (u[:, sl] * s).astype(BF16)

    kn = kn_ref[...]
    ck = ck_ref[...]
    cv = cv_ref[...]
    hd = HEAD_DIM
    rows = GQA * ds
    qt = lax.broadcasted_iota(jnp.int32, (rows, SAMPLE_KEYS), 0) & (ds - 1)
    col = lax.broadcasted_iota(jnp.int32, (rows, SAMPLE_KEYS), 1)
    mask = ((col < WINDOW) & (col >= qt)) | ((col >= WINDOW) & (col - WINDOW <= qt))
    pad = jnp.zeros((SAMPLE_KEYS - WINDOW - ds, hd), F32)
    for j in range(N_KV_HEADS):
        ks = slice(j * hd, (j + 1) * hd)
        vs = slice(KV_WIDTH + j * hd, KV_WIDTH + (j + 1) * hd)
        kk = jnp.concatenate([ck[:, ks], kn[:, ks], pad], axis=0).astype(BF16)
        vv = jnp.concatenate([cv[:, ks], kn[:, vs], pad], axis=0).astype(BF16)
        s = lax.dot_general(q_ref[j], kk, (((1,), (1,)), ((), ())), preferred_element_type=F32)
        s = jnp.where(mask, s, NEG_INF)
        sk = sk_ref[j]
        m = jnp.maximum(jnp.max(s, axis=-1, keepdims=True), sk)
        p = jnp.exp(s - m)
        den = jnp.sum(p, axis=-1, keepdims=True) + jnp.exp(sk - m)
        o_ref[j] = (_dot(p.astype(BF16), vv) / den).astype(BF16)


def _mix_sample(u_s, v_s, q_s, kv_s, cache_k, cache_v, w_s, b_s_t, sink_rows):
    rows = GQA * DEC_SEQ
    per_b = lambda *shape: pl.BlockSpec((SAMPLE_BATCH_BLOCK,) + shape, lambda b: (b,) + (0,) * len(shape))
    const = lambda *shape: pl.BlockSpec(shape, lambda b: (0,) * len(shape))
    return pl.pallas_call(
        _mix_sample_body,
        grid=(DEC_BATCH // SAMPLE_BATCH_BLOCK,),
        in_specs=[per_b(DEC_SEQ, A_WIDTH), per_b(DEC_SEQ, A_WIDTH), per_b(N_KV_HEADS, rows, HEAD_DIM),
                  per_b(DEC_SEQ, 2 * KV_WIDTH), per_b(WINDOW, KV_WIDTH), per_b(WINDOW, KV_WIDTH),
                  const(A_GROUPS, CHUNK, CHUNK), const(CHUNK, A_GROUPS), const(N_KV_HEADS, rows, 1)],
        out_specs=[per_b(DEC_SEQ, A_WIDTH), per_b(N_KV_HEADS, rows, HEAD_DIM)],
        out_shape=[jax.ShapeDtypeStruct((DEC_BATCH, DEC_SEQ, A_WIDTH), BF16),
                   jax.ShapeDtypeStruct((DEC_BATCH, N_KV_HEADS, rows, HEAD_DIM), BF16)],
        compiler_params=_params("arbitrary"),
        name="mix_sample",
    )(u_s, v_s, q_s, kv_s, cache_k, cache_v, w_s, b_s_t, sink_rows)


PACKED_WIDTH = D_MODEL // 2


def _pack_bf16_pairs(h):
    bits = lambda v: lax.bitcast_convert_type(v.astype(BF16).astype(F32), jnp.uint32)
    lo = lax.shift_right_logical(bits(h[:, :PACKED_WIDTH]), jnp.uint32(16))
    hi = bits(h[:, PACKED_WIDTH:]) & jnp.uint32(0xFFFF0000)
    return lo | hi


def _unpack_bf16_pairs(w):
    lo = lax.bitcast_convert_type(lax.shift_left(w, jnp.uint32(16)), F32).astype(BF16)
    hi = lax.bitcast_convert_type(w & jnp.uint32(0xFFFF0000), F32).astype(BF16)
    return lo, hi


def _route_top2(h, rs, wrt_ref, ri_ref, rw_ref):
    logits = [jnp.sum(h * wrt_ref[e:e + 1, :], axis=-1, keepdims=True) for e in range(N_EXPERTS)]
    m1 = logits[0]
    i1 = jnp.zeros_like(m1, dtype=jnp.int32)
    for e in range(1, N_EXPERTS):
        gt = logits[e] > m1
        m1 = jnp.where(gt, logits[e], m1)
        i1 = jnp.where(gt, e, i1)
    m2 = jnp.full_like(m1, -jnp.inf)
    i2 = jnp.zeros_like(i1)
    for e in range(N_EXPERTS):
        ok = (i1 != e) & (logits[e] > m2)
        m2 = jnp.where(ok, logits[e], m2)
        i2 = jnp.where(ok, e, i2)
    w1 = 1.0 / (1.0 + jnp.exp(m2 - m1))
    ri_ref[rs, 0:1] = i1
    ri_ref[rs, 1:2] = i2
    rw_ref[rs, 0:1] = w1
    rw_ref[rs, 1:2] = 1.0 - w1


FINISH_SPLIT = 2


def _finish_rows(rs, x, y, gate, shift, scale, lg_ref, lb_ref, xo_ref, ho_ref, route_refs):
    xn = _layer_norm(ALPHA * x + (1 + gate) * y, lg_ref[...], lb_ref[...])
    xo_ref[rs, :] = xn
    if ho_ref is None:
        return
    h = xn * (1 + scale) + shift
    ho_ref[rs, :] = h.astype(BF16)
    if route_refs is not None:
        wrt_ref, hp_ref, ri_ref, rw_ref = route_refs
        hp_ref[rs, :] = _pack_bf16_pairs(h)
        _route_top2(h, rs, wrt_ref, ri_ref, rw_ref)


def _finish_dispatch(i, x_ref, y_of, pg, psh, psc, mpg_ref, mpn_ref, gs_ref, shs_ref, scs_ref,
                     lg_ref, lb_ref, xo_ref, ho_ref, route_refs):
    n = ROWS_PER_UNIT // FINISH_SPLIT
    slices = [slice(k * n, (k + 1) * n) for k in range(FINISH_SPLIT)]
    xo_prompt, xo_sample = xo_ref if isinstance(xo_ref, tuple) else (xo_ref, xo_ref)

    @pl.when(i < N_PROMPT_UNITS)
    def _():
        for rs in slices:
            _finish_rows(rs, x_ref[rs, :], y_of(True, rs), mpg_ref[pg:pg + 1, :], mpn_ref[psh:psh + 1, :],
                         mpn_ref[psc:psc + 1, :], lg_ref, lb_ref, xo_prompt, ho_ref, route_refs)

    @pl.when(i >= N_PROMPT_UNITS)
    def _():
        for rs in slices:
            _finish_rows(rs, x_ref[rs, :], y_of(False, rs), gs_ref[rs, :], shs_ref[rs, :], scs_ref[rs, :],
                         lg_ref, lb_ref, xo_sample, ho_ref, route_refs)


def _epilogue_body(*refs, pg, psh, psc, route, project):
    refs = list(refs)
    if project:
        mp_ref, ms_ref, w32_ref, b_ref = refs[:4]
        w_ref = refs.pop()
        refs = refs[4:]
        _cast_weight_once(w32_ref, w_ref, ROWS_PER_UNIT)
        y_of = lambda prompt, rs: _dot((mp_ref if prompt else ms_ref)[rs, :], w_ref[...]) + b_ref[...]
    else:
        y_ref = refs.pop(0)
        y_of = lambda prompt, rs: y_ref[rs, :]
    x_ref, mpg_ref, mpn_ref, gs_ref, shs_ref, scs_ref, lg_ref, lb_ref = refs[:8]
    if route:
        wrt_ref, xo_ref, ho_ref, hp_ref, ri_ref, rw_ref = refs[8:]
        route_refs = (wrt_ref, hp_ref, ri_ref, rw_ref)
    else:
        xo_ref, ho_ref = refs[8:]
        route_refs = None
    _finish_dispatch(pl.program_id(0), x_ref, y_of, pg, psh, psc, mpg_ref, mpn_ref,
                     gs_ref, shs_ref, scs_ref, lg_ref, lb_ref, xo_ref, ho_ref, route_refs)


def _epilogue_specs(layer, next_layer, pg, psh, psc):
    row = pl.BlockSpec((1, D_MODEL), lambda i, *_: (0, 0))
    modp = lambda l: pl.BlockSpec((None, None, 6, D_MODEL), lambda i, *_: (l, _prompt_batch_of_unit(i), 0, 0))
    return [modp(layer), modp(next_layer), _mods_spec(layer, pg), _mods_spec(next_layer, psh),
            _mods_spec(next_layer, psc), row, row]


def _epilogue(x, y, modp, mods, ln_g, ln_b, layer, next_layer, pg, psh, psc, w_router_t=None):
    route = w_router_t is not None
    project = isinstance(y, tuple)
    unit = lambda w: pl.BlockSpec((ROWS_PER_UNIT, w), lambda i: (i, 0))
    if project:
        mix_p, mix_s, w_out, b_out = y
        in_specs = [pl.BlockSpec((ROWS_PER_UNIT, D_MODEL), lambda i: (jnp.minimum(i, N_PROMPT_UNITS - 1), 0)),
                    pl.BlockSpec((M_SAMPLE, D_MODEL), lambda i: (0, 0)),
                    _resident(layer, (D_MODEL, D_MODEL)),
                    pl.BlockSpec((1, D_MODEL), lambda i: (0, 0))]
        args = [mix_p, mix_s, w_out, b_out.reshape(1, D_MODEL)]
        scratch = [pltpu.VMEM((D_MODEL, D_MODEL), BF16)]
    else:
        in_specs = [unit(D_MODEL)]
        args = [y]
        scratch = []
    in_specs += [unit(D_MODEL)] + _epilogue_specs(layer, next_layer, pg, psh, psc)
    args += [x, modp, modp, mods, mods, mods, ln_g.reshape(1, D_MODEL), ln_b.reshape(1, D_MODEL)]
    out_specs = [unit(D_MODEL), unit(D_MODEL)]
    out_shape = [jax.ShapeDtypeStruct((M_ROWS, D_MODEL), F32), jax.ShapeDtypeStruct((M_ROWS, D_MODEL), BF16)]
    if route:
        in_specs.append(pl.BlockSpec((N_EXPERTS, D_MODEL), lambda i: (0, 0)))
        args.append(w_router_t)
        out_specs += [unit(PACKED_WIDTH), unit(TOP_K), unit(TOP_K)]
        out_shape += [jax.ShapeDtypeStruct((M_ROWS, PACKED_WIDTH), jnp.uint32),
                      jax.ShapeDtypeStruct((M_ROWS, TOP_K), jnp.int32),
                      jax.ShapeDtypeStruct((M_ROWS, TOP_K), F32)]
    return pl.pallas_call(
        functools.partial(_epilogue_body, pg=pg, psh=psh, psc=psc, route=route, project=project),
        grid=(N_UNITS,),
        in_specs=in_specs,
        out_specs=out_specs,
        out_shape=out_shape,
        scratch_shapes=scratch,
        compiler_params=_params("arbitrary"),
        name=("outproj_" if project else "") + ("epilogue_route" if route else "epilogue"),
    )(*args)


FF_LANE_BLOCKS = D_FF // 128


def _ffn_prep_body(g_ref, u_ref, d_ref, go_ref, uo_ref, do_ref):
    c = pl.program_id(0)

    @pl.when(c < FF_LANE_BLOCKS)
    def _():
        go_ref[...] = g_ref[...].astype(BF16)
        uo_ref[...] = u_ref[...].astype(BF16)
        do_ref[...] = d_ref[...].astype(BF16)

    @pl.when(c >= FF_LANE_BLOCKS)
    def _():
        go_ref[...] = jnp.zeros_like(go_ref)
        uo_ref[...] = jnp.zeros_like(uo_ref)
        do_ref[...] = jnp.zeros_like(do_ref)


def _ffn_prep(w_gu, w_down, ld):
    src = lambda c: jnp.minimum(c, FF_LANE_BLOCKS - 1)
    return pl.pallas_call(
        _ffn_prep_body,
        grid=(D_FF_PAD // 128,),
        in_specs=[pl.BlockSpec((None, D_MODEL, 128), lambda c: (ld, 0, src(c))),
                  pl.BlockSpec((None, D_MODEL, 128), lambda c: (ld, 0, FF_LANE_BLOCKS + src(c))),
                  pl.BlockSpec((None, 128, D_MODEL), lambda c: (ld, src(c), 0))],
        out_specs=[pl.BlockSpec((D_MODEL, 128), lambda c: (0, c)),
                   pl.BlockSpec((D_MODEL, 128), lambda c: (0, c)),
                   pl.BlockSpec((128, D_MODEL), lambda c: (c, 0))],
        out_shape=[jax.ShapeDtypeStruct((D_MODEL, D_FF_PAD), BF16), jax.ShapeDtypeStruct((D_MODEL, D_FF_PAD), BF16),
                   jax.ShapeDtypeStruct((D_FF_PAD, D_MODEL), BF16)],
        compiler_params=_params("arbitrary"),
        name="ffn_prep",
    )(w_gu, w_gu, w_down)


def _swiglu_rows(x, wg, wu, wd):
    g = _dot(x, wg)
    u = _dot(x, wu)
    return _dot((_silu(g) * u).astype(BF16), wd)


def _ffn_body(h_ref, wg_ref, wu_ref, wd_ref, o_ref):
    i = pl.program_id(0)
    j = pl.program_id(1)
    upt = FFN_ROW_TILE // ROWS_PER_UNIT
    full = (i + 1) * upt <= N_UNITS

    def run(n_units):
        @pl.when(j == 0)
        def _():
            o_ref[:n_units * ROWS_PER_UNIT, :] = jnp.zeros((n_units * ROWS_PER_UNIT, D_MODEL), F32)

        step = min(n_units, UNITS_PER_DOT)
        for r in range(0, n_units, step):
            rows = slice(r * ROWS_PER_UNIT, (r + step) * ROWS_PER_UNIT)
            o_ref[rows, :] += _swiglu_rows(h_ref[rows, :], wg_ref[...], wu_ref[...], wd_ref[...])

    pl.when(full)(functools.partial(run, upt))
    pl.when(jnp.logical_not(full))(functools.partial(run, N_UNITS % upt))


def _ffn(h, wg, wu, wd):
    nj = D_FF_PAD // FFN_TILE
    return pl.pallas_call(
        _ffn_body,
        grid=(pl.cdiv(M_ROWS, FFN_ROW_TILE), nj),
        in_specs=[pl.BlockSpec((FFN_ROW_TILE, D_MODEL), lambda i, j: (i, 0)),
                  pl.BlockSpec((D_MODEL, FFN_TILE), lambda i, j: (0, j)),
                  pl.BlockSpec((D_MODEL, FFN_TILE), lambda i, j: (0, j)),
                  pl.BlockSpec((FFN_TILE, D_MODEL), lambda i, j: (j, 0))],
        out_specs=pl.BlockSpec((FFN_ROW_TILE, D_MODEL), lambda i, j: (i, 0)),
        out_shape=jax.ShapeDtypeStruct((M_ROWS, D_MODEL), F32),
        compiler_params=_params("arbitrary", "arbitrary"),
        name="ffn_dense",
    )(h, wg, wu, wd)


def _row_copy(src_hbm, row, dst, dst_row, sem):
    return pltpu.make_async_copy(src_hbm.at[pl.ds(row, 1), :], dst.at[pl.ds(dst_row, 1), :], sem)


ROW_DMA_UNROLL = 8


def _unit_copy(src, src_unit, dst, dst_unit, sem):
    rows = lambda u: pl.ds(pl.multiple_of(u * ROWS_PER_UNIT, ROWS_PER_UNIT), ROWS_PER_UNIT)
    return pltpu.make_async_copy(src.at[rows(src_unit), :], dst.at[rows(dst_unit), :], sem)


def _moe_dispatch_body(dest_ref, pad_lo_ref, pad_hi_ref, nu_ref, hp_ref, xs_hbm, zbuf, sem, zsem):
    i = pl.program_id(0)

    @pl.when(i == 0)
    def _():
        zbuf[...] = jnp.zeros_like(zbuf)
        zero_row = lambda r: _row_copy(zbuf, 0, xs_hbm, r, zsem)
        zero_unit = lambda u: _unit_copy(zbuf, 0, xs_hbm, u, zsem)

        def over_gaps(act):
            def row_body(r, c):
                act(zero_row(r))
                return c

            def unit_body(u, c):
                act(zero_unit(u))
                return c

            for e in range(N_EXPERTS):
                lax.fori_loop(pad_lo_ref[e], pad_hi_ref[e], row_body, 0)
            lax.fori_loop(nu_ref[0], MOE_UNITS, unit_body, 0)

        over_gaps(lambda cp: cp.start())
        over_gaps(lambda cp: cp.wait())

    base = i * (ROWS_PER_UNIT * TOP_K)

    def start(r, c):
        for k in range(TOP_K):
            _row_copy(hp_ref, r, xs_hbm, dest_ref[base + TOP_K * r + k], sem).start()
        return c

    def wait(r, c):
        for k in range(TOP_K):
            _row_copy(hp_ref, r, xs_hbm, 0, sem).wait()
        return c

    lax.fori_loop(0, ROWS_PER_UNIT, start, 0, unroll=ROW_DMA_UNROLL // TOP_K)
    lax.fori_loop(0, ROWS_PER_UNIT, wait, 0, unroll=ROW_DMA_UNROLL // TOP_K)


def _moe_dispatch(dest, pad_lo, pad_hi, total_units, h_packed):
    grid_spec = pltpu.PrefetchScalarGridSpec(
        num_scalar_prefetch=4,
        grid=(N_UNITS,),
        in_specs=[pl.BlockSpec((ROWS_PER_UNIT, PACKED_WIDTH), lambda i, *_: (i, 0))],
        out_specs=pl.BlockSpec(memory_space=pl.ANY),
        scratch_shapes=[pltpu.VMEM((ROWS_PER_UNIT, PACKED_WIDTH), jnp.uint32),
                        pltpu.SemaphoreType.DMA(()), pltpu.SemaphoreType.DMA(())],
    )
    return pl.pallas_call(
        _moe_dispatch_body,
        grid_spec=grid_spec,
        out_shape=jax.ShapeDtypeStruct((MOE_ROWS, PACKED_WIDTH), jnp.uint32),
        compiler_params=_params("arbitrary"),
        name="moe_dispatch",
    )(dest, pad_lo, pad_hi, total_units, h_packed)


def _moe_gmm_body(te_ref, tu_ref, tn_ref, nu_ref, xs_hbm, wg_ref, wu_ref, wd_ref, ys_hbm,
                  xsc, acc, wgu_b, wd_b, stage, sem_in, sem_out):
    t = pl.program_id(0)
    j = pl.program_id(1)
    nj = pl.num_programs(1)
    n = tn_ref[t]
    u0 = tu_ref[t]
    tf = MOE_FF_TILE

    @pl.when((j == 0) & (n > 0))
    def _():
        unit_in = lambda u, s: pltpu.make_async_copy(
            xs_hbm.at[pl.ds(pl.multiple_of((u0 + u) * ROWS_PER_UNIT, ROWS_PER_UNIT), ROWS_PER_UNIT), :],
            stage.at[s], sem_in.at[s])
        unit_in(0, 0).start()

        def load(u, c):
            s = u & 1

            @pl.when(u + 1 < n)
            def _():
                unit_in(u + 1, 1 - s).start()

            rows = pl.ds(pl.multiple_of(u * ROWS_PER_UNIT, ROWS_PER_UNIT), ROWS_PER_UNIT)
            unit_in(u, s).wait()
            lo, hi = _unpack_bf16_pairs(stage[s])
            xsc[rows, :PACKED_WIDTH] = lo
            xsc[rows, PACKED_WIDTH:] = hi
            return c

        lax.fori_loop(0, n, load, 0)

    t_prev = jnp.maximum(t - 1, 0)

    @pl.when((j == 0) & (t > 0) & (tn_ref[t_prev] > 0))
    def _():
        def wait(u, c):
            _unit_copy(acc, u, ys_hbm, u, sem_out).wait()
            return c

        lax.fori_loop(0, tn_ref[t_prev], wait, 0)

    @pl.when((j == 0) & (n > 0))
    def _():
        def zero(u, c):
            acc[pl.ds(pl.multiple_of(u * ROWS_PER_UNIT, ROWS_PER_UNIT), ROWS_PER_UNIT), :] = jnp.zeros(
                (ROWS_PER_UNIT, D_MODEL), F32)
            return c

        lax.fori_loop(0, n, zero, 0)

    @pl.when((j == 0) & (t == 0))
    def _():
        def start(u, c):
            _unit_copy(acc, 0, ys_hbm, u, sem_out).start()
            return c

        def wait(u, c):
            _unit_copy(acc, 0, ys_hbm, u, sem_out).wait()
            return c

        lax.fori_loop(nu_ref[0], MOE_UNITS, start, 0)
        lax.fori_loop(nu_ref[0], MOE_UNITS, wait, 0)

    @pl.when(n > 0)
    def _():
        wgu_b[:, :tf] = wg_ref[...].astype(BF16)
        wgu_b[:, tf:] = wu_ref[...].astype(BF16)
        wd_b[...] = wd_ref[...].astype(BF16)

        def unit_rows(r0, units):
            rs = pl.ds(r0, units * ROWS_PER_UNIT)
            gu = _dot(xsc[rs, :], wgu_b[...])
            a = (_silu(gu[:, :tf]) * gu[:, tf:]).astype(BF16)
            acc[rs, :] += _dot(a, wd_b[...])

        def units_from(u_first, k):
            step = min(k, UNITS_PER_DOT)
            for h in range(0, k, step):
                unit_rows(pl.multiple_of((u_first + h) * ROWS_PER_UNIT, ROWS_PER_UNIT), step)

        for b in reversed(range(MOE_TILE_UNITS.bit_length())):
            size = 1 << b
            done = lax.shift_left(lax.shift_right_logical(n, b + 1), b + 1)
            pl.when((n & size) != 0)(functools.partial(units_from, done, size))

    @pl.when((j == nj - 1) & (n > 0))
    def _():
        def start(u, c):
            _unit_copy(acc, u, ys_hbm, u0 + u, sem_out).start()
            return c

        def wait(u, c):
            _unit_copy(acc, u, ys_hbm, u0 + u, sem_out).wait()
            return c

        lax.fori_loop(0, n, start, 0)

        @pl.when(t == pl.num_programs(0) - 1)
        def _():
            lax.fori_loop(0, n, wait, 0)


def _moe_gmm(tile_expert, tile_unit0, tile_nunits, total_units, xs, w_gu, w_down, lm):
    nj = D_FF_EXPERT // MOE_FF_TILE
    jj = lambda t, j, tn: jnp.where(tn[t] > 0, j, nj - 1)
    tile_rows = MOE_TILE_UNITS * ROWS_PER_UNIT
    grid_spec = pltpu.PrefetchScalarGridSpec(
        num_scalar_prefetch=4,
        grid=(MOE_TILES, nj),
        in_specs=[pl.BlockSpec(memory_space=pl.ANY),
                  pl.BlockSpec((None, None, D_MODEL, MOE_FF_TILE),
                               lambda t, j, te, tu, tn, nu: (lm, te[t], 0, jj(t, j, tn))),
                  pl.BlockSpec((None, None, D_MODEL, MOE_FF_TILE),
                               lambda t, j, te, tu, tn, nu: (lm, te[t], 0, nj + jj(t, j, tn))),
                  pl.BlockSpec((None, None, MOE_FF_TILE, D_MODEL),
                               lambda t, j, te, tu, tn, nu: (lm, te[t], jj(t, j, tn), 0))],
        out_specs=pl.BlockSpec(memory_space=pl.ANY),
        scratch_shapes=[pltpu.VMEM((tile_rows, D_MODEL), BF16), pltpu.VMEM((tile_rows, D_MODEL), F32),
                        pltpu.VMEM((D_MODEL, 2 * MOE_FF_TILE), BF16), pltpu.VMEM((MOE_FF_TILE, D_MODEL), BF16),
                        pltpu.VMEM((2, ROWS_PER_UNIT, PACKED_WIDTH), jnp.uint32),
                        pltpu.SemaphoreType.DMA((2,)), pltpu.SemaphoreType.DMA(())],
    )
    return pl.pallas_call(
        _moe_gmm_body,
        grid_spec=grid_spec,
        out_shape=jax.ShapeDtypeStruct((MOE_ROWS, D_MODEL), F32),
        compiler_params=_params("arbitrary", "arbitrary"),
        name="moe_gmm",
    )(tile_expert, tile_unit0, tile_nunits, total_units, xs, w_gu, w_gu, w_down)


def _moe_combine_body(dest_ref, ys_hbm, rw_ref, x_ref, mpg_ref, mpn_ref, gs_ref, shs_ref, scs_ref, lg_ref, lb_ref,
                      *rest, pg, psh, psc, final):
    if final:
        xo_prompt, xo_sample, buf, sem = rest
        xo_ref, ho_ref = (xo_prompt, xo_sample), None
    else:
        xo_ref, ho_ref, buf, sem = rest
    i = pl.program_id(0)
    slot = i & 1

    def issue(unit, s):
        base = unit * (ROWS_PER_UNIT * TOP_K)

        def start(r, c):
            for k in range(TOP_K):
                _row_copy(ys_hbm, dest_ref[base + TOP_K * r + k], buf.at[s, k], r, sem.at[s]).start()
            return c

        lax.fori_loop(0, ROWS_PER_UNIT, start, 0, unroll=ROW_DMA_UNROLL // TOP_K)

    @pl.when(i == 0)
    def _():
        issue(0, 0)

    @pl.when(i + 1 < pl.num_programs(0))
    def _():
        issue(i + 1, 1 - slot)

    def wait(r, c):
        for k in range(TOP_K):
            _row_copy(ys_hbm, 0, buf.at[slot, k], r, sem.at[slot]).wait()
        return c

    lax.fori_loop(0, ROWS_PER_UNIT, wait, 0, unroll=ROW_DMA_UNROLL // TOP_K)
    y_of = lambda prompt, rs: rw_ref[rs, 0:1] * buf[slot, 0, rs, :] + rw_ref[rs, 1:2] * buf[slot, 1, rs, :]
    _finish_dispatch(i, x_ref, y_of, pg, psh, psc, mpg_ref, mpn_ref, gs_ref, shs_ref, scs_ref,
                     lg_ref, lb_ref, xo_ref, ho_ref, None)


def _moe_combine(dest, ys, route_w, x, modp, mods, ln_g, ln_b, layer, next_layer, pg, psh, psc, final):
    unit = lambda w: pl.BlockSpec((ROWS_PER_UNIT, w), lambda i, d: (i, 0))
    if final:
        out_specs = [_prompt_unit_spec(D_MODEL), _sample_rows_spec(D_MODEL)]
        out_shape = [jax.ShapeDtypeStruct((M_PROMPT, D_MODEL), F32), jax.ShapeDtypeStruct((M_SAMPLE, D_MODEL), F32)]
    else:
        out_specs = [unit(D_MODEL), unit(D_MODEL)]
        out_shape = [jax.ShapeDtypeStruct((M_ROWS, D_MODEL), F32), jax.ShapeDtypeStruct((M_ROWS, D_MODEL), BF16)]
    grid_spec = pltpu.PrefetchScalarGridSpec(
        num_scalar_prefetch=1,
        grid=(N_UNITS,),
        in_specs=[pl.BlockSpec(memory_space=pl.ANY), unit(TOP_K), unit(D_MODEL)]
        + _epilogue_specs(layer, next_layer, pg, psh, psc),
        out_specs=out_specs,
        scratch_shapes=[pltpu.VMEM((2, TOP_K, ROWS_PER_UNIT, D_MODEL), F32), pltpu.SemaphoreType.DMA((2,))],
    )
    return pl.pallas_call(
        functools.partial(_moe_combine_body, pg=pg, psh=psh, psc=psc, final=final),
        grid_spec=grid_spec,
        out_shape=out_shape,
        compiler_params=_params("arbitrary"),
        name="moe_combine",
    )(dest, ys, route_w, x, modp, modp, mods, mods, mods, ln_g.reshape(1, D_MODEL), ln_b.reshape(1, D_MODEL))


def _moe_plan(route_i):
    i32 = jnp.int32
    e_flat = route_i.reshape(-1)
    onehot = (e_flat[:, None] == jnp.arange(N_EXPERTS, dtype=i32)[None, :]).astype(i32)
    csum = jnp.cumsum(onehot, axis=0)
    rank = jnp.sum(csum * onehot, axis=1) - 1
    counts = csum[-1]
    units_e = (counts + ROWS_PER_UNIT - 1) // ROWS_PER_UNIT
    unit_end = jnp.cumsum(units_e)
    unit_start = unit_end - units_e
    dest = (unit_start[e_flat] * ROWS_PER_UNIT + rank).astype(i32)
    pad_lo = (unit_start * ROWS_PER_UNIT + counts).astype(i32)
    pad_hi = (unit_end * ROWS_PER_UNIT).astype(i32)
    total_units = unit_end[-1:].astype(i32)
    tiles_e = (units_e + MOE_TILE_UNITS - 1) // MOE_TILE_UNITS
    tile_end = jnp.cumsum(tiles_e)
    tile_start = tile_end - tiles_e
    n_tiles = tile_end[-1]
    t_ids = jnp.arange(MOE_TILES, dtype=i32)
    t_eff = jnp.minimum(t_ids, n_tiles - 1)
    tile_expert = jnp.sum((t_eff[:, None] >= tile_end[None, :]).astype(i32), axis=1)
    k = t_eff - tile_start[tile_expert]
    tile_unit0 = unit_start[tile_expert] + k * MOE_TILE_UNITS
    units_left = units_e[tile_expert] - k * MOE_TILE_UNITS
    tile_nunits = jnp.where(t_ids < n_tiles, jnp.minimum(units_left, MOE_TILE_UNITS), 0)
    return (dest, pad_lo, pad_hi, total_units, tile_expert.astype(i32), tile_unit0.astype(i32),
            tile_nunits.astype(i32))


def kernel(x_prompt, x_sample, cache_win_k, cache_win_v, c_prompt, c_sample, w_ada, b_ada, w_in, b_in, v_norm_g, v_norm_b, w_spatial, b_spatial, attn_sinks, w_out, b_out, ln1_g, ln1_b, ln2_g, ln2_b, w_ffn_gu, w_ffn_down, w_router, w_exp_gu, w_exp_down):
    c_rows = jnp.concatenate([c_prompt, c_sample, jnp.zeros((ADA_ROWS - BATCH - DEC_BATCH, D_MODEL), F32)], axis=0)
    mod = _adaln(c_rows, w_ada, b_ada)
    modp = mod[:, :BATCH].reshape(DEPTH, BATCH, 6, D_MODEL)
    mods = jnp.repeat(mod[:, BATCH:BATCH + DEC_BATCH], DEC_SEQ, axis=1)

    assert DEPTH % 2 == 0, "the last layer is a mixture-of-experts layer; its combine kernel emits the outputs"
    h, x = _modulate0(x_prompt.reshape(M_PROMPT, D_MODEL), x_sample.reshape(M_SAMPLE, D_MODEL), modp, mods)

    rows = GQA * DEC_SEQ
    kp_list, vp_list, ks_list, vs_list, cv_list = [], [], [], [], []
    for l in range(DEPTH):
        nl = min(l + 1, DEPTH - 1)
        u, v, q, kv = _proj(h, w_in, l, b_in[l], v_norm_g[l], v_norm_b[l])
        b_s_t = b_spatial[l].T
        mix_p = _mix_prompt(attn_sinks[l].reshape(-1), u, v, q, kv, w_spatial[l], b_s_t)

        u_s = u[M_PROMPT:].reshape(DEC_BATCH, DEC_SEQ, A_WIDTH)
        v_s = v[M_PROMPT:].reshape(DEC_BATCH, DEC_SEQ, A_WIDTH)
        kv_s = kv[M_PROMPT:].reshape(DEC_BATCH, DEC_SEQ, 2 * KV_WIDTH)
        q_s = q[M_PROMPT:].reshape(DEC_BATCH, DEC_SEQ, N_KV_HEADS, GQA, HEAD_DIM)
        q_s = q_s.transpose(0, 2, 3, 1, 4).reshape(DEC_BATCH, N_KV_HEADS, rows, HEAD_DIM)
        ck = cache_win_k[l].reshape(DEC_BATCH, WINDOW, KV_WIDTH)
        cv = cache_win_v[l].reshape(DEC_BATCH, WINDOW, KV_WIDTH)
        sink_rows = jnp.repeat(attn_sinks[l], DEC_SEQ, axis=1).reshape(N_KV_HEADS, rows, 1)
        a_s, o_s = _mix_sample(u_s, v_s, q_s, kv_s, ck, cv, w_spatial[l], b_s_t, sink_rows)
        o_s = o_s.reshape(DEC_BATCH, N_KV_HEADS, GQA, DEC_SEQ, HEAD_DIM).transpose(0, 3, 1, 2, 4)
        mix_s = jnp.concatenate([a_s.reshape(M_SAMPLE, A_WIDTH), o_s.reshape(M_SAMPLE, B_WIDTH)], axis=1)

        y = (mix_p, mix_s, w_out, b_out[l])
        if l % 2 == 0:
            x, h = _epilogue(x, y, modp, mods, ln1_g[l], ln1_b[l], l, l, P_GATE1, P_SHIFT2, P_SCALE2)
            ld = l // 2
            wg, wu, wd = _ffn_prep(w_ffn_gu, w_ffn_down, ld)
            f = _ffn(h, wg, wu, wd)
            x, h = _epilogue(x, f, modp, mods, ln2_g[l], ln2_b[l], l, nl, P_GATE2, P_SHIFT1, P_SCALE1)
        else:
            lm = l // 2
            x, h, h_packed, route_i, route_w = _epilogue(x, y, modp, mods, ln1_g[l], ln1_b[l], l, l,
                                                      P_GATE1, P_SHIFT2, P_SCALE2, w_router_t=w_router[lm].T)
            dest, pad_lo, pad_hi, total_units, tile_expert, tile_unit0, tile_nunits = _moe_plan(route_i)
            xs = _moe_dispatch(dest, pad_lo, pad_hi, total_units, h_packed)
            ys = _moe_gmm(tile_expert, tile_unit0, tile_nunits, total_units, xs, w_exp_gu, w_exp_down, lm)
            x, h = _moe_combine(dest, ys, route_w, x, modp, mods, ln2_g[l], ln2_b[l], l, nl,
                                P_GATE2, P_SHIFT1, P_SCALE1, final=(l == DEPTH - 1))

        kv_p = kv[:M_PROMPT].reshape(BATCH, SEQ, 2 * KV_WIDTH)[:, SEQ - WINDOW:]
        kp_list.append(kv_p[..., :KV_WIDTH].reshape(BATCH, WINDOW, N_KV_HEADS, HEAD_DIM))
        vp_list.append(kv_p[..., KV_WIDTH:].reshape(BATCH, WINDOW, N_KV_HEADS, HEAD_DIM))
        k_new = kv_s[..., :KV_WIDTH].reshape(DEC_BATCH, DEC_SEQ, N_KV_HEADS, HEAD_DIM)
        v_new = kv_s[..., KV_WIDTH:].reshape(DEC_BATCH, DEC_SEQ, N_KV_HEADS, HEAD_DIM)
        ks_list.append(jnp.concatenate([cache_win_k[l][:, DEC_SEQ:], k_new], axis=1))
        vs_list.append(jnp.concatenate([cache_win_v[l][:, DEC_SEQ:], v_new], axis=1))
        cv_list.append(v_s.astype(F32))

    y_prompt = x.reshape(BATCH, SEQ, D_MODEL)
    y_sample = h.reshape(DEC_BATCH, DEC_SEQ, D_MODEL)
    return (y_prompt, y_sample, jnp.stack(kp_list), jnp.stack(vp_list), jnp.stack(ks_list), jnp.stack(vs_list),
            jnp.stack(cv_list))
```

```python
import functools

import jax
import jax.numpy as jnp
from jax import lax
from jax.experimental import pallas as pl
from jax.experimental.pallas import tpu as pltpu

F32 = jnp.float32
BF16 = jnp.bfloat16

D_MODEL = 2048
BATCH = 4
SEQ = 2048
DEPTH = 4
DEC_BATCH = 32
DEC_SEQ = 8
CHUNK = 128
A_WIDTH = D_MODEL // 2
A_GROUPS = 4
A_GROUP_DIM = A_WIDTH // A_GROUPS
HEAD_DIM = 64
B_WIDTH = D_MODEL - A_WIDTH
N_HEADS = B_WIDTH // HEAD_DIM
N_KV_HEADS = 2
GQA = N_HEADS // N_KV_HEADS
KV_WIDTH = N_KV_HEADS * HEAD_DIM
WINDOW = 128
IN_COLS = 2 * A_WIDTH + B_WIDTH + 2 * KV_WIDTH
D_FF = 5504
N_EXPERTS = 8
TOP_K = 2
D_FF_EXPERT = 7168
ALPHA = (2.0 * DEPTH) ** 0.25
LN_EPS = 1e-5
NEG_INF = -1e30

M_PROMPT = BATCH * SEQ
M_SAMPLE = DEC_BATCH * DEC_SEQ
M_ROWS = M_PROMPT + M_SAMPLE
ROWS_PER_UNIT = 256
N_PROMPT_UNITS = M_PROMPT // ROWS_PER_UNIT
N_UNITS = M_ROWS // ROWS_PER_UNIT
UNITS_PER_SEQ = SEQ // ROWS_PER_UNIT
ADA_ROWS = 40
P_SHIFT1, P_SCALE1, P_GATE1, P_SHIFT2, P_SCALE2, P_GATE2 = range(6)

D_FF_PAD = 5632
FFN_TILE = 512
FFN_ROW_TILE = 4 * ROWS_PER_UNIT
MOE_FF_TILE = 256
N_PAIRS = M_ROWS * TOP_K
MOE_UNITS = N_PAIRS // ROWS_PER_UNIT + N_EXPERTS
MOE_ROWS = MOE_UNITS * ROWS_PER_UNIT
MOE_TILE_UNITS = 9
UNITS_PER_DOT = 2
FULL_TILE_UNITS_PER_DOT = 3
MOE_TILES =MOE_UNITS // MOE_TILE_UNITS + N_EXPERTS

VMEM_LIMIT = 56 * 1024 * 1024


def _params(*sem):
    return pltpu.CompilerParams(dimension_semantics=sem, vmem_limit_bytes=VMEM_LIMIT)


def _dot(a, b):
    return jnp.dot(a, b, preferred_element_type=F32)


def _layer_norm(z, g, b):
    mu = jnp.mean(z, axis=-1, keepdims=True)
    zc = z - mu
    var = jnp.mean(zc * zc, axis=-1, keepdims=True)
    return zc * lax.rsqrt(var + LN_EPS) * g + b


def _gelu_tanh(x):
    return x * (0.5 * (1.0 + jnp.tanh(0.7978845608028654 * (x + 0.044715 * (x * x * x)))))


def _silu(x):
    return x * jax.nn.sigmoid(x)


def _prompt_batch_of_unit(i):
    return jnp.minimum(i // UNITS_PER_SEQ, BATCH - 1)


def _adaln_body(c_ref, w_ref, b_ref, o_ref):
    s = _silu(c_ref[...]).astype(BF16)
    o_ref[...] = _dot(s, w_ref[...].astype(BF16)) + b_ref[...]


def _adaln(c_rows, w_ada, b_ada):
    tn = 1024
    return pl.pallas_call(
        _adaln_body,
        grid=(DEPTH, 6 * D_MODEL // tn),
        in_specs=[pl.BlockSpec((ADA_ROWS, D_MODEL), lambda l, n: (0, 0)),
                  pl.BlockSpec((None, D_MODEL, tn), lambda l, n: (l, 0, n)),
                  pl.BlockSpec((None, 1, tn), lambda l, n: (l, 0, n))],
        out_specs=pl.BlockSpec((None, ADA_ROWS, tn), lambda l, n: (l, 0, n)),
        out_shape=jax.ShapeDtypeStruct((DEPTH, ADA_ROWS, 6 * D_MODEL), F32),
        compiler_params=_params("arbitrary", "arbitrary"),
        name="adaln",
    )(c_rows, w_ada, b_ada.reshape(DEPTH, 1, 6 * D_MODEL))


def _mod0_body(xp_ref, xs_ref, mp_ref, ss_ref, cs_ref, h_ref, x_ref):
    i = pl.program_id(0)

    @pl.when(i < N_PROMPT_UNITS)
    def _():
        x = xp_ref[...]
        x_ref[...] = x
        h_ref[...] = (x * (1 + mp_ref[P_SCALE1:P_SCALE1 + 1, :]) + mp_ref[P_SHIFT1:P_SHIFT1 + 1, :]).astype(BF16)

    @pl.when(i >= N_PROMPT_UNITS)
    def _():
        x = xs_ref[...]
        x_ref[...] = x
        h_ref[...] = (x * (1 + cs_ref[...]) + ss_ref[...]).astype(BF16)


def _unit_spec(width):
    return pl.BlockSpec((ROWS_PER_UNIT, width), lambda i: (i, 0))


def _modp_spec(layer):
    return pl.BlockSpec((None, None, 6, D_MODEL), lambda i: (layer, _prompt_batch_of_unit(i), 0, 0))


def _mods_spec(layer, p):
    return pl.BlockSpec((None, ROWS_PER_UNIT, D_MODEL), lambda i, *_: (layer, 0, p))


def _prompt_unit_spec(width):
    return pl.BlockSpec((ROWS_PER_UNIT, width), lambda i, *_: (jnp.minimum(i, N_PROMPT_UNITS - 1), 0))


def _sample_rows_spec(width):
    return pl.BlockSpec((M_SAMPLE, width), lambda i, *_: (0, 0))


def _modulate0(x_prompt, x_sample, modp, mods):
    return pl.pallas_call(
        _mod0_body,
        grid=(N_UNITS,),
        in_specs=[_prompt_unit_spec(D_MODEL), _sample_rows_spec(D_MODEL), _modp_spec(0),
                  _mods_spec(0, P_SHIFT1), _mods_spec(0, P_SCALE1)],
        out_specs=[_unit_spec(D_MODEL), _unit_spec(D_MODEL)],
        out_shape=[jax.ShapeDtypeStruct((M_ROWS, D_MODEL), BF16), jax.ShapeDtypeStruct((M_ROWS, D_MODEL), F32)],
        compiler_params=_params("arbitrary"),
        name="modulate0",
    )(x_prompt, x_sample, modp, mods, mods)


def _cast_weight_once(w32_ref, w_ref, chunk):
    @pl.when(pl.program_id(0) == 0)
    def _():
        for c in range(0, w32_ref.shape[1], chunk):
            w_ref[:, c:c + chunk] = w32_ref[:, c:c + chunk].astype(BF16)


def _resident(layer, shape):
    return pl.BlockSpec((None,) + shape, lambda i: (layer,) + (0,) * len(shape), pipeline_mode=pl.Buffered(1))


def _proj_body(h_ref, w32_ref, b_ref, gv_ref, bv_ref, u_ref, v_ref, q_ref, kv_ref, w_ref):
    _cast_weight_once(w32_ref, w_ref, A_GROUP_DIM)
    h = h_ref[...]
    gd = A_GROUP_DIM
    for c in range(A_GROUPS):
        sl = slice(c * gd, (c + 1) * gd)
        z = _dot(h, w_ref[:, sl]) + b_ref[:, sl]
        u_ref[:, sl] = _gelu_tanh(z).astype(BF16)
    for g in range(A_GROUPS):
        sl = slice(A_WIDTH + g * gd, A_WIDTH + (g + 1) * gd)
        z = _gelu_tanh(_dot(h, w_ref[:, sl]) + b_ref[:, sl])
        vn = _layer_norm(z, gv_ref[g:g + 1, :], bv_ref[g:g + 1, :])
        v_ref[:, g * gd:(g + 1) * gd] = vn.astype(BF16)
    o1 = 2 * A_WIDTH
    o2 = o1 + B_WIDTH
    z = _dot(h, w_ref[:, o1:o2]) + b_ref[:, o1:o2]
    q_ref[...] = (z * (HEAD_DIM ** -0.5)).astype(BF16)
    kv_ref[...] = _dot(h, w_ref[:, o2:]) + b_ref[:, o2:]


def _proj(h, w_all, layer, b, gv, bv):
    const = lambda shape: pl.BlockSpec(shape, lambda i: (0,) * len(shape))
    return pl.pallas_call(
        _proj_body,
        grid=(N_UNITS,),
        in_specs=[_unit_spec(D_MODEL), _resident(layer, (D_MODEL, IN_COLS)), const((1, IN_COLS)),
                  const((A_GROUPS, A_GROUP_DIM)), const((A_GROUPS, A_GROUP_DIM))],
        out_specs=[_unit_spec(A_WIDTH), _unit_spec(A_WIDTH), _unit_spec(B_WIDTH), _unit_spec(2 * KV_WIDTH)],
        out_shape=[jax.ShapeDtypeStruct((M_ROWS, A_WIDTH), BF16), jax.ShapeDtypeStruct((M_ROWS, A_WIDTH), BF16),
                   jax.ShapeDtypeStruct((M_ROWS, B_WIDTH), BF16), jax.ShapeDtypeStruct((M_ROWS, 2 * KV_WIDTH), F32)],
        scratch_shapes=[pltpu.VMEM((D_MODEL, IN_COLS), BF16)],
        compiler_params=_params("arbitrary"),
        name="proj",
    )(h, w_all, b.reshape(1, IN_COLS), gv, bv)


def _mix_prompt_body(sink_ref, u_ref, v_ref, q_ref, kvc_ref, kvp_ref, ws_ref, bs_ref, o_ref):
    blk = pl.program_id(1)
    gd = A_GROUP_DIM
    tr = lax.broadcasted_iota(jnp.int32, (CHUNK, CHUNK), 0)
    tc = lax.broadcasted_iota(jnp.int32, (CHUNK, CHUNK), 1)
    for g in range(A_GROUPS):
        sl = slice(g * gd, (g + 1) * gd)
        w = jnp.where(tr >= tc, ws_ref[g], 0.0).astype(BF16)
        s = _dot(w, v_ref[:, sl]) + bs_ref[:, g:g + 1]
        o_ref[:, sl] = (u_ref[:, sl].astype(F32) * s).astype(BF16)

    kvc = kvc_ref[...]
    kvp = kvp_ref[...]
    row = lax.broadcasted_iota(jnp.int32, (WINDOW, 2 * WINDOW), 0)
    col = lax.broadcasted_iota(jnp.int32, (WINDOW, 2 * WINDOW), 1)
    dlt = row + WINDOW - col
    mask = (dlt >= 0) & (dlt <= WINDOW) & ((col >= WINDOW) | (blk > 0))
    hd = HEAD_DIM
    for j in range(N_KV_HEADS):
        ks = slice(j * hd, (j + 1) * hd)
        vs = slice(KV_WIDTH + j * hd, KV_WIDTH + (j + 1) * hd)
        kk = jnp.concatenate([kvp[:, ks], kvc[:, ks]], axis=0).astype(BF16)
        vv = jnp.concatenate([kvp[:, vs], kvc[:, vs]], axis=0).astype(BF16)
        for hh in range(GQA):
            h = j * GQA + hh
            qh = q_ref[:, h * hd:(h + 1) * hd]
            s = lax.dot_general(qh, kk, (((1,), (1,)), ((), ())), preferred_element_type=F32)
            s = jnp.where(mask, s, NEG_INF)
            sk = sink_ref[h]
            m = jnp.maximum(jnp.max(s, axis=-1, keepdims=True), sk)
            p = jnp.exp(s - m)
            den = jnp.sum(p, axis=-1, keepdims=True) + jnp.exp(sk - m)
            o = _dot(p.astype(BF16), vv) / den
            o_ref[:, A_WIDTH + h * hd:A_WIDTH + (h + 1) * hd] = o.astype(BF16)


def _mix_prompt(sinks, u, v, q, kv, w_s, b_s_t):
    nb = SEQ // WINDOW
    row_blk = lambda w: pl.BlockSpec((WINDOW, w), lambda b, i, s: (b * nb + i, 0))
    grid_spec = pltpu.PrefetchScalarGridSpec(
        num_scalar_prefetch=1,
        grid=(BATCH, nb),
        in_specs=[row_blk(A_WIDTH), row_blk(A_WIDTH), row_blk(B_WIDTH), row_blk(2 * KV_WIDTH),
                  pl.BlockSpec((WINDOW, 2 * KV_WIDTH), lambda b, i, s: (b * nb + jnp.maximum(i - 1, 0), 0)),
                  pl.BlockSpec((A_GROUPS, CHUNK, CHUNK), lambda b, i, s: (0, 0, 0)),
                  pl.BlockSpec((CHUNK, A_GROUPS), lambda b, i, s: (0, 0))],
        out_specs=pl.BlockSpec((WINDOW, D_MODEL), lambda b, i, s: (b * nb + i, 0)),
    )
    return pl.pallas_call(
        _mix_prompt_body,
        grid_spec=grid_spec,
        out_shape=jax.ShapeDtypeStruct((M_PROMPT, D_MODEL), BF16),
        compiler_params=_params("arbitrary", "arbitrary"),
        name="mix_prompt",
    )(sinks, u, v, q, kv, kv, w_s, b_s_t)


SAMPLE_KEYS = 2 * WINDOW


SAMPLE_BATCH_BLOCK = 8


def _mix_sample_body(u_ref, v_ref, q_ref, kn_ref, ck_ref, cv_ref, ws_ref, bs_ref, sk_ref, a_ref, o_ref):
    for b in range(SAMPLE_BATCH_BLOCK):
        _mix_sample_one(u_ref.at[b], v_ref.at[b], q_ref.at[b], kn_ref.at[b], ck_ref.at[b], cv_ref.at[b],
                        ws_ref, bs_ref, sk_ref, a_ref.at[b], o_ref.at[b])


def _mix_sample_one(u_ref, v_ref, q_ref, kn_ref, ck_ref, cv_ref, ws_ref, bs_ref, sk_ref, a_ref, o_ref):
    gd = A_GROUP_DIM
    ds = DEC_SEQ
    v = v_ref[...].astype(F32)
    u = u_ref[...].astype(F32)
    tr = lax.broadcasted_iota(jnp.int32, (ds, ds), 0)
    tc = lax.broadcasted_iota(jnp.int32, (ds, ds), 1)
    for g in range(A_GROUPS):
        sl = slice(g * gd, (g + 1) * gd)
        w = jnp.where(tr >= tc, ws_ref[g][:ds, :ds], 0.0)
        s = jnp.zeros((ds, gd), F32) + bs_ref[:ds, g:g + 1]
        for t in range(ds):
            s = s + w[:, t:t + 1] * v[t:t + 1, sl]
        a_ref[:, sl] = (u[:, sl] * s).astype(BF16)

    kn = kn_ref[...]
    ck = ck_ref[...]
    cv = cv_ref[...]
    hd = HEAD_DIM
    rows = GQA * ds
    qt = lax.broadcasted_iota(jnp.int32, (rows, SAMPLE_KEYS), 0) & (ds - 1)
    col = lax.broadcasted_iota(jnp.int32, (rows, SAMPLE_KEYS), 1)
    mask = ((col < WINDOW) & (col >= qt)) | ((col >= WINDOW) & (col - WINDOW <= qt))
    pad = jnp.zeros((SAMPLE_KEYS - WINDOW - ds, hd), F32)
    for j in range(N_KV_HEADS):
        ks = slice(j * hd, (j + 1) * hd)
        vs = slice(KV_WIDTH + j * hd, KV_WIDTH + (j + 1) * hd)
        kk = jnp.concatenate([ck[:, ks], kn[:, ks], pad], axis=0).astype(BF16)
        vv = jnp.concatenate([cv[:, ks], kn[:, vs], pad], axis=0).astype(BF16)
        s = lax.dot_general(q_ref[j], kk, (((1,), (1,)), ((), ())), preferred_element_type=F32)
        s = jnp.where(mask, s, NEG_INF)
        sk = sk_ref[j]
        m = jnp.maximum(jnp.max(s, axis=-1, keepdims=True), sk)
        p = jnp.exp(s - m)
        den = jnp.sum(p, axis=-1, keepdims=True) + jnp.exp(sk - m)
        o_ref[j] = (_dot(p.astype(BF16), vv) / den).astype(BF16)


def _mix_sample(u_s, v_s, q_s, kv_s, cache_k, cache_v, w_s, b_s_t, sink_rows):
    rows = GQA * DEC_SEQ
    per_b = lambda *shape: pl.BlockSpec((SAMPLE_BATCH_BLOCK,) + shape, lambda b: (b,) + (0,) * len(shape))
    const = lambda *shape: pl.BlockSpec(shape, lambda b: (0,) * len(shape))
    return pl.pallas_call(
        _mix_sample_body,
        grid=(DEC_BATCH // SAMPLE_BATCH_BLOCK,),
        in_specs=[per_b(DEC_SEQ, A_WIDTH), per_b(DEC_SEQ, A_WIDTH), per_b(N_KV_HEADS, rows, HEAD_DIM),
                  per_b(DEC_SEQ, 2 * KV_WIDTH), per_b(WINDOW, KV_WIDTH), per_b(WINDOW, KV_WIDTH),
                  const(A_GROUPS, CHUNK, CHUNK), const(CHUNK, A_GROUPS), const(N_KV_HEADS, rows, 1)],
        out_specs=[per_b(DEC_SEQ, A_WIDTH), per_b(N_KV_HEADS, rows, HEAD_DIM)],
        out_shape=[jax.ShapeDtypeStruct((DEC_BATCH, DEC_SEQ, A_WIDTH), BF16),
                   jax.ShapeDtypeStruct((DEC_BATCH, N_KV_HEADS, rows, HEAD_DIM), BF16)],
        compiler_params=_params("arbitrary"),
        name="mix_sample",
    )(u_s, v_s, q_s, kv_s, cache_k, cache_v, w_s, b_s_t, sink_rows)


PACKED_WIDTH = D_MODEL // 2


def _pack_bf16_pairs(h):
    bits = lambda v: lax.bitcast_convert_type(v.astype(BF16).astype(F32), jnp.uint32)
    lo = lax.shift_right_logical(bits(h[:, :PACKED_WIDTH]), jnp.uint32(16))
    hi = bits(h[:, PACKED_WIDTH:]) & jnp.uint32(0xFFFF0000)
    return lo | hi


def _unpack_bf16_pairs(w):
    lo = lax.bitcast_convert_type(lax.shift_left(w, jnp.uint32(16)), F32).astype(BF16)
    hi = lax.bitcast_convert_type(w & jnp.uint32(0xFFFF0000), F32).astype(BF16)
    return lo, hi


def _route_top2(h, rs, wrt_ref, ri_ref, rw_ref):
    logits = [jnp.sum(h * wrt_ref[e:e + 1, :], axis=-1, keepdims=True) for e in range(N_EXPERTS)]
    m1 = logits[0]
    i1 = jnp.zeros_like(m1, dtype=jnp.int32)
    for e in range(1, N_EXPERTS):
        gt = logits[e] > m1
        m1 = jnp.where(gt, logits[e], m1)
        i1 = jnp.where(gt, e, i1)
    m2 = jnp.full_like(m1, -jnp.inf)
    i2 = jnp.zeros_like(i1)
    for e in range(N_EXPERTS):
        ok = (i1 != e) & (logits[e] > m2)
        m2 = jnp.where(ok, logits[e], m2)
        i2 = jnp.where(ok, e, i2)
    w1 = 1.0 / (1.0 + jnp.exp(m2 - m1))
    ri_ref[rs, 0:1] = i1
    ri_ref[rs, 1:2] = i2
    rw_ref[rs, 0:1] = w1
    rw_ref[rs, 1:2] = 1.0 - w1


FINISH_SPLIT = 2


def _finish_rows(rs, x, y, gate, shift, scale, lg_ref, lb_ref, xo_ref, ho_ref, route_refs):
    xn = _layer_norm(ALPHA * x + (1 + gate) * y, lg_ref[...], lb_ref[...])
    xo_ref[rs, :] = xn
    if ho_ref is None:
        return
    h = xn * (1 + scale) + shift
    ho_ref[rs, :] = h.astype(BF16)
    if route_refs is not None:
        wrt_ref, hp_ref, ri_ref, rw_ref = route_refs
        hp_ref[rs, :] = _pack_bf16_pairs(h)
        _route_top2(h, rs, wrt_ref, ri_ref, rw_ref)


def _finish_dispatch(i, x_ref, y_of, pg, psh, psc, mpg_ref, mpn_ref, gs_ref, shs_ref, scs_ref,
                     lg_ref, lb_ref, xo_ref, ho_ref, route_refs):
    n = ROWS_PER_UNIT // FINISH_SPLIT
    slices = [slice(k * n, (k + 1) * n) for k in range(FINISH_SPLIT)]
    xo_prompt, xo_sample = xo_ref if isinstance(xo_ref, tuple) else (xo_ref, xo_ref)

    @pl.when(i < N_PROMPT_UNITS)
    def _():
        for rs in slices:
            _finish_rows(rs, x_ref[rs, :], y_of(True, rs), mpg_ref[pg:pg + 1, :], mpn_ref[psh:psh + 1, :],
                         mpn_ref[psc:psc + 1, :], lg_ref, lb_ref, xo_prompt, ho_ref, route_refs)

    @pl.when(i >= N_PROMPT_UNITS)
    def _():
        for rs in slices:
            _finish_rows(rs, x_ref[rs, :], y_of(False, rs), gs_ref[rs, :], shs_ref[rs, :], scs_ref[rs, :],
                         lg_ref, lb_ref, xo_sample, ho_ref, route_refs)


def _epilogue_body(*refs, pg, psh, psc, route, project):
    refs = list(refs)
    if project:
        mp_ref, ms_ref, w32_ref, b_ref = refs[:4]
        w_ref = refs.pop()
        refs = refs[4:]
        _cast_weight_once(w32_ref, w_ref, ROWS_PER_UNIT)
        y_of = lambda prompt, rs: _dot((mp_ref if prompt else ms_ref)[rs, :], w_ref[...]) + b_ref[...]
    else:
        y_ref = refs.pop(0)
        y_of = lambda prompt, rs: y_ref[rs, :]
    x_ref, mpg_ref, mpn_ref, gs_ref, shs_ref, scs_ref, lg_ref, lb_ref = refs[:8]
    if route:
        wrt_ref, xo_ref, ho_ref, hp_ref, ri_ref, rw_ref = refs[8:]
        route_refs = (wrt_ref, hp_ref, ri_ref, rw_ref)
    else:
        xo_ref, ho_ref = refs[8:]
        route_refs = None
    _finish_dispatch(pl.program_id(0), x_ref, y_of, pg, psh, psc, mpg_ref, mpn_ref,
                     gs_ref, shs_ref, scs_ref, lg_ref, lb_ref, xo_ref, ho_ref, route_refs)


def _epilogue_specs(layer, next_layer, pg, psh, psc):
    row = pl.BlockSpec((1, D_MODEL), lambda i, *_: (0, 0))
    modp = lambda l: pl.BlockSpec((None, None, 6, D_MODEL), lambda i, *_: (l, _prompt_batch_of_unit(i), 0, 0))
    return [modp(layer), modp(next_layer), _mods_spec(layer, pg), _mods_spec(next_layer, psh),
            _mods_spec(next_layer, psc), row, row]


def _epilogue(x, y, modp, mods, ln_g, ln_b, layer, next_layer, pg, psh, psc, w_router=None):
    route = w_router is not None
    project = isinstance(y, tuple)
    unit = lambda w: pl.BlockSpec((ROWS_PER_UNIT, w), lambda i: (i, 0))
    if project:
        mix_p, mix_s, w_out, b_out = y
        in_specs = [pl.BlockSpec((ROWS_PER_UNIT, D_MODEL), lambda i: (jnp.minimum(i, N_PROMPT_UNITS - 1), 0)),
                    pl.BlockSpec((M_SAMPLE, D_MODEL), lambda i: (0, 0)),
                    _resident(layer, (D_MODEL, D_MODEL)),
                    pl.BlockSpec((1, D_MODEL), lambda i: (0, 0))]
        args = [mix_p, mix_s, w_out, b_out.reshape(1, D_MODEL)]
        scratch = [pltpu.VMEM((D_MODEL, D_MODEL), BF16)]
    else:
        in_specs = [unit(D_MODEL)]
        args = [y]
        scratch = []
    in_specs += [unit(D_MODEL)] + _epilogue_specs(layer, next_layer, pg, psh, psc)
    args += [x, modp, modp, mods, mods, mods, ln_g.reshape(1, D_MODEL), ln_b.reshape(1, D_MODEL)]
    out_specs = [unit(D_MODEL), unit(D_MODEL)]
    out_shape = [jax.ShapeDtypeStruct((M_ROWS, D_MODEL), F32), jax.ShapeDtypeStruct((M_ROWS, D_MODEL), BF16)]
    if route:
        in_specs.append(pl.BlockSpec((N_EXPERTS, D_MODEL), lambda i: (0, 0)))
        args.append(w_router.T)
        out_specs += [unit(PACKED_WIDTH), unit(TOP_K), unit(TOP_K)]
        out_shape += [jax.ShapeDtypeStruct((M_ROWS, PACKED_WIDTH), jnp.uint32),
                      jax.ShapeDtypeStruct((M_ROWS, TOP_K), jnp.int32),
                      jax.ShapeDtypeStruct((M_ROWS, TOP_K), F32)]
    return pl.pallas_call(
        functools.partial(_epilogue_body, pg=pg, psh=psh, psc=psc, route=route, project=project),
        grid=(N_UNITS,),
        in_specs=in_specs,
        out_specs=out_specs,
        out_shape=out_shape,
        scratch_shapes=scratch,
        compiler_params=_params("arbitrary"),
        name=("outproj_" if project else "") + ("epilogue_route" if route else "epilogue"),
    )(*args)


FF_LANE_BLOCKS = D_FF // 128


def _ffn_prep_body(g_ref, u_ref, d_ref, go_ref, uo_ref, do_ref):
    c = pl.program_id(0)

    @pl.when(c < FF_LANE_BLOCKS)
    def _():
        go_ref[...] = g_ref[...].astype(BF16)
        uo_ref[...] = u_ref[...].astype(BF16)
        do_ref[...] = d_ref[...].astype(BF16)

    @pl.when(c >= FF_LANE_BLOCKS)
    def _():
        go_ref[...] = jnp.zeros_like(go_ref)
        uo_ref[...] = jnp.zeros_like(uo_ref)
        do_ref[...] = jnp.zeros_like(do_ref)


def _ffn_prep(w_gu, w_down, ld):
    src = lambda c: jnp.minimum(c, FF_LANE_BLOCKS - 1)
    return pl.pallas_call(
        _ffn_prep_body,
        grid=(D_FF_PAD // 128,),
        in_specs=[pl.BlockSpec((None, D_MODEL, 128), lambda c: (ld, 0, src(c))),
                  pl.BlockSpec((None, D_MODEL, 128), lambda c: (ld, 0, FF_LANE_BLOCKS + src(c))),
                  pl.BlockSpec((None, 128, D_MODEL), lambda c: (ld, src(c), 0))],
        out_specs=[pl.BlockSpec((D_MODEL, 128), lambda c: (0, c)),
                   pl.BlockSpec((D_MODEL, 128), lambda c: (0, c)),
                   pl.BlockSpec((128, D_MODEL), lambda c: (c, 0))],
        out_shape=[jax.ShapeDtypeStruct((D_MODEL, D_FF_PAD), BF16), jax.ShapeDtypeStruct((D_MODEL, D_FF_PAD), BF16),
                   jax.ShapeDtypeStruct((D_FF_PAD, D_MODEL), BF16)],
        compiler_params=_params("arbitrary"),
        name="ffn_prep",
    )(w_gu, w_gu, w_down)


def _swiglu_rows(x, wg, wu, wd):
    g = _dot(x, wg)
    u = _dot(x, wu)
    return _dot((_silu(g) * u).astype(BF16), wd)


def _ffn_body(h_ref, wg_ref, wu_ref, wd_ref, o_ref):
    i = pl.program_id(0)
    j = pl.program_id(1)
    upt = FFN_ROW_TILE // ROWS_PER_UNIT
    full = (i + 1) * upt <= N_UNITS

    def run(n_units):
        @pl.when(j == 0)
        def _():
            o_ref[:n_units * ROWS_PER_UNIT, :] = jnp.zeros((n_units * ROWS_PER_UNIT, D_MODEL), F32)

        step = min(n_units, UNITS_PER_DOT)
        for r in range(0, n_units, step):
            rows = slice(r * ROWS_PER_UNIT, (r + step) * ROWS_PER_UNIT)
            o_ref[rows, :] += _swiglu_rows(h_ref[rows, :], wg_ref[...], wu_ref[...], wd_ref[...])

    pl.when(full)(functools.partial(run, upt))
    pl.when(jnp.logical_not(full))(functools.partial(run, N_UNITS % upt))


def _ffn(h, wg, wu, wd):
    nj = D_FF_PAD // FFN_TILE
    return pl.pallas_call(
        _ffn_body,
        grid=(pl.cdiv(M_ROWS, FFN_ROW_TILE), nj),
        in_specs=[pl.BlockSpec((FFN_ROW_TILE, D_MODEL), lambda i, j: (i, 0)),
                  pl.BlockSpec((D_MODEL, FFN_TILE), lambda i, j: (0, j)),
                  pl.BlockSpec((D_MODEL, FFN_TILE), lambda i, j: (0, j)),
                  pl.BlockSpec((FFN_TILE, D_MODEL), lambda i, j: (j, 0))],
        out_specs=pl.BlockSpec((FFN_ROW_TILE, D_MODEL), lambda i, j: (i, 0)),
        out_shape=jax.ShapeDtypeStruct((M_ROWS, D_MODEL), F32),
        compiler_params=_params("arbitrary", "arbitrary"),
        name="ffn_dense",
    )(h, wg, wu, wd)


def _row_copy(src_hbm, row, dst, dst_row, sem):
    return pltpu.make_async_copy(src_hbm.at[pl.ds(row, 1), :], dst.at[pl.ds(dst_row, 1), :], sem)


ROW_DMA_UNROLL = 8


def _unit_copy(src, src_unit, dst, dst_unit, sem):
    rows = lambda u: pl.ds(pl.multiple_of(u * ROWS_PER_UNIT, ROWS_PER_UNIT), ROWS_PER_UNIT)
    return pltpu.make_async_copy(src.at[rows(src_unit), :], dst.at[rows(dst_unit), :], sem)


def _moe_dispatch_body(dest_ref, pad_lo_ref, pad_hi_ref, nu_ref, hp_ref, xs_hbm, zbuf, sem, zsem):
    i = pl.program_id(0)

    @pl.when(i == 0)
    def _():
        zbuf[...] = jnp.zeros_like(zbuf)
        zero_row = lambda r: _row_copy(zbuf, 0, xs_hbm, r, zsem)
        zero_unit = lambda u: _unit_copy(zbuf, 0, xs_hbm, u, zsem)

        def over_gaps(act):
            def row_body(r, c):
                act(zero_row(r))
                return c

            def unit_body(u, c):
                act(zero_unit(u))
                return c

            for e in range(N_EXPERTS):
                lax.fori_loop(pad_lo_ref[e], pad_hi_ref[e], row_body, 0)
            lax.fori_loop(nu_ref[0], MOE_UNITS, unit_body, 0)

        over_gaps(lambda cp: cp.start())
        over_gaps(lambda cp: cp.wait())

    base = i * (ROWS_PER_UNIT * TOP_K)

    def start(r, c):
        for k in range(TOP_K):
            _row_copy(hp_ref, r, xs_hbm, dest_ref[base + TOP_K * r + k], sem).start()
        return c

    def wait(r, c):
        for k in range(TOP_K):
            _row_copy(hp_ref, r, xs_hbm, 0, sem).wait()
        return c

    lax.fori_loop(0, ROWS_PER_UNIT, start, 0, unroll=ROW_DMA_UNROLL // TOP_K)
    lax.fori_loop(0, ROWS_PER_UNIT, wait, 0, unroll=ROW_DMA_UNROLL // TOP_K)


def _moe_dispatch(dest, pad_lo, pad_hi, total_units, h_packed):
    grid_spec = pltpu.PrefetchScalarGridSpec(
        num_scalar_prefetch=4,
        grid=(N_UNITS,),
        in_specs=[pl.BlockSpec((ROWS_PER_UNIT, PACKED_WIDTH), lambda i, *_: (i, 0))],
        out_specs=pl.BlockSpec(memory_space=pl.ANY),
        scratch_shapes=[pltpu.VMEM((ROWS_PER_UNIT, PACKED_WIDTH), jnp.uint32),
                        pltpu.SemaphoreType.DMA(()), pltpu.SemaphoreType.DMA(())],
    )
    return pl.pallas_call(
        _moe_dispatch_body,
        grid_spec=grid_spec,
        out_shape=jax.ShapeDtypeStruct((MOE_ROWS, PACKED_WIDTH), jnp.uint32),
        compiler_params=_params("arbitrary"),
        name="moe_dispatch",
    )(dest, pad_lo, pad_hi, total_units, h_packed)


def _moe_gmm_body(te_ref, tu_ref, tn_ref, nu_ref, xs_hbm, wg_ref, wu_ref, wd_ref, ys_hbm,
                  xsc, acc, wgu_b, wd_b, stage, sem_in, sem_out):
    t = pl.program_id(0)
    j = pl.program_id(1)
    nj = pl.num_programs(1)
    n = tn_ref[t]
    u0 = tu_ref[t]
    tf = MOE_FF_TILE

    @pl.when((j == 0) & (n > 0))
    def _():
        unit_in = lambda u, s: pltpu.make_async_copy(
            xs_hbm.at[pl.ds(pl.multiple_of((u0 + u) * ROWS_PER_UNIT, ROWS_PER_UNIT), ROWS_PER_UNIT), :],
            stage.at[s], sem_in.at[s])
        unit_in(0, 0).start()

        def load(u, c):
            s = u & 1

            @pl.when(u + 1 < n)
            def _():
                unit_in(u + 1, 1 - s).start()

            rows = pl.ds(pl.multiple_of(u * ROWS_PER_UNIT, ROWS_PER_UNIT), ROWS_PER_UNIT)
            unit_in(u, s).wait()
            lo, hi = _unpack_bf16_pairs(stage[s])
            xsc[rows, :PACKED_WIDTH] = lo
            xsc[rows, PACKED_WIDTH:] = hi
            return c

        lax.fori_loop(0, n, load, 0)

    t_prev = jnp.maximum(t - 1, 0)

    @pl.when((j == 0) & (t > 0) & (tn_ref[t_prev] > 0))
    def _():
        def wait(u, c):
            _unit_copy(acc, u, ys_hbm, u, sem_out).wait()
            return c

        lax.fori_loop(0, tn_ref[t_prev], wait, 0)

    @pl.when((j == 0) & (n > 0))
    def _():
        def zero(u, c):
            acc[pl.ds(pl.multiple_of(u * ROWS_PER_UNIT, ROWS_PER_UNIT), ROWS_PER_UNIT), :] = jnp.zeros(
                (ROWS_PER_UNIT, D_MODEL), F32)
            return c

        lax.fori_loop(0, n, zero, 0)

    @pl.when((j == 0) & (t == 0))
    def _():
        def start(u, c):
            _unit_copy(acc, 0, ys_hbm, u, sem_out).start()
            return c

        def wait(u, c):
            _unit_copy(acc, 0, ys_hbm, u, sem_out).wait()
            return c

        lax.fori_loop(nu_ref[0], MOE_UNITS, start, 0)
        lax.fori_loop(nu_ref[0], MOE_UNITS, wait, 0)

    @pl.when(n > 0)
    def _():
        def cast_weights():
            wgu_b[:, :tf] = wg_ref[...].astype(BF16)
            wgu_b[:, tf:] = wu_ref[...].astype(BF16)
            wd_b[...] = wd_ref[...].astype(BF16)

        def unit_rows(r0, units):
            rs = pl.ds(r0, units * ROWS_PER_UNIT)
            gu = _dot(xsc[rs, :], wgu_b[...])
            a = (_silu(gu[:, :tf]) * gu[:, tf:]).astype(BF16)
            acc[rs, :] += _dot(a, wd_b[...])

        def units_from(u_first, k, step):
            for h in range(0, k, step):
                unit_rows(pl.multiple_of((u_first + h) * ROWS_PER_UNIT, ROWS_PER_UNIT), min(step, k - h))

        @pl.when(n == MOE_TILE_UNITS)
        def _():
            cast_weights()
            units_from(0, MOE_TILE_UNITS, FULL_TILE_UNITS_PER_DOT)

        pl.when(n != MOE_TILE_UNITS)(cast_weights)
        for b in reversed(range(MOE_TILE_UNITS.bit_length())):
            size = 1 << b
            done = lax.shift_left(lax.shift_right_logical(n, b + 1), b + 1)
            pl.when((n != MOE_TILE_UNITS) & ((n & size) != 0))(
                functools.partial(units_from, done, size, UNITS_PER_DOT))

    @pl.when((j == nj - 1) & (n > 0))
    def _():
        def start(u, c):
            _unit_copy(acc, u, ys_hbm, u0 + u, sem_out).start()
            return c

        def wait(u, c):
            _unit_copy(acc, u, ys_hbm, u0 + u, sem_out).wait()
            return c

        lax.fori_loop(0, n, start, 0)

        @pl.when(t == pl.num_programs(0) - 1)
        def _():
            lax.fori_loop(0, n, wait, 0)


def _moe_gmm(tile_expert, tile_unit0, tile_nunits, total_units, xs, w_gu, w_down, lm):
    nj = D_FF_EXPERT // MOE_FF_TILE
    jj = lambda t, j, tn: jnp.where(tn[t] > 0, j, nj - 1)
    tile_rows = MOE_TILE_UNITS * ROWS_PER_UNIT
    grid_spec = pltpu.PrefetchScalarGridSpec(
        num_scalar_prefetch=4,
        grid=(MOE_TILES, nj),
        in_specs=[pl.BlockSpec(memory_space=pl.ANY),
                  pl.BlockSpec((None, None, D_MODEL, MOE_FF_TILE),
                               lambda t, j, te, tu, tn, nu: (lm, te[t], 0, jj(t, j, tn))),
                  pl.BlockSpec((None, None, D_MODEL, MOE_FF_TILE),
                               lambda t, j, te, tu, tn, nu: (lm, te[t], 0, nj + jj(t, j, tn))),
                  pl.BlockSpec((None, None, MOE_FF_TILE, D_MODEL),
                               lambda t, j, te, tu, tn, nu: (lm, te[t], jj(t, j, tn), 0))],
        out_specs=pl.BlockSpec(memory_space=pl.ANY),
        scratch_shapes=[pltpu.VMEM((tile_rows, D_MODEL), BF16), pltpu.VMEM((tile_rows, D_MODEL), F32),
                        pltpu.VMEM((D_MODEL, 2 * MOE_FF_TILE), BF16), pltpu.VMEM((MOE_FF_TILE, D_MODEL), BF16),
                        pltpu.VMEM((2, ROWS_PER_UNIT, PACKED_WIDTH), jnp.uint32),
                        pltpu.SemaphoreType.DMA((2,)), pltpu.SemaphoreType.DMA(())],
    )
    return pl.pallas_call(
        _moe_gmm_body,
        grid_spec=grid_spec,
        out_shape=jax.ShapeDtypeStruct((MOE_ROWS, D_MODEL), F32),
        compiler_params=_params("arbitrary", "arbitrary"),
        name="moe_gmm",
    )(tile_expert, tile_unit0, tile_nunits, total_units, xs, w_gu, w_gu, w_down)


def _moe_combine_body(dest_ref, ys_hbm, rw_ref, x_ref, mpg_ref, mpn_ref, gs_ref, shs_ref, scs_ref, lg_ref, lb_ref,
                      *rest, pg, psh, psc, final):
    if final:
        xo_prompt, xo_sample, buf, sem = rest
        xo_ref, ho_ref = (xo_prompt, xo_sample), None
    else:
        xo_ref, ho_ref, buf, sem = rest
    i = pl.program_id(0)
    slot = i & 1

    def issue(unit, s):
        base = unit * (ROWS_PER_UNIT * TOP_K)

        def start(r, c):
            for k in range(TOP_K):
                _row_copy(ys_hbm, dest_ref[base + TOP_K * r + k], buf.at[s, k], r, sem.at[s]).start()
            return c

        lax.fori_loop(0, ROWS_PER_UNIT, start, 0, unroll=ROW_DMA_UNROLL // TOP_K)

    @pl.when(i == 0)
    def _():
        issue(0, 0)

    @pl.when(i + 1 < pl.num_programs(0))
    def _():
        issue(i + 1, 1 - slot)

    def wait(r, c):
        for k in range(TOP_K):
            _row_copy(ys_hbm, 0, buf.at[slot, k], r, sem.at[slot]).wait()
        return c

    lax.fori_loop(0, ROWS_PER_UNIT, wait, 0, unroll=ROW_DMA_UNROLL // TOP_K)
    y_of = lambda prompt, rs: rw_ref[rs, 0:1] * buf[slot, 0, rs, :] + rw_ref[rs, 1:2] * buf[slot, 1, rs, :]
    _finish_dispatch(i, x_ref, y_of, pg, psh, psc, mpg_ref, mpn_ref, gs_ref, shs_ref, scs_ref,
                     lg_ref, lb_ref, xo_ref, ho_ref, None)


def _moe_combine(dest, ys, route_w, x, modp, mods, ln_g, ln_b, layer, next_layer, pg, psh, psc, final):
    unit = lambda w: pl.BlockSpec((ROWS_PER_UNIT, w), lambda i, d: (i, 0))
    if final:
        out_specs = [_prompt_unit_spec(D_MODEL), _sample_rows_spec(D_MODEL)]
        out_shape = [jax.ShapeDtypeStruct((M_PROMPT, D_MODEL), F32), jax.ShapeDtypeStruct((M_SAMPLE, D_MODEL), F32)]
    else:
        out_specs = [unit(D_MODEL), unit(D_MODEL)]
        out_shape = [jax.ShapeDtypeStruct((M_ROWS, D_MODEL), F32), jax.ShapeDtypeStruct((M_ROWS, D_MODEL), BF16)]
    grid_spec = pltpu.PrefetchScalarGridSpec(
        num_scalar_prefetch=1,
        grid=(N_UNITS,),
        in_specs=[pl.BlockSpec(memory_space=pl.ANY), unit(TOP_K), unit(D_MODEL)]
        + _epilogue_specs(layer, next_layer, pg, psh, psc),
        out_specs=out_specs,
        scratch_shapes=[pltpu.VMEM((2, TOP_K, ROWS_PER_UNIT, D_MODEL), F32), pltpu.SemaphoreType.DMA((2,))],
    )
    return pl.pallas_call(
        functools.partial(_moe_combine_body, pg=pg, psh=psh, psc=psc, final=final),
        grid_spec=grid_spec,
        out_shape=out_shape,
        compiler_params=_params("arbitrary"),
        name="moe_combine",
    )(dest, ys, route_w, x, modp, modp, mods, mods, mods, ln_g.reshape(1, D_MODEL), ln_b.reshape(1, D_MODEL))


def _moe_plan(route_i):
    i32 = jnp.int32
    e_flat = route_i.reshape(-1)
    onehot = (e_flat[:, None] == jnp.arange(N_EXPERTS, dtype=i32)[None, :]).astype(i32)
    csum = jnp.cumsum(onehot, axis=0)
    rank = jnp.sum(csum * onehot, axis=1) - 1
    counts = csum[-1]
    units_e = (counts + ROWS_PER_UNIT - 1) // ROWS_PER_UNIT
    unit_end = jnp.cumsum(units_e)
    unit_start = unit_end - units_e
    dest = (unit_start[e_flat] * ROWS_PER_UNIT + rank).astype(i32)
    pad_lo = (unit_start * ROWS_PER_UNIT + counts).astype(i32)
    pad_hi = (unit_end * ROWS_PER_UNIT).astype(i32)
    total_units = unit_end[-1:].astype(i32)
    tiles_e = (units_e + MOE_TILE_UNITS - 1) // MOE_TILE_UNITS
    tile_end = jnp.cumsum(tiles_e)
    tile_start = tile_end - tiles_e
    n_tiles = tile_end[-1]
    t_ids = jnp.arange(MOE_TILES, dtype=i32)
    t_eff = jnp.minimum(t_ids, n_tiles - 1)
    tile_expert = jnp.sum((t_eff[:, None] >= tile_end[None, :]).astype(i32), axis=1)
    k = t_eff - tile_start[tile_expert]
    tile_unit0 = unit_start[tile_expert] + k * MOE_TILE_UNITS
    units_left = units_e[tile_expert] - k * MOE_TILE_UNITS
    tile_nunits = jnp.where(t_ids < n_tiles, jnp.minimum(units_left, MOE_TILE_UNITS), 0)
    return (dest, pad_lo, pad_hi, total_units, tile_expert.astype(i32), tile_unit0.astype(i32),
            tile_nunits.astype(i32))


def kernel(x_prompt, x_sample, cache_win_k, cache_win_v, c_prompt, c_sample, w_ada, b_ada, w_in, b_in, v_norm_g, v_norm_b, w_spatial, b_spatial, attn_sinks, w_out, b_out, ln1_g, ln1_b, ln2_g, ln2_b, w_ffn_gu, w_ffn_down, w_router, w_exp_gu, w_exp_down):
    c_rows = jnp.concatenate([c_prompt, c_sample, jnp.zeros((ADA_ROWS - BATCH - DEC_BATCH, D_MODEL), F32)], axis=0)
    mod = _adaln(c_rows, w_ada, b_ada)
    modp = mod[:, :BATCH].reshape(DEPTH, BATCH, 6, D_MODEL)
    mods = jnp.repeat(mod[:, BATCH:BATCH + DEC_BATCH], DEC_SEQ, axis=1)

    assert DEPTH % 2 == 0, "the last layer is a mixture-of-experts layer; its combine kernel emits the outputs"
    h, x = _modulate0(x_prompt.reshape(M_PROMPT, D_MODEL), x_sample.reshape(M_SAMPLE, D_MODEL), modp, mods)

    rows = GQA * DEC_SEQ
    kp_list, vp_list, ks_list, vs_list, cv_list = [], [], [], [], []
    for l in range(DEPTH):
        nl = min(l + 1, DEPTH - 1)
        u, v, q, kv = _proj(h, w_in, l, b_in[l], v_norm_g[l], v_norm_b[l])
        b_s_t = b_spatial[l].T
        mix_p = _mix_prompt(attn_sinks[l].reshape(-1), u, v, q, kv, w_spatial[l], b_s_t)

        u_s = u[M_PROMPT:].reshape(DEC_BATCH, DEC_SEQ, A_WIDTH)
        v_s = v[M_PROMPT:].reshape(DEC_BATCH, DEC_SEQ, A_WIDTH)
        kv_s = kv[M_PROMPT:].reshape(DEC_BATCH, DEC_SEQ, 2 * KV_WIDTH)
        q_s = q[M_PROMPT:].reshape(DEC_BATCH, DEC_SEQ, N_KV_HEADS, GQA, HEAD_DIM)
        q_s = q_s.transpose(0, 2, 3, 1, 4).reshape(DEC_BATCH, N_KV_HEADS, rows, HEAD_DIM)
        ck = cache_win_k[l].reshape(DEC_BATCH, WINDOW, KV_WIDTH)
        cv = cache_win_v[l].reshape(DEC_BATCH, WINDOW, KV_WIDTH)
        sink_rows = jnp.repeat(attn_sinks[l], DEC_SEQ, axis=1).reshape(N_KV_HEADS, rows, 1)
        a_s, o_s = _mix_sample(u_s, v_s, q_s, kv_s, ck, cv, w_spatial[l], b_s_t, sink_rows)
        o_s = o_s.reshape(DEC_BATCH, N_KV_HEADS, GQA, DEC_SEQ, HEAD_DIM).transpose(0, 3, 1, 2, 4)
        mix_s = jnp.concatenate([a_s.reshape(M_SAMPLE, A_WIDTH), o_s.reshape(M_SAMPLE, B_WIDTH)], axis=1)

        y = (mix_p, mix_s, w_out, b_out[l])
        if l % 2 == 0:
            x, h = _epilogue(x, y, modp, mods, ln1_g[l], ln1_b[l], l, l, P_GATE1, P_SHIFT2, P_SCALE2)
            ld = l // 2
            wg, wu, wd = _ffn_prep(w_ffn_gu, w_ffn_down, ld)
            f = _ffn(h, wg, wu, wd)
            x, h = _epilogue(x, f, modp, mods, ln2_g[l], ln2_b[l], l, nl, P_GATE2, P_SHIFT1, P_SCALE1)
        else:
            lm = l // 2
            x, h, h_packed, route_i, route_w = _epilogue(x, y, modp, mods, ln1_g[l], ln1_b[l], l, l,
                                                      P_GATE1, P_SHIFT2, P_SCALE2, w_router=w_router[lm])
            dest, pad_lo, pad_hi, total_units, tile_expert, tile_unit0, tile_nunits = _moe_plan(route_i)
            xs = _moe_dispatch(dest, pad_lo, pad_hi, total_units, h_packed)
            ys = _moe_gmm(tile_expert, tile_unit0, tile_nunits, total_units, xs, w_exp_gu, w_exp_down, lm)
            x, h = _moe_combine(dest, ys, route_w, x, modp, mods, ln2_g[l], ln2_b[l], l, nl,
                                P_GATE2, P_SHIFT1, P_SCALE1, final=(l == DEPTH - 1))

        kv_p = kv[:M_PROMPT].reshape(BATCH, SEQ, 2 * KV_WIDTH)[:, SEQ - WINDOW:]
        kp_list.append(kv_p[..., :KV_WIDTH].reshape(BATCH, WINDOW, N_KV_HEADS, HEAD_DIM))
        vp_list.append(kv_p[..., KV_WIDTH:].reshape(BATCH, WINDOW, N_KV_HEADS, HEAD_DIM))
        k_new = kv_s[..., :KV_WIDTH].reshape(DEC_BATCH, DEC_SEQ, N_KV_HEADS, HEAD_DIM)
        v_new = kv_s[..., KV_WIDTH:].reshape(DEC_BATCH, DEC_SEQ, N_KV_HEADS, HEAD_DIM)
        ks_list.append(jnp.concatenate([cache_win_k[l][:, DEC_SEQ:], k_new], axis=1))
        vs_list.append(jnp.concatenate([cache_win_v[l][:, DEC_SEQ:], v_new], axis=1))
        cv_list.append(v_s.astype(F32))

    y_prompt = x.reshape(BATCH, SEQ, D_MODEL)
    y_sample = h.reshape(DEC_BATCH, DEC_SEQ, D_MODEL)
    return (y_prompt, y_sample, jnp.stack(kp_list), jnp.stack(vp_list), jnp.stack(ks_list), jnp.stack(vs_list),
            jnp.stack(cv_list))
```

```python
import functools

import jax
import jax.numpy as jnp
from jax import lax
from jax.experimental import pallas as pl
from jax.experimental.pallas import tpu as pltpu

F32 = jnp.float32
BF16 = jnp.bfloat16

D_MODEL = 2048
BATCH = 4
SEQ = 2048
DEPTH = 4
DEC_BATCH = 32
DEC_SEQ = 8
CHUNK = 128
A_WIDTH = D_MODEL // 2
A_GROUPS = 4
A_GROUP_DIM = A_WIDTH // A_GROUPS
HEAD_DIM = 64
B_WIDTH = D_MODEL - A_WIDTH
N_HEADS = B_WIDTH // HEAD_DIM
N_KV_HEADS = 2
GQA = N_HEADS // N_KV_HEADS
KV_WIDTH = N_KV_HEADS * HEAD_DIM
WINDOW = 128
IN_COLS = 2 * A_WIDTH + B_WIDTH + 2 * KV_WIDTH
D_FF = 5504
N_EXPERTS = 8
TOP_K = 2
D_FF_EXPERT = 7168
ALPHA = (2.0 * DEPTH) ** 0.25
LN_EPS = 1e-5
NEG_INF = -1e30

M_PROMPT = BATCH * SEQ
M_SAMPLE = DEC_BATCH * DEC_SEQ
M_ROWS = M_PROMPT + M_SAMPLE
ROWS_PER_UNIT = 256
N_PROMPT_UNITS = M_PROMPT // ROWS_PER_UNIT
N_UNITS = M_ROWS // ROWS_PER_UNIT
UNITS_PER_SEQ = SEQ // ROWS_PER_UNIT
ADA_ROWS = 40
P_SHIFT1, P_SCALE1, P_GATE1, P_SHIFT2, P_SCALE2, P_GATE2 = range(6)

D_FF_PAD = 5632
FFN_TILE = 512
FFN_ROW_TILE = 4 * ROWS_PER_UNIT
MOE_FF_TILE = 256
N_PAIRS = M_ROWS * TOP_K
MOE_UNITS = N_PAIRS // ROWS_PER_UNIT + N_EXPERTS
MOE_ROWS = MOE_UNITS * ROWS_PER_UNIT
MOE_TILE_UNITS = 9
UNITS_PER_DOT = 2
FULL_TILE_UNITS_PER_DOT = 3
MOE_TILES =MOE_UNITS // MOE_TILE_UNITS + N_EXPERTS

VMEM_LIMIT = 56 * 1024 * 1024


def _params(*sem):
    return pltpu.CompilerParams(dimension_semantics=sem, vmem_limit_bytes=VMEM_LIMIT)


def _dot(a, b):
    return jnp.dot(a, b, preferred_element_type=F32)


def _layer_norm(z, g, b):
    mu = jnp.mean(z, axis=-1, keepdims=True)
    zc = z - mu
    var = jnp.mean(zc * zc, axis=-1, keepdims=True)
    return zc * lax.rsqrt(var + LN_EPS) * g + b


def _gelu_tanh(x):
    return x * (0.5 * (1.0 + jnp.tanh(0.7978845608028654 * (x + 0.044715 * (x * x * x)))))


def _silu(x):
    return x * jax.nn.sigmoid(x)


def _prompt_batch_of_unit(i):
    return jnp.minimum(i // UNITS_PER_SEQ, BATCH - 1)


def _adaln_body(c_ref, w_ref, b_ref, o_ref):
    s = _silu(c_ref[...]).astype(BF16)
    o_ref[...] = _dot(s, w_ref[...].astype(BF16)) + b_ref[...]


def _adaln(c_rows, w_ada, b_ada):
    tn = 1024
    return pl.pallas_call(
        _adaln_body,
        grid=(DEPTH, 6 * D_MODEL // tn),
        in_specs=[pl.BlockSpec((ADA_ROWS, D_MODEL), lambda l, n: (0, 0)),
                  pl.BlockSpec((None, D_MODEL, tn), lambda l, n: (l, 0, n)),
                  pl.BlockSpec((None, 1, tn), lambda l, n: (l, 0, n))],
        out_specs=pl.BlockSpec((None, ADA_ROWS, tn), lambda l, n: (l, 0, n)),
        out_shape=jax.ShapeDtypeStruct((DEPTH, ADA_ROWS, 6 * D_MODEL), F32),
        compiler_params=_params("arbitrary", "arbitrary"),
        name="adaln",
    )(c_rows, w_ada, b_ada.reshape(DEPTH, 1, 6 * D_MODEL))


def _mod0_body(xp_ref, xs_ref, mp_ref, ss_ref, cs_ref, h_ref, x_ref):
    i = pl.program_id(0)

    @pl.when(i < N_PROMPT_UNITS)
    def _():
        x = xp_ref[...]
        x_ref[...] = x
        h_ref[...] = (x * (1 + mp_ref[P_SCALE1:P_SCALE1 + 1, :]) + mp_ref[P_SHIFT1:P_SHIFT1 + 1, :]).astype(BF16)

    @pl.when(i >= N_PROMPT_UNITS)
    def _():
        x = xs_ref[...]
        x_ref[...] = x
        h_ref[...] = (x * (1 + cs_ref[...]) + ss_ref[...]).astype(BF16)


def _unit_spec(width):
    return pl.BlockSpec((ROWS_PER_UNIT, width), lambda i: (i, 0))


def _modp_spec(layer):
    return pl.BlockSpec((None, None, 6, D_MODEL), lambda i: (layer, _prompt_batch_of_unit(i), 0, 0))


def _mods_spec(layer, p):
    return pl.BlockSpec((None, ROWS_PER_UNIT, D_MODEL), lambda i, *_: (layer, 0, p))


def _prompt_unit_spec(width):
    return pl.BlockSpec((ROWS_PER_UNIT, width), lambda i, *_: (jnp.minimum(i, N_PROMPT_UNITS - 1), 0))


def _sample_rows_spec(width):
    return pl.BlockSpec((M_SAMPLE, width), lambda i, *_: (0, 0))


def _modulate0(x_prompt, x_sample, modp, mods):
    return pl.pallas_call(
        _mod0_body,
        grid=(N_UNITS,),
        in_specs=[_prompt_unit_spec(D_MODEL), _sample_rows_spec(D_MODEL), _modp_spec(0),
                  _mods_spec(0, P_SHIFT1), _mods_spec(0, P_SCALE1)],
        out_specs=[_unit_spec(D_MODEL), _unit_spec(D_MODEL)],
        out_shape=[jax.ShapeDtypeStruct((M_ROWS, D_MODEL), BF16), jax.ShapeDtypeStruct((M_ROWS, D_MODEL), F32)],
        compiler_params=_params("arbitrary"),
        name="modulate0",
    )(x_prompt, x_sample, modp, mods, mods)


def _cast_weight_once(w32_ref, w_ref, chunk):
    @pl.when(pl.program_id(0) == 0)
    def _():
        for c in range(0, w32_ref.shape[1], chunk):
            w_ref[:, c:c + chunk] = w32_ref[:, c:c + chunk].astype(BF16)


def _resident(layer, shape):
    return pl.BlockSpec((None,) + shape, lambda i: (layer,) + (0,) * len(shape), pipeline_mode=pl.Buffered(1))


def _proj_body(h_ref, w32_ref, b_ref, gv_ref, bv_ref, u_ref, v_ref, q_ref, kv_ref, w_ref):
    _cast_weight_once(w32_ref, w_ref, A_GROUP_DIM)
    h = h_ref[...]
    gd = A_GROUP_DIM
    for c in range(A_GROUPS):
        sl = slice(c * gd, (c + 1) * gd)
        z = _dot(h, w_ref[:, sl]) + b_ref[:, sl]
        u_ref[:, sl] = _gelu_tanh(z).astype(BF16)
    for g in range(A_GROUPS):
        sl = slice(A_WIDTH + g * gd, A_WIDTH + (g + 1) * gd)
        z = _gelu_tanh(_dot(h, w_ref[:, sl]) + b_ref[:, sl])
        vn = _layer_norm(z, gv_ref[g:g + 1, :], bv_ref[g:g + 1, :])
        v_ref[:, g * gd:(g + 1) * gd] = vn.astype(BF16)
    o1 = 2 * A_WIDTH
    o2 = o1 + B_WIDTH
    z = _dot(h, w_ref[:, o1:o2]) + b_ref[:, o1:o2]
    q_ref[...] = (z * (HEAD_DIM ** -0.5)).astype(BF16)
    kv_ref[...] = _dot(h, w_ref[:, o2:]) + b_ref[:, o2:]


def _proj(h, w_all, layer, b, gv, bv):
    const = lambda shape: pl.BlockSpec(shape, lambda i: (0,) * len(shape))
    return pl.pallas_call(
        _proj_body,
        grid=(N_UNITS,),
        in_specs=[_unit_spec(D_MODEL), _resident(layer, (D_MODEL, IN_COLS)), const((1, IN_COLS)),
                  const((A_GROUPS, A_GROUP_DIM)), const((A_GROUPS, A_GROUP_DIM))],
        out_specs=[_unit_spec(A_WIDTH), _unit_spec(A_WIDTH), _unit_spec(B_WIDTH), _unit_spec(2 * KV_WIDTH)],
        out_shape=[jax.ShapeDtypeStruct((M_ROWS, A_WIDTH), BF16), jax.ShapeDtypeStruct((M_ROWS, A_WIDTH), BF16),
                   jax.ShapeDtypeStruct((M_ROWS, B_WIDTH), BF16), jax.ShapeDtypeStruct((M_ROWS, 2 * KV_WIDTH), F32)],
        scratch_shapes=[pltpu.VMEM((D_MODEL, IN_COLS), BF16)],
        compiler_params=_params("arbitrary"),
        name="proj",
    )(h, w_all, b.reshape(1, IN_COLS), gv, bv)


def _mix_prompt_body(sink_ref, u_ref, v_ref, q_ref, kvc_ref, kvp_ref, ws_ref, bs_ref, o_ref):
    blk = pl.program_id(1)
    gd = A_GROUP_DIM
    tr = lax.broadcasted_iota(jnp.int32, (CHUNK, CHUNK), 0)
    tc = lax.broadcasted_iota(jnp.int32, (CHUNK, CHUNK), 1)
    for g in range(A_GROUPS):
        sl = slice(g * gd, (g + 1) * gd)
        w = jnp.where(tr >= tc, ws_ref[g], 0.0).astype(BF16)
        s = _dot(w, v_ref[:, sl]) + bs_ref[:, g:g + 1]
        o_ref[:, sl] = (u_ref[:, sl].astype(F32) * s).astype(BF16)

    kvc = kvc_ref[...]
    kvp = kvp_ref[...]
    row = lax.broadcasted_iota(jnp.int32, (WINDOW, 2 * WINDOW), 0)
    col = lax.broadcasted_iota(jnp.int32, (WINDOW, 2 * WINDOW), 1)
    dlt = row + WINDOW - col
    mask = (dlt >= 0) & (dlt <= WINDOW) & ((col >= WINDOW) | (blk > 0))
    hd = HEAD_DIM
    for j in range(N_KV_HEADS):
        ks = slice(j * hd, (j + 1) * hd)
        vs = slice(KV_WIDTH + j * hd, KV_WIDTH + (j + 1) * hd)
        kk = jnp.concatenate([kvp[:, ks], kvc[:, ks]], axis=0).astype(BF16)
        vv = jnp.concatenate([kvp[:, vs], kvc[:, vs]], axis=0).astype(BF16)
        for hh in range(GQA):
            h = j * GQA + hh
            qh = q_ref[:, h * hd:(h + 1) * hd]
            s = lax.dot_general(qh, kk, (((1,), (1,)), ((), ())), preferred_element_type=F32)
            s = jnp.where(mask, s, NEG_INF)
            sk = sink_ref[h]
            m = jnp.maximum(jnp.max(s, axis=-1, keepdims=True), sk)
            p = jnp.exp(s - m)
            den = jnp.sum(p, axis=-1, keepdims=True) + jnp.exp(sk - m)
            o = _dot(p.astype(BF16), vv) / den
            o_ref[:, A_WIDTH + h * hd:A_WIDTH + (h + 1) * hd] = o.astype(BF16)


def _mix_prompt(sinks, u, v, q, kv, w_s, b_s_t):
    nb = SEQ // WINDOW
    row_blk = lambda w: pl.BlockSpec((WINDOW, w), lambda b, i, s: (b * nb + i, 0))
    grid_spec = pltpu.PrefetchScalarGridSpec(
        num_scalar_prefetch=1,
        grid=(BATCH, nb),
        in_specs=[row_blk(A_WIDTH), row_blk(A_WIDTH), row_blk(B_WIDTH), row_blk(2 * KV_WIDTH),
                  pl.BlockSpec((WINDOW, 2 * KV_WIDTH), lambda b, i, s: (b * nb + jnp.maximum(i - 1, 0), 0)),
                  pl.BlockSpec((A_GROUPS, CHUNK, CHUNK), lambda b, i, s: (0, 0, 0)),
                  pl.BlockSpec((CHUNK, A_GROUPS), lambda b, i, s: (0, 0))],
        out_specs=pl.BlockSpec((WINDOW, D_MODEL), lambda b, i, s: (b * nb + i, 0)),
    )
    return pl.pallas_call(
        _mix_prompt_body,
        grid_spec=grid_spec,
        out_shape=jax.ShapeDtypeStruct((M_PROMPT, D_MODEL), BF16),
        compiler_params=_params("arbitrary", "arbitrary"),
        name="mix_prompt",
    )(sinks, u, v, q, kv, kv, w_s, b_s_t)


SAMPLE_KEYS = 2 * WINDOW


SAMPLE_BATCH_BLOCK = 8


def _mix_sample_body(u_ref, v_ref, q_ref, kn_ref, ck_ref, cv_ref, ws_ref, bs_ref, sk_ref, a_ref, o_ref):
    for b in range(SAMPLE_BATCH_BLOCK):
        _mix_sample_one(u_ref.at[b], v_ref.at[b], q_ref.at[b], kn_ref.at[b], ck_ref.at[b], cv_ref.at[b],
                        ws_ref, bs_ref, sk_ref, a_ref.at[b], o_ref.at[b])


def _mix_sample_one(u_ref, v_ref, q_ref, kn_ref, ck_ref, cv_ref, ws_ref, bs_ref, sk_ref, a_ref, o_ref):
    gd = A_GROUP_DIM
    ds = DEC_SEQ
    v = v_ref[...].astype(F32)
    u = u_ref[...].astype(F32)
    tr = lax.broadcasted_iota(jnp.int32, (ds, ds), 0)
    tc = lax.broadcasted_iota(jnp.int32, (ds, ds), 1)
    for g in range(A_GROUPS):
        sl = slice(g * gd, (g + 1) * gd)
        w = jnp.where(tr >= tc, ws_ref[g][:ds, :ds], 0.0)
        s = jnp.zeros((ds, gd), F32) + bs_ref[:ds, g:g + 1]
        for t in range(ds):
            s = s + w[:, t:t + 1] * v[t:t + 1, sl]
        a_ref[:, sl] = (u[:, sl] * s).astype(BF16)

    kn = kn_ref[...]
    ck = ck_ref[...]
    cv = cv_ref[...]
    hd = HEAD_DIM
    rows = GQA * ds
    qt = lax.broadcasted_iota(jnp.int32, (rows, SAMPLE_KEYS), 0) & (ds - 1)
    col = lax.broadcasted_iota(jnp.int32, (rows, SAMPLE_KEYS), 1)
    mask = ((col < WINDOW) & (col >= qt)) | ((col >= WINDOW) & (col - WINDOW <= qt))
    pad = jnp.zeros((SAMPLE_KEYS - WINDOW - ds, hd), F32)
    for j in range(N_KV_HEADS):
        ks = slice(j * hd, (j + 1) * hd)
        vs = slice(KV_WIDTH + j * hd, KV_WIDTH + (j + 1) * hd)
        kk = jnp.concatenate([ck[:, ks], kn[:, ks], pad], axis=0).astype(BF16)
        vv = jnp.concatenate([cv[:, ks], kn[:, vs], pad], axis=0).astype(BF16)
        s = lax.dot_general(q_ref[j], kk, (((1,), (1,)), ((), ())), preferred_element_type=F32)
        s = jnp.where(mask, s, NEG_INF)
        sk = sk_ref[j]
        m = jnp.maximum(jnp.max(s, axis=-1, keepdims=True), sk)
        p = jnp.exp(s - m)
        den = jnp.sum(p, axis=-1, keepdims=True) + jnp.exp(sk - m)
        o_ref[j] = (_dot(p.astype(BF16), vv) / den).astype(BF16)


def _mix_sample(u_s, v_s, q_s, kv_s, cache_k, cache_v, w_s, b_s_t, sink_rows):
    rows = GQA * DEC_SEQ
    per_b = lambda *shape: pl.BlockSpec((SAMPLE_BATCH_BLOCK,) + shape, lambda b: (b,) + (0,) * len(shape))
    const = lambda *shape: pl.BlockSpec(shape, lambda b: (0,) * len(shape))
    return pl.pallas_call(
        _mix_sample_body,
        grid=(DEC_BATCH // SAMPLE_BATCH_BLOCK,),
        in_specs=[per_b(DEC_SEQ, A_WIDTH), per_b(DEC_SEQ, A_WIDTH), per_b(N_KV_HEADS, rows, HEAD_DIM),
                  per_b(DEC_SEQ, 2 * KV_WIDTH), per_b(WINDOW, KV_WIDTH), per_b(WINDOW, KV_WIDTH),
                  const(A_GROUPS, CHUNK, CHUNK), const(CHUNK, A_GROUPS), const(N_KV_HEADS, rows, 1)],
        out_specs=[per_b(DEC_SEQ, A_WIDTH), per_b(N_KV_HEADS, rows, HEAD_DIM)],
        out_shape=[jax.ShapeDtypeStruct((DEC_BATCH, DEC_SEQ, A_WIDTH), BF16),
                   jax.ShapeDtypeStruct((DEC_BATCH, N_KV_HEADS, rows, HEAD_DIM), BF16)],
        compiler_params=_params("arbitrary"),
        name="mix_sample",
    )(u_s, v_s, q_s, kv_s, cache_k, cache_v, w_s, b_s_t, sink_rows)


PACKED_WIDTH = D_MODEL // 2


def _pack_bf16_pairs(h):
    bits = lambda v: lax.bitcast_convert_type(v.astype(BF16).astype(F32), jnp.uint32)
    lo = lax.shift_right_logical(bits(h[:, :PACKED_WIDTH]), jnp.uint32(16))
    hi = bits(h[:, PACKED_WIDTH:]) & jnp.uint32(0xFFFF0000)
    return lo | hi


def _unpack_bf16_pairs(w):
    lo = lax.bitcast_convert_type(lax.shift_left(w, jnp.uint32(16)), F32).astype(BF16)
    hi = lax.bitcast_convert_type(w & jnp.uint32(0xFFFF0000), F32).astype(BF16)
    return lo, hi


def _route_top2(h, rs, wrt_ref, ri_ref, rw_ref):
    logits = [jnp.sum(h * wrt_ref[e:e + 1, :], axis=-1, keepdims=True) for e in range(N_EXPERTS)]
    m1 = logits[0]
    i1 = jnp.zeros_like(m1, dtype=jnp.int32)
    for e in range(1, N_EXPERTS):
        gt = logits[e] > m1
        m1 = jnp.where(gt, logits[e], m1)
        i1 = jnp.where(gt, e, i1)
    m2 = jnp.full_like(m1, -jnp.inf)
    i2 = jnp.zeros_like(i1)
    for e in range(N_EXPERTS):
        ok = (i1 != e) & (logits[e] > m2)
        m2 = jnp.where(ok, logits[e], m2)
        i2 = jnp.where(ok, e, i2)
    w1 = 1.0 / (1.0 + jnp.exp(m2 - m1))
    ri_ref[rs, 0:1] = i1
    ri_ref[rs, 1:2] = i2
    rw_ref[rs, 0:1] = w1
    rw_ref[rs, 1:2] = 1.0 - w1


FINISH_SPLIT = 2


def _finish_rows(rs, x, y, gate, shift, scale, lg_ref, lb_ref, xo_ref, ho_ref, route_refs):
    xn = _layer_norm(ALPHA * x + (1 + gate) * y, lg_ref[...], lb_ref[...])
    xo_ref[rs, :] = xn
    if ho_ref is None:
        return
    h = xn * (1 + scale) + shift
    ho_ref[rs, :] = h.astype(BF16)
    if route_refs is not None:
        wrt_ref, hp_ref, ri_ref, rw_ref = route_refs
        hp_ref[rs, :] = _pack_bf16_pairs(h)
        _route_top2(h, rs, wrt_ref, ri_ref, rw_ref)


def _finish_dispatch(i, x_ref, y_of, pg, psh, psc, mpg_ref, mpn_ref, gs_ref, shs_ref, scs_ref,
                     lg_ref, lb_ref, xo_ref, ho_ref, route_refs):
    n = ROWS_PER_UNIT // FINISH_SPLIT
    slices = [slice(k * n, (k + 1) * n) for k in range(FINISH_SPLIT)]
    xo_prompt, xo_sample = xo_ref if isinstance(xo_ref, tuple) else (xo_ref, xo_ref)

    @pl.when(i < N_PROMPT_UNITS)
    def _():
        for rs in slices:
            _finish_rows(rs, x_ref[rs, :], y_of(True, rs), mpg_ref[pg:pg + 1, :], mpn_ref[psh:psh + 1, :],
                         mpn_ref[psc:psc + 1, :], lg_ref, lb_ref, xo_prompt, ho_ref, route_refs)

    @pl.when(i >= N_PROMPT_UNITS)
    def _():
        for rs in slices:
            _finish_rows(rs, x_ref[rs, :], y_of(False, rs), gs_ref[rs, :], shs_ref[rs, :], scs_ref[rs, :],
                         lg_ref, lb_ref, xo_sample, ho_ref, route_refs)


def _epilogue_body(*refs, pg, psh, psc, route, project):
    refs = list(refs)
    if project:
        mp_ref, ms_ref, w32_ref, b_ref = refs[:4]
        w_ref = refs.pop()
        refs = refs[4:]
        _cast_weight_once(w32_ref, w_ref, ROWS_PER_UNIT)
        y_of = lambda prompt, rs: _dot((mp_ref if prompt else ms_ref)[rs, :], w_ref[...]) + b_ref[...]
    else:
        y_ref = refs.pop(0)
        y_of = lambda prompt, rs: y_ref[rs, :]
    x_ref, mpg_ref, mpn_ref, gs_ref, shs_ref, scs_ref, lg_ref, lb_ref = refs[:8]
    if route:
        wrt_ref, xo_ref, ho_ref, hp_ref, ri_ref, rw_ref = refs[8:]
        route_refs = (wrt_ref, hp_ref, ri_ref, rw_ref)
    else:
        xo_ref, ho_ref = refs[8:]
        route_refs = None
    _finish_dispatch(pl.program_id(0), x_ref, y_of, pg, psh, psc, mpg_ref, mpn_ref,
                     gs_ref, shs_ref, scs_ref, lg_ref, lb_ref, xo_ref, ho_ref, route_refs)


def _epilogue_specs(layer, next_layer, pg, psh, psc):
    row = pl.BlockSpec((1, D_MODEL), lambda i, *_: (0, 0))
    modp = lambda l: pl.BlockSpec((None, None, 6, D_MODEL), lambda i, *_: (l, _prompt_batch_of_unit(i), 0, 0))
    return [modp(layer), modp(next_layer), _mods_spec(layer, pg), _mods_spec(next_layer, psh),
            _mods_spec(next_layer, psc), row, row]


def _epilogue(x, y, modp, mods, ln_g, ln_b, layer, next_layer, pg, psh, psc, w_router=None):
    route = w_router is not None
    project = isinstance(y, tuple)
    unit = lambda w: pl.BlockSpec((ROWS_PER_UNIT, w), lambda i: (i, 0))
    if project:
        mix_p, mix_s, w_out, b_out = y
        in_specs = [pl.BlockSpec((ROWS_PER_UNIT, D_MODEL), lambda i: (jnp.minimum(i, N_PROMPT_UNITS - 1), 0)),
                    pl.BlockSpec((M_SAMPLE, D_MODEL), lambda i: (0, 0)),
                    _resident(layer, (D_MODEL, D_MODEL)),
                    pl.BlockSpec((1, D_MODEL), lambda i: (0, 0))]
        args = [mix_p, mix_s, w_out, b_out.reshape(1, D_MODEL)]
        scratch = [pltpu.VMEM((D_MODEL, D_MODEL), BF16)]
    else:
        in_specs = [unit(D_MODEL)]
        args = [y]
        scratch = []
    in_specs += [unit(D_MODEL)] + _epilogue_specs(layer, next_layer, pg, psh, psc)
    args += [x, modp, modp, mods, mods, mods, ln_g.reshape(1, D_MODEL), ln_b.reshape(1, D_MODEL)]
    out_specs = [unit(D_MODEL), unit(D_MODEL)]
    out_shape = [jax.ShapeDtypeStruct((M_ROWS, D_MODEL), F32), jax.ShapeDtypeStruct((M_ROWS, D_MODEL), BF16)]
    if route:
        in_specs.append(pl.BlockSpec((N_EXPERTS, D_MODEL), lambda i: (0, 0)))
        args.append(w_router.T)
        out_specs += [unit(PACKED_WIDTH), unit(TOP_K), unit(TOP_K)]
        out_shape += [jax.ShapeDtypeStruct((M_ROWS, PACKED_WIDTH), jnp.uint32),
                      jax.ShapeDtypeStruct((M_ROWS, TOP_K), jnp.int32),
                      jax.ShapeDtypeStruct((M_ROWS, TOP_K), F32)]
    return pl.pallas_call(
        functools.partial(_epilogue_body, pg=pg, psh=psh, psc=psc, route=route, project=project),
        grid=(N_UNITS,),
        in_specs=in_specs,
        out_specs=out_specs,
        out_shape=out_shape,
        scratch_shapes=scratch,
        compiler_params=_params("arbitrary"),
        name=("outproj_" if project else "") + ("epilogue_route" if route else "epilogue"),
    )(*args)


FF_LANE_BLOCKS = D_FF // 128


def _ffn_prep_body(g_ref, u_ref, d_ref, go_ref, uo_ref, do_ref):
    c = pl.program_id(0)

    @pl.when(c < FF_LANE_BLOCKS)
    def _():
        go_ref[...] = g_ref[...].astype(BF16)
        uo_ref[...] = u_ref[...].astype(BF16)
        do_ref[...] = d_ref[...].astype(BF16)

    @pl.when(c >= FF_LANE_BLOCKS)
    def _():
        go_ref[...] = jnp.zeros_like(go_ref)
        uo_ref[...] = jnp.zeros_like(uo_ref)
        do_ref[...] = jnp.zeros_like(do_ref)


def _ffn_prep(w_gu, w_down, ld):
    src = lambda c: jnp.minimum(c, FF_LANE_BLOCKS - 1)
    return pl.pallas_call(
        _ffn_prep_body,
        grid=(D_FF_PAD // 128,),
        in_specs=[pl.BlockSpec((None, D_MODEL, 128), lambda c: (ld, 0, src(c))),
                  pl.BlockSpec((None, D_MODEL, 128), lambda c: (ld, 0, FF_LANE_BLOCKS + src(c))),
                  pl.BlockSpec((None, 128, D_MODEL), lambda c: (ld, src(c), 0))],
        out_specs=[pl.BlockSpec((D_MODEL, 128), lambda c: (0, c)),
                   pl.BlockSpec((D_MODEL, 128), lambda c: (0, c)),
                   pl.BlockSpec((128, D_MODEL), lambda c: (c, 0))],
        out_shape=[jax.ShapeDtypeStruct((D_MODEL, D_FF_PAD), BF16), jax.ShapeDtypeStruct((D_MODEL, D_FF_PAD), BF16),
                   jax.ShapeDtypeStruct((D_FF_PAD, D_MODEL), BF16)],
        compiler_params=_params("arbitrary"),
        name="ffn_prep",
    )(w_gu, w_gu, w_down)


def _swiglu_rows(x, wg, wu, wd):
    g = _dot(x, wg)
    u = _dot(x, wu)
    return _dot((_silu(g) * u).astype(BF16), wd)


def _ffn_body(h_ref, wg_ref, wu_ref, wd_ref, o_ref):
    i = pl.program_id(0)
    j = pl.program_id(1)
    upt = FFN_ROW_TILE // ROWS_PER_UNIT
    full = (i + 1) * upt <= N_UNITS

    def run(n_units):
        @pl.when(j == 0)
        def _():
            o_ref[:n_units * ROWS_PER_UNIT, :] = jnp.zeros((n_units * ROWS_PER_UNIT, D_MODEL), F32)

        step = min(n_units, UNITS_PER_DOT)
        for r in range(0, n_units, step):
            rows = slice(r * ROWS_PER_UNIT, (r + step) * ROWS_PER_UNIT)
            o_ref[rows, :] += _swiglu_rows(h_ref[rows, :], wg_ref[...], wu_ref[...], wd_ref[...])

    pl.when(full)(functools.partial(run, upt))
    pl.when(jnp.logical_not(full))(functools.partial(run, N_UNITS % upt))


def _ffn(h, wg, wu, wd):
    nj = D_FF_PAD // FFN_TILE
    return pl.pallas_call(
        _ffn_body,
        grid=(pl.cdiv(M_ROWS, FFN_ROW_TILE), nj),
        in_specs=[pl.BlockSpec((FFN_ROW_TILE, D_MODEL), lambda i, j: (i, 0)),
                  pl.BlockSpec((D_MODEL, FFN_TILE), lambda i, j: (0, j)),
                  pl.BlockSpec((D_MODEL, FFN_TILE), lambda i, j: (0, j)),
                  pl.BlockSpec((FFN_TILE, D_MODEL), lambda i, j: (j, 0))],
        out_specs=pl.BlockSpec((FFN_ROW_TILE, D_MODEL), lambda i, j: (i, 0)),
        out_shape=jax.ShapeDtypeStruct((M_ROWS, D_MODEL), F32),
        compiler_params=_params("arbitrary", "arbitrary"),
        name="ffn_dense",
    )(h, wg, wu, wd)


def _row_copy(src_hbm, row, dst, dst_row, sem):
    return pltpu.make_async_copy(src_hbm.at[pl.ds(row, 1), :], dst.at[pl.ds(dst_row, 1), :], sem)


ROW_DMA_UNROLL = 8


def _unit_copy(src, src_unit, dst, dst_unit, sem):
    rows = lambda u: pl.ds(pl.multiple_of(u * ROWS_PER_UNIT, ROWS_PER_UNIT), ROWS_PER_UNIT)
    return pltpu.make_async_copy(src.at[rows(src_unit), :], dst.at[rows(dst_unit), :], sem)


def _moe_dispatch_body(dest_ref, pad_lo_ref, pad_hi_ref, nu_ref, hp_ref, xs_hbm, zbuf, sem, zsem):
    i = pl.program_id(0)

    @pl.when(i == 0)
    def _():
        zbuf[...] = jnp.zeros_like(zbuf)
        zero_row = lambda r: _row_copy(zbuf, 0, xs_hbm, r, zsem)
        zero_unit = lambda u: _unit_copy(zbuf, 0, xs_hbm, u, zsem)

        def over_gaps(act):
            def row_body(r, c):
                act(zero_row(r))
                return c

            def unit_body(u, c):
                act(zero_unit(u))
                return c

            for e in range(N_EXPERTS):
                lax.fori_loop(pad_lo_ref[e], pad_hi_ref[e], row_body, 0)
            lax.fori_loop(nu_ref[0], MOE_UNITS, unit_body, 0)

        over_gaps(lambda cp: cp.start())
        over_gaps(lambda cp: cp.wait())

    base = i * (ROWS_PER_UNIT * TOP_K)

    def start(r, c):
        for k in range(TOP_K):
            _row_copy(hp_ref, r, xs_hbm, dest_ref[base + TOP_K * r + k], sem).start(priority=k % 2)
        return c

    def wait(r, c):
        for k in range(TOP_K):
            _row_copy(hp_ref, r, xs_hbm, 0, sem).wait()
        return c

    lax.fori_loop(0, ROWS_PER_UNIT, start, 0, unroll=ROW_DMA_UNROLL // TOP_K)
    lax.fori_loop(0, ROWS_PER_UNIT, wait, 0, unroll=ROW_DMA_UNROLL // TOP_K)


def _moe_dispatch(dest, pad_lo, pad_hi, total_units, h_packed):
    grid_spec = pltpu.PrefetchScalarGridSpec(
        num_scalar_prefetch=4,
        grid=(N_UNITS,),
        in_specs=[pl.BlockSpec((ROWS_PER_UNIT, PACKED_WIDTH), lambda i, *_: (i, 0))],
        out_specs=pl.BlockSpec(memory_space=pl.ANY),
        scratch_shapes=[pltpu.VMEM((ROWS_PER_UNIT, PACKED_WIDTH), jnp.uint32),
                        pltpu.SemaphoreType.DMA(()), pltpu.SemaphoreType.DMA(())],
    )
    return pl.pallas_call(
        _moe_dispatch_body,
        grid_spec=grid_spec,
        out_shape=jax.ShapeDtypeStruct((MOE_ROWS, PACKED_WIDTH), jnp.uint32),
        compiler_params=_params("arbitrary"),
        name="moe_dispatch",
    )(dest, pad_lo, pad_hi, total_units, h_packed)


def _moe_gmm_body(te_ref, tu_ref, tn_ref, nu_ref, xs_hbm, wg_ref, wu_ref, wd_ref, ys_hbm,
                  xsc, acc, wgu_b, wd_b, stage, sem_in, sem_out):
    t = pl.program_id(0)
    j = pl.program_id(1)
    nj = pl.num_programs(1)
    n = tn_ref[t]
    u0 = tu_ref[t]
    tf = MOE_FF_TILE

    @pl.when((j == 0) & (n > 0))
    def _():
        unit_in = lambda u, s: pltpu.make_async_copy(
            xs_hbm.at[pl.ds(pl.multiple_of((u0 + u) * ROWS_PER_UNIT, ROWS_PER_UNIT), ROWS_PER_UNIT), :],
            stage.at[s], sem_in.at[s])
        unit_in(0, 0).start()

        def load(u, c):
            s = u & 1

            @pl.when(u + 1 < n)
            def _():
                unit_in(u + 1, 1 - s).start()

            rows = pl.ds(pl.multiple_of(u * ROWS_PER_UNIT, ROWS_PER_UNIT), ROWS_PER_UNIT)
            unit_in(u, s).wait()
            lo, hi = _unpack_bf16_pairs(stage[s])
            xsc[rows, :PACKED_WIDTH] = lo
            xsc[rows, PACKED_WIDTH:] = hi
            return c

        lax.fori_loop(0, n, load, 0)

    t_prev = jnp.maximum(t - 1, 0)

    @pl.when((j == 0) & (t > 0) & (tn_ref[t_prev] > 0))
    def _():
        def wait(u, c):
            _unit_copy(acc, u, ys_hbm, u, sem_out).wait()
            return c

        lax.fori_loop(0, tn_ref[t_prev], wait, 0)

    @pl.when((j == 0) & (n > 0))
    def _():
        def zero(u, c):
            acc[pl.ds(pl.multiple_of(u * ROWS_PER_UNIT, ROWS_PER_UNIT), ROWS_PER_UNIT), :] = jnp.zeros(
                (ROWS_PER_UNIT, D_MODEL), F32)
            return c

        lax.fori_loop(0, n, zero, 0)

    @pl.when((j == 0) & (t == 0))
    def _():
        def start(u, c):
            _unit_copy(acc, 0, ys_hbm, u, sem_out).start()
            return c

        def wait(u, c):
            _unit_copy(acc, 0, ys_hbm, u, sem_out).wait()
            return c

        lax.fori_loop(nu_ref[0], MOE_UNITS, start, 0)
        lax.fori_loop(nu_ref[0], MOE_UNITS, wait, 0)

    @pl.when(n > 0)
    def _():
        def cast_weights():
            wgu_b[:, :tf] = wg_ref[...].astype(BF16)
            wgu_b[:, tf:] = wu_ref[...].astype(BF16)
            wd_b[...] = wd_ref[...].astype(BF16)

        def unit_rows(r0, units):
            rs = pl.ds(r0, units * ROWS_PER_UNIT)
            gu = _dot(xsc[rs, :], wgu_b[...])
            a = (_silu(gu[:, :tf]) * gu[:, tf:]).astype(BF16)
            acc[rs, :] += _dot(a, wd_b[...])

        def units_from(u_first, k, step):
            for h in range(0, k, step):
                unit_rows(pl.multiple_of((u_first + h) * ROWS_PER_UNIT, ROWS_PER_UNIT), min(step, k - h))

        @pl.when(n == MOE_TILE_UNITS)
        def _():
            cast_weights()
            units_from(0, MOE_TILE_UNITS, FULL_TILE_UNITS_PER_DOT)

        pl.when(n != MOE_TILE_UNITS)(cast_weights)
        for b in reversed(range(MOE_TILE_UNITS.bit_length())):
            size = 1 << b
            done = lax.shift_left(lax.shift_right_logical(n, b + 1), b + 1)
            pl.when((n != MOE_TILE_UNITS) & ((n & size) != 0))(
                functools.partial(units_from, done, size, UNITS_PER_DOT))

    @pl.when((j == nj - 1) & (n > 0))
    def _():
        def start(u, c):
            _unit_copy(acc, u, ys_hbm, u0 + u, sem_out).start()
            return c

        def wait(u, c):
            _unit_copy(acc, u, ys_hbm, u0 + u, sem_out).wait()
            return c

        lax.fori_loop(0, n, start, 0)

        @pl.when(t == pl.num_programs(0) - 1)
        def _():
            lax.fori_loop(0, n, wait, 0)


def _moe_gmm(tile_expert, tile_unit0, tile_nunits, total_units, xs, w_gu, w_down, lm):
    nj = D_FF_EXPERT // MOE_FF_TILE
    jj = lambda t, j, tn: jnp.where(tn[t] > 0, j, nj - 1)
    tile_rows = MOE_TILE_UNITS * ROWS_PER_UNIT
    grid_spec = pltpu.PrefetchScalarGridSpec(
        num_scalar_prefetch=4,
        grid=(MOE_TILES, nj),
        in_specs=[pl.BlockSpec(memory_space=pl.ANY),
                  pl.BlockSpec((None, None, D_MODEL, MOE_FF_TILE),
                               lambda t, j, te, tu, tn, nu: (lm, te[t], 0, jj(t, j, tn))),
                  pl.BlockSpec((None, None, D_MODEL, MOE_FF_TILE),
                               lambda t, j, te, tu, tn, nu: (lm, te[t], 0, nj + jj(t, j, tn))),
                  pl.BlockSpec((None, None, MOE_FF_TILE, D_MODEL),
                               lambda t, j, te, tu, tn, nu: (lm, te[t], jj(t, j, tn), 0))],
        out_specs=pl.BlockSpec(memory_space=pl.ANY),
        scratch_shapes=[pltpu.VMEM((tile_rows, D_MODEL), BF16), pltpu.VMEM((tile_rows, D_MODEL), F32),
                        pltpu.VMEM((D_MODEL, 2 * MOE_FF_TILE), BF16), pltpu.VMEM((MOE_FF_TILE, D_MODEL), BF16),
                        pltpu.VMEM((2, ROWS_PER_UNIT, PACKED_WIDTH), jnp.uint32),
                        pltpu.SemaphoreType.DMA((2,)), pltpu.SemaphoreType.DMA(())],
    )
    return pl.pallas_call(
        _moe_gmm_body,
        grid_spec=grid_spec,
        out_shape=jax.ShapeDtypeStruct((MOE_ROWS, D_MODEL), F32),
        compiler_params=_params("arbitrary", "arbitrary"),
        name="moe_gmm",
    )(tile_expert, tile_unit0, tile_nunits, total_units, xs, w_gu, w_gu, w_down)


def _moe_combine_body(dest_ref, ys_hbm, rw_ref, x_ref, mpg_ref, mpn_ref, gs_ref, shs_ref, scs_ref, lg_ref, lb_ref,
                      *rest, pg, psh, psc, final):
    if final:
        xo_prompt, xo_sample, buf, sem = rest
        xo_ref, ho_ref = (xo_prompt, xo_sample), None
    else:
        xo_ref, ho_ref, buf, sem = rest
    i = pl.program_id(0)
    slot = i & 1

    def issue(unit, s):
        base = unit * (ROWS_PER_UNIT * TOP_K)

        def start(r, c):
            for k in range(TOP_K):
                _row_copy(ys_hbm, dest_ref[base + TOP_K * r + k], buf.at[s, k], r, sem.at[s]).start(priority=k % 2)
            return c

        lax.fori_loop(0, ROWS_PER_UNIT, start, 0, unroll=ROW_DMA_UNROLL // TOP_K)

    @pl.when(i == 0)
    def _():
        issue(0, 0)

    @pl.when(i + 1 < pl.num_programs(0))
    def _():
        issue(i + 1, 1 - slot)

    def wait(r, c):
        for k in range(TOP_K):
            _row_copy(ys_hbm, 0, buf.at[slot, k], r, sem.at[slot]).wait()
        return c

    lax.fori_loop(0, ROWS_PER_UNIT, wait, 0, unroll=ROW_DMA_UNROLL // TOP_K)
    y_of = lambda prompt, rs: rw_ref[rs, 0:1] * buf[slot, 0, rs, :] + rw_ref[rs, 1:2] * buf[slot, 1, rs, :]
    _finish_dispatch(i, x_ref, y_of, pg, psh, psc, mpg_ref, mpn_ref, gs_ref, shs_ref, scs_ref,
                     lg_ref, lb_ref, xo_ref, ho_ref, None)


def _moe_combine(dest, ys, route_w, x, modp, mods, ln_g, ln_b, layer, next_layer, pg, psh, psc, final):
    unit = lambda w: pl.BlockSpec((ROWS_PER_UNIT, w), lambda i, d: (i, 0))
    if final:
        out_specs = [_prompt_unit_spec(D_MODEL), _sample_rows_spec(D_MODEL)]
        out_shape = [jax.ShapeDtypeStruct((M_PROMPT, D_MODEL), F32), jax.ShapeDtypeStruct((M_SAMPLE, D_MODEL), F32)]
    else:
        out_specs = [unit(D_MODEL), unit(D_MODEL)]
        out_shape = [jax.ShapeDtypeStruct((M_ROWS, D_MODEL), F32), jax.ShapeDtypeStruct((M_ROWS, D_MODEL), BF16)]
    grid_spec = pltpu.PrefetchScalarGridSpec(
        num_scalar_prefetch=1,
        grid=(N_UNITS,),
        in_specs=[pl.BlockSpec(memory_space=pl.ANY), unit(TOP_K), unit(D_MODEL)]
        + _epilogue_specs(layer, next_layer, pg, psh, psc),
        out_specs=out_specs,
        scratch_shapes=[pltpu.VMEM((2, TOP_K, ROWS_PER_UNIT, D_MODEL), F32), pltpu.SemaphoreType.DMA((2,))],
    )
    return pl.pallas_call(
        functools.partial(_moe_combine_body, pg=pg, psh=psh, psc=psc, final=final),
        grid_spec=grid_spec,
        out_shape=out_shape,
        compiler_params=_params("arbitrary"),
        name="moe_combine",
    )(dest, ys, route_w, x, modp, modp, mods, mods, mods, ln_g.reshape(1, D_MODEL), ln_b.reshape(1, D_MODEL))


def _moe_plan(route_i):
    i32 = jnp.int32
    e_flat = route_i.reshape(-1)
    onehot = (e_flat[:, None] == jnp.arange(N_EXPERTS, dtype=i32)[None, :]).astype(i32)
    csum = jnp.cumsum(onehot, axis=0)
    rank = jnp.sum(csum * onehot, axis=1) - 1
    counts = csum[-1]
    units_e = (counts + ROWS_PER_UNIT - 1) // ROWS_PER_UNIT
    unit_end = jnp.cumsum(units_e)
    unit_start = unit_end - units_e
    dest = (unit_start[e_flat] * ROWS_PER_UNIT + rank).astype(i32)
    pad_lo = (unit_start * ROWS_PER_UNIT + counts).astype(i32)
    pad_hi = (unit_end * ROWS_PER_UNIT).astype(i32)
    total_units = unit_end[-1:].astype(i32)
    tiles_e = (units_e + MOE_TILE_UNITS - 1) // MOE_TILE_UNITS
    tile_end = jnp.cumsum(tiles_e)
    tile_start = tile_end - tiles_e
    n_tiles = tile_end[-1]
    t_ids = jnp.arange(MOE_TILES, dtype=i32)
    t_eff = jnp.minimum(t_ids, n_tiles - 1)
    tile_expert = jnp.sum((t_eff[:, None] >= tile_end[None, :]).astype(i32), axis=1)
    k = t_eff - tile_start[tile_expert]
    tile_unit0 = unit_start[tile_expert] + k * MOE_TILE_UNITS
    units_left = units_e[tile_expert] - k * MOE_TILE_UNITS
    tile_nunits = jnp.where(t_ids < n_tiles, jnp.minimum(units_left, MOE_TILE_UNITS), 0)
    return (dest, pad_lo, pad_hi, total_units, tile_expert.astype(i32), tile_unit0.astype(i32),
            tile_nunits.astype(i32))


def kernel(x_prompt, x_sample, cache_win_k, cache_win_v, c_prompt, c_sample, w_ada, b_ada, w_in, b_in, v_norm_g, v_norm_b, w_spatial, b_spatial, attn_sinks, w_out, b_out, ln1_g, ln1_b, ln2_g, ln2_b, w_ffn_gu, w_ffn_down, w_router, w_exp_gu, w_exp_down):
    c_rows = jnp.concatenate([c_prompt, c_sample, jnp.zeros((ADA_ROWS - BATCH - DEC_BATCH, D_MODEL), F32)], axis=0)
    mod = _adaln(c_rows, w_ada, b_ada)
    modp = mod[:, :BATCH].reshape(DEPTH, BATCH, 6, D_MODEL)
    mods = jnp.repeat(mod[:, BATCH:BATCH + DEC_BATCH], DEC_SEQ, axis=1)

    assert DEPTH % 2 == 0, "the last layer is a mixture-of-experts layer; its combine kernel emits the outputs"
    h, x = _modulate0(x_prompt.reshape(M_PROMPT, D_MODEL), x_sample.reshape(M_SAMPLE, D_MODEL), modp, mods)

    rows = GQA * DEC_SEQ
    kp_list, vp_list, ks_list, vs_list, cv_list = [], [], [], [], []
    for l in range(DEPTH):
        nl = min(l + 1, DEPTH - 1)
        u, v, q, kv = _proj(h, w_in, l, b_in[l], v_norm_g[l], v_norm_b[l])
        b_s_t = b_spatial[l].T
        mix_p = _mix_prompt(attn_sinks[l].reshape(-1), u, v, q, kv, w_spatial[l], b_s_t)

        u_s = u[M_PROMPT:].reshape(DEC_BATCH, DEC_SEQ, A_WIDTH)
        v_s = v[M_PROMPT:].reshape(DEC_BATCH, DEC_SEQ, A_WIDTH)
        kv_s = kv[M_PROMPT:].reshape(DEC_BATCH, DEC_SEQ, 2 * KV_WIDTH)
        q_s = q[M_PROMPT:].reshape(DEC_BATCH, DEC_SEQ, N_KV_HEADS, GQA, HEAD_DIM)
        q_s = q_s.transpose(0, 2, 3, 1, 4).reshape(DEC_BATCH, N_KV_HEADS, rows, HEAD_DIM)
        ck = cache_win_k[l].reshape(DEC_BATCH, WINDOW, KV_WIDTH)
        cv = cache_win_v[l].reshape(DEC_BATCH, WINDOW, KV_WIDTH)
        sink_rows = jnp.repeat(attn_sinks[l], DEC_SEQ, axis=1).reshape(N_KV_HEADS, rows, 1)
        a_s, o_s = _mix_sample(u_s, v_s, q_s, kv_s, ck, cv, w_spatial[l], b_s_t, sink_rows)
        o_s = o_s.reshape(DEC_BATCH, N_KV_HEADS, GQA, DEC_SEQ, HEAD_DIM).transpose(0, 3, 1, 2, 4)
        mix_s = jnp.concatenate([a_s.reshape(M_SAMPLE, A_WIDTH), o_s.reshape(M_SAMPLE, B_WIDTH)], axis=1)

        y = (mix_p, mix_s, w_out, b_out[l])
        if l % 2 == 0:
            x, h = _epilogue(x, y, modp, mods, ln1_g[l], ln1_b[l], l, l, P_GATE1, P_SHIFT2, P_SCALE2)
            ld = l // 2
            wg, wu, wd = _ffn_prep(w_ffn_gu, w_ffn_down, ld)
            f = _ffn(h, wg, wu, wd)
            x, h = _epilogue(x, f, modp, mods, ln2_g[l], ln2_b[l], l, nl, P_GATE2, P_SHIFT1, P_SCALE1)
        else:
            lm = l // 2
            x, h, h_packed, route_i, route_w = _epilogue(x, y, modp, mods, ln1_g[l], ln1_b[l], l, l,
                                                      P_GATE1, P_SHIFT2, P_SCALE2, w_router=w_router[lm])
            dest, pad_lo, pad_hi, total_units, tile_expert, tile_unit0, tile_nunits = _moe_plan(route_i)
            xs = _moe_dispatch(dest, pad_lo, pad_hi, total_units, h_packed)
            ys = _moe_gmm(tile_expert, tile_unit0, tile_nunits, total_units, xs, w_exp_gu, w_exp_down, lm)
            x, h = _moe_combine(dest, ys, route_w, x, modp, mods, ln2_g[l], ln2_b[l], l, nl,
                                P_GATE2, P_SHIFT1, P_SCALE1, final=(l == DEPTH - 1))

        kv_p = kv[:M_PROMPT].reshape(BATCH, SEQ, 2 * KV_WIDTH)[:, SEQ - WINDOW:]
        kp_list.append(kv_p[..., :KV_WIDTH].reshape(BATCH, WINDOW, N_KV_HEADS, HEAD_DIM))
        vp_list.append(kv_p[..., KV_WIDTH:].reshape(BATCH, WINDOW, N_KV_HEADS, HEAD_DIM))
        k_new = kv_s[..., :KV_WIDTH].reshape(DEC_BATCH, DEC_SEQ, N_KV_HEADS, HEAD_DIM)
        v_new = kv_s[..., KV_WIDTH:].reshape(DEC_BATCH, DEC_SEQ, N_KV_HEADS, HEAD_DIM)
        ks_list.append(jnp.concatenate([cache_win_k[l][:, DEC_SEQ:], k_new], axis=1))
        vs_list.append(jnp.concatenate([cache_win_v[l][:, DEC_SEQ:], v_new], axis=1))
        cv_list.append(v_s.astype(F32))

    y_prompt = x.reshape(BATCH, SEQ, D_MODEL)
    y_sample = h.reshape(DEC_BATCH, DEC_SEQ, D_MODEL)
    return (y_prompt, y_sample, jnp.stack(kp_list), jnp.stack(vp_list), jnp.stack(ks_list), jnp.stack(vs_list),
            jnp.stack(cv_list))
```
